```python
import jax, jax.numpy as jnp
from jax import lax
import numpy as np

D_MODEL = 1024
BATCH = 8
SEQ = 2048
DEPTH = 2
DEC_BATCH = 128
DEC_SEQ = 4
PAST_LEN = 16384
PAGE_SIZE = 128

EXPAND = 2
MIX_WIDTH = EXPAND * D_MODEL
RW_HEAD = 64
RW_HEADS = MIX_WIDTH // RW_HEAD
LORA_W = 64
LORA_A = 64
SHIFT_COLS = 3 * MIX_WIDTH + LORA_W + LORA_A
RW_IN_COLS = SHIFT_COLS + MIX_WIDTH
GN_EPS = 64e-5
CHUNK = 128
GM_GROUPS = 8
GM_GROUP_DIM = MIX_WIDTH // GM_GROUPS
GM_IN_COLS = 3 * MIX_WIDTH
N_RWKV = (DEPTH + 1) // 2
N_GMLP = DEPTH // 2
NORM_EPS = 1e-6
LN_EPS = 1e-5

kernel_name = "hybrid_rwkv7_gmlp_decode_step"


def _rmsnorm(x, g):
    xf = x.astype(jnp.float32)
    y = xf * lax.rsqrt(jnp.mean(xf * xf, axis=-1, keepdims=True) + NORM_EPS)
    return y.astype(x.dtype) * g


def _wkv7_scan(r, decay, k, v, kk, a, s0):
    xs = tuple(jnp.swapaxes(arr, 0, 1) for arr in (r, decay, k, v, kk, a))

    def step(S, inp):
        r_t, w_t, k_t, v_t, kk_t, a_t = inp
        sa = jnp.einsum('bhvk,bhk->bhv', S, -kk_t)
        S = (S * w_t[:, :, None, :]
             + sa[..., None] * (kk_t * a_t)[:, :, None, :]
             + v_t[..., None] * k_t[:, :, None, :])
        return S, jnp.einsum('bhvk,bhk->bhv', S, r_t)

    S, ys = lax.scan(step, s0, xs)
    return jnp.swapaxes(ys, 0, 1), S


def _rwkv7_mixer(h, shift_prev, wkv0, w_in, mu, w0, w2, a0, a2, k_k, k_a, r_k, lnx_g, lnx_b, w_out):
    bt, t, _ = h.shape
    proj = h @ w_in
    sh, z = proj[..., :SHIFT_COLS], proj[..., SHIFT_COLS:]
    prev = jnp.concatenate([shift_prev[:, None, :].astype(sh.dtype), sh[:, :-1]], axis=1)
    xm = sh + (prev - sh) * mu
    e = MIX_WIDTH
    r, k, v = xm[..., :e], xm[..., e:2 * e], xm[..., 2 * e:3 * e]
    wd = xm[..., 3 * e:3 * e + LORA_W]
    ad = xm[..., 3 * e + LORA_W:]
    wf = (w0 + jnp.tanh(wd) @ w2).astype(jnp.float32)
    w_log = -jax.nn.softplus(-wf) - 0.5
    decay = jnp.exp(-jnp.exp(w_log))
    a = jax.nn.sigmoid((a0 + ad @ a2).astype(jnp.float32))

    hs = lambda u: u.astype(jnp.float32).reshape(bt, t, RW_HEADS, RW_HEAD)
    hp = lambda p: p.astype(jnp.float32).reshape(RW_HEADS, RW_HEAD)
    r, k, v, a, decay = hs(r), hs(k), hs(v), hs(a), hs(decay)
    kk = k * hp(k_k)
    kk = kk / jnp.maximum(jnp.sqrt(jnp.sum(kk * kk, axis=-1, keepdims=True)), 1e-12)
    k = k * (1.0 + (a - 1.0) * hp(k_a))

    y, s_fin = _wkv7_scan(r, decay, k, v, kk, a, wkv0.astype(jnp.float32))
    mean = jnp.mean(y, axis=-1, keepdims=True)
    var = jnp.mean(jnp.square(y - mean), axis=-1, keepdims=True)
    yn = ((y - mean) * lax.rsqrt(var + GN_EPS)).reshape(bt, t, e)
    yn = yn * lnx_g.astype(jnp.float32) + lnx_b.astype(jnp.float32)
    bonus = jnp.sum(r * k * r_k.astype(jnp.float32), axis=-1, keepdims=True) * v
    o = (yn + bonus.reshape(bt, t, e)).astype(h.dtype) * jax.nn.silu(z)
    return o @ w_out, sh[:, -1], s_fin.astype(wkv0.dtype)


def _gmlp_mixer(h, w_in, v_g, v_b, ws, bs, w_out):
    bt, t, _ = h.shape
    e = MIX_WIDTH
    proj = h @ w_in
    u = jax.nn.gelu(proj[..., :e])
    v = jax.nn.gelu(proj[..., e:2 * e])
    z = proj[..., 2 * e:]
    vf = v.astype(jnp.float32)
    vm_ = jnp.mean(vf, axis=-1, keepdims=True)
    vv = jnp.mean(jnp.square(vf - vm_), axis=-1, keepdims=True)
    v = ((vf - vm_) * lax.rsqrt(vv + LN_EPS)).astype(h.dtype) * v_g + v_b
    n_chunks = -(-t // CHUNK)
    pad = n_chunks * CHUNK - t
    vp = jnp.pad(v, ((0, 0), (0, pad), (0, 0))).reshape(bt, n_chunks, CHUNK, GM_GROUPS, GM_GROUP_DIM)
    mask = jnp.tril(jnp.ones((CHUNK, CHUNK), dtype=bool))
    wm = jnp.where(mask[None], ws, jnp.zeros_like(ws))
    mixed = jnp.einsum('gts,bcsgd->bctgd', wm, vp) + bs.T[None, None, :, :, None]
    mixed = mixed.reshape(bt, n_chunks * CHUNK, e)[:, :t]
    o = u * mixed * jax.nn.silu(z)
    return o @ w_out, v


def setup_inputs(seed: int = 0) -> dict:
    key = jax.random.key(seed)
    ks = jax.random.split(key, 26)
    f32 = jnp.float32
    nrm = lambda k, shape, s: jax.random.normal(k, shape, f32) * s
    e = MIX_WIDTH
    return {
        "x_prompt": nrm(ks[0], (BATCH, SEQ, D_MODEL), 1.0),
        "x_sample": nrm(ks[1], (DEC_BATCH, DEC_SEQ, D_MODEL), 1.0),
        "state_shift": nrm(ks[2], (N_RWKV, DEC_BATCH, SHIFT_COLS), 1.0),
        "state_wkv": nrm(ks[3], (N_RWKV, DEC_BATCH, RW_HEADS, RW_HEAD, RW_HEAD), 0.3),
        "norm_g": 1.0 + nrm(ks[4], (DEPTH, D_MODEL), 0.05),
        "norm_f": 1.0 + nrm(ks[5], (D_MODEL,), 0.05),
        "rw_in": nrm(ks[6], (N_RWKV, D_MODEL, RW_IN_COLS), D_MODEL ** -0.5),
        "rw_mu": jax.random.uniform(ks[7], (N_RWKV, SHIFT_COLS), f32),
        "rw_w0": jax.random.uniform(ks[8], (N_RWKV, e), f32, -6.0, 1.0),
        "rw_w2": nrm(ks[9], (N_RWKV, LORA_W, e), 0.1),
        "rw_a0": nrm(ks[10], (N_RWKV, e), 0.5),
        "rw_a2": nrm(ks[11], (N_RWKV, LORA_A, e), 0.1),
        "rw_kk": 0.85 + nrm(ks[12], (N_RWKV, e), 0.05),
        "rw_ka": 1.0 + nrm(ks[13], (N_RWKV, e), 0.05),
        "rw_rk": nrm(ks[14], (N_RWKV, RW_HEADS, RW_HEAD), 0.1),
        "rw_lnx_g": 1.0 + nrm(ks[15], (N_RWKV, e), 0.05),
        "rw_lnx_b": nrm(ks[16], (N_RWKV, e), 0.05),
        "rw_out": nrm(ks[17], (N_RWKV, e, D_MODEL), e ** -0.5),
        "gm_in": nrm(ks[18], (N_GMLP, D_MODEL, GM_IN_COLS), D_MODEL ** -0.5),
        "gm_vg": 1.0 + nrm(ks[19], (N_GMLP, e), 0.05),
        "gm_vb": nrm(ks[20], (N_GMLP, e), 0.05),
        "gm_ws": nrm(ks[21], (N_GMLP, GM_GROUPS, CHUNK, CHUNK), CHUNK ** -0.5),
        "gm_bs": 1.0 + nrm(ks[22], (N_GMLP, GM_GROUPS, CHUNK), 0.1),
        "gm_out": nrm(ks[23], (N_GMLP, e, D_MODEL), e ** -0.5),
    }


def reference(x_prompt, x_sample, state_shift, state_wkv, norm_g, norm_f,
              rw_in, rw_mu, rw_w0, rw_w2, rw_a0, rw_a2, rw_kk, rw_ka, rw_rk, rw_lnx_g, rw_lnx_b, rw_out,
              gm_in, gm_vg, gm_vb, gm_ws, gm_bs, gm_out):
    xp, xs = x_prompt, x_sample
    p_shift, p_wkv, s_shift, s_wkv, s_v = [], [], [], [], []
    for i in range(DEPTH):
        j = i // 2
        hp = _rmsnorm(xp, norm_g[i])
        hs = _rmsnorm(xs, norm_g[i])
        if i % 2 == 0:
            prm = (rw_in[j], rw_mu[j], rw_w0[j], rw_w2[j], rw_a0[j], rw_a2[j], rw_kk[j], rw_ka[j],
                   rw_rk[j], rw_lnx_g[j], rw_lnx_b[j], rw_out[j])
            sh0 = jnp.zeros((xp.shape[0], SHIFT_COLS), xp.dtype)
            wkv0 = jnp.zeros((xp.shape[0], RW_HEADS, RW_HEAD, RW_HEAD), state_wkv.dtype)
            op, shp, wkp = _rwkv7_mixer(hp, sh0, wkv0, *prm)
            os_, shs, wks = _rwkv7_mixer(hs, state_shift[j], state_wkv[j], *prm)
            p_shift.append(shp); p_wkv.append(wkp); s_shift.append(shs); s_wkv.append(wks)
        else:
            prm = (gm_in[j], gm_vg[j], gm_vb[j], gm_ws[j], gm_bs[j], gm_out[j])
            op, _ = _gmlp_mixer(hp, *prm)
            os_, vs = _gmlp_mixer(hs, *prm)
            s_v.append(vs)
        xp = xp + op
        xs = xs + os_
    y_prompt = _rmsnorm(xp, norm_f)
    y_sample = _rmsnorm(xs, norm_f)
    prompt_shift = jnp.stack(p_shift)
    prompt_wkv = jnp.stack(p_wkv)
    sample_shift = jnp.stack(s_shift)
    sample_wkv = jnp.stack(s_wkv)
    sample_v = jnp.stack(s_v)
    return (y_prompt, y_sample, prompt_shift, prompt_wkv, sample_shift, sample_wkv, sample_v)
```

```python
import functools
import math

import jax
import jax.numpy as jnp
from jax import lax
from jax.experimental import pallas as pl
from jax.experimental.pallas import tpu as pltpu

F32 = jnp.float32
BF16 = jnp.bfloat16

HEAD = 64
QUAD = 4
QW = QUAD * HEAD
LORA = 64
NORM_EPS = 1e-6
LN_EPS = 1e-5
GN_EPS = 64e-5
GM_CHUNK = 128
GM_GROUPS = 8
VMEM_LIMIT = 56 * 1024 * 1024


def _split2(x):
    hi = x.astype(BF16)
    lo = (x - hi.astype(F32)).astype(BF16)
    return hi, lo


def _split3(x):
    hi = x.astype(BF16)
    r1 = x - hi.astype(F32)
    mid = r1.astype(BF16)
    lo = (r1 - mid.astype(F32)).astype(BF16)
    return hi, mid, lo


_NN = (((1,), (0,)), ((), ()))
_NT = (((1,), (1,)), ((), ()))
_TN = (((0,), (0,)), ((), ()))


def _mm(a, b, dims=_NN, passes=1):
    if passes == 1:
        return lax.dot_general(a.astype(BF16), b.astype(BF16), dims, preferred_element_type=F32)
    ah, al = _split2(a)
    bh, bl = _split2(b)
    d = functools.partial(lax.dot_general, dimension_numbers=dims, preferred_element_type=F32)
    return d(ah, bh) + (d(ah, bl) + d(al, bh))


def _mm_exact_rhs(a, b_bf16, dims=_NN):
    d = functools.partial(lax.dot_general, dimension_numbers=dims, preferred_element_type=F32)
    h, m, l = _split3(a)
    return d(h, b_bf16) + (d(m, b_bf16) + d(l, b_bf16))


def _mm_exact_lhs(a_bf16, b, dims=_NN):
    d = functools.partial(lax.dot_general, dimension_numbers=dims, preferred_element_type=F32)
    h, m, l = _split3(b)
    return d(a_bf16, h) + (d(a_bf16, m) + d(a_bf16, l))


def _iota(shape, dim):
    return lax.broadcasted_iota(jnp.int32, shape, dim)


def _sigmoid(x):
    return 1.0 / (1.0 + jnp.exp(-x))


def _norm_proj_kernel(x_ref, g_ref, w_ref, o_ref):
    x = x_ref[...]
    y = x * lax.rsqrt(jnp.mean(x * x, axis=-1, keepdims=True) + NORM_EPS)
    h = (y * g_ref[...]).astype(BF16)
    o_ref[...] = jnp.dot(h, w_ref[...], preferred_element_type=F32)


def _norm_proj(x2d, g, w_bf16, tm):
    n, d = x2d.shape
    cols = w_bf16.shape[1]
    return pl.pallas_call(
        _norm_proj_kernel,
        grid=(n // tm,),
        in_specs=[pl.BlockSpec((tm, d), lambda i: (i, 0)),
                  pl.BlockSpec((1, d), lambda i: (0, 0)),
                  pl.BlockSpec((d, cols), lambda i: (0, 0), pipeline_mode=pl.Buffered(1))],
        out_specs=pl.BlockSpec((tm, cols), lambda i: (i, 0)),
        out_shape=jax.ShapeDtypeStruct((n, cols), F32),
        compiler_params=pltpu.CompilerParams(dimension_semantics=("arbitrary",), vmem_limit_bytes=VMEM_LIMIT),
        name="norm_proj",
    )(x2d, g.reshape(1, d), w_bf16)


def _headsum(xs, ones_bd):
    e = xs[0].shape[1]
    nq = e // QW
    rows = [x[:, q * QW:(q + 1) * QW] for x in xs for q in range(nq)]
    stacked = jnp.concatenate(rows, axis=0)
    s = _mm_exact_rhs(stacked, ones_bd)
    c = xs[0].shape[0]
    outs = []
    for i in range(len(xs)):
        outs.append(jnp.concatenate([s[(i * nq + q) * c:(i * nq + q + 1) * c] for q in range(nq)], axis=1))
    return outs


def _wkv_kernel(p_ref, sp_ref, s0_ref, mu_ref, w0_ref, w2_ref, a0_ref, a2_ref, kk_ref, ka_ref, rk_ref,
                lng_ref, lnb_ref, o_ref, sho_ref, so_ref, carry_ref, st_ref, *, chunk, t_valid, n_levels, passes):
    c_idx = pl.program_id(1)
    n_chunks = pl.num_programs(1)
    C = chunk
    TL = QUAD * C
    E = o_ref.shape[2]
    NQ = E // QW
    SHIFT = 3 * E + 2 * LORA

    rep = (_iota((HEAD, QW), 1) % HEAD == _iota((HEAD, QW), 0)).astype(BF16)
    rep_t = (_iota((QW, HEAD), 0) % HEAD == _iota((QW, HEAD), 1)).astype(BF16)
    mask_bd = (_iota((QW, QW), 0) // HEAD) == (_iota((QW, QW), 1) // HEAD)
    ones_bd = mask_bd.astype(BF16)

    @pl.when(c_idx == 0)
    def _init():
        carry_ref[...] = sp_ref[0]
        for q in range(NQ):
            st_ref[q] = jnp.where(mask_bd, _mm_exact_rhs(s0_ref[0, q], rep), 0.0)

    p = p_ref[0]
    sh = p[:, :SHIFT]
    z = p[:, SHIFT:]
    row1 = _iota((C, 1), 0)
    prev = jnp.where(row1 == 0, carry_ref[...], pltpu.roll(sh, 1, 0))
    xm = sh + (prev - sh) * mu_ref[...]
    last = sh[t_valid - 1:t_valid]
    carry_ref[...] = last

    @pl.when(c_idx == n_chunks - 1)
    def _shift_out():
        sho_ref[0] = last

    r = xm[:, :E]
    k = xm[:, E:2 * E]
    v = xm[:, 2 * E:3 * E]
    x_lora = xm[:, 3 * E:]
    lane_l = _iota((C, 2 * LORA), 1)
    t_lora = jnp.where(lane_l < LORA, jnp.tanh(x_lora), x_lora)
    wf = w0_ref[...] + _mm(t_lora, w2_ref[...])
    af = a0_ref[...] + _mm(x_lora, a2_ref[...])
    ld = -math.exp(-0.5) * _sigmoid(wf)
    a = _sigmoid(af)

    valid = row1 < t_valid
    if t_valid < C:
        ld = jnp.where(valid, ld, 0.0)
        v = jnp.where(valid, v, 0.0)

    kk0 = k * kk_ref[...]
    k2 = k * (1.0 + (a - 1.0) * ka_ref[...])
    ssq, rkk = _headsum([kk0 * kk0, r * k2 * rk_ref[...]], ones_bd)
    kk = kk0 / jnp.maximum(jnp.sqrt(ssq), 1e-12)
    bonus = rkk * v

    tri = (_iota((C, C), 0) >= _iota((C, C), 1)).astype(BF16)
    cum = _mm_exact_lhs(tri, ld)
    cum_last = cum[C - 1:C]
    g_ex = jnp.exp(cum - ld)
    g_in = jnp.exp(cum)
    g_inv = jnp.exp(-cum)
    g_rem = jnp.exp(cum_last - cum)
    g_tot = jnp.exp(cum_last)
    kka = kk * a
    at = -kk * g_ex
    bt = kka * g_inv
    kt = k2 * g_inv
    rt = r * g_in
    bg = kka * g_rem
    kg = k2 * g_rem
    if t_valid < C:
        at = jnp.where(valid, at, 0.0)
        bt = jnp.where(valid, bt, 0.0)
        kt = jnp.where(valid, kt, 0.0)
        bg = jnp.where(valid, bg, 0.0)
        kg = jnp.where(valid, kg, 0.0)

    mask_hc = (_iota((TL, QW), 0) // C) == (_iota((TL, QW), 1) // HEAD)
    mask_tt = (_iota((TL, TL), 0) // C) == (_iota((TL, TL), 1) // C)
    tok_t = _iota((C, TL), 0)
    tok_j = _iota((C, TL), 1) % C
    strict = tok_j < tok_t
    incl = tok_j <= tok_t
    eye_all = (tok_j == tok_t).astype(F32)

    def stack_hc(x):
        return jnp.where(mask_hc, jnp.concatenate([x] * QUAD, axis=0), 0.0)

    def stack_tt(x):
        return jnp.where(mask_tt, jnp.concatenate([x] * QUAD, axis=0), 0.0)

    mm = functools.partial(_mm, passes=passes)
    ys = []
    for q in range(NQ):
        sl = slice(q * QW, (q + 1) * QW)
        at_q, rt_q, v_q = at[:, sl], rt[:, sl], v[:, sl]
        s_q = st_ref[q]
        ar = jnp.concatenate([at_q, rt_q], axis=0)
        ab = mm(ar, stack_hc(bt[:, sl]), _NT)
        ak = mm(ar, stack_hc(kt[:, sl]), _NT)
        a_ab = jnp.where(strict, ab[:C], 0.0)
        a_rb = jnp.where(incl, ab[C:], 0.0)
        a_ak = jnp.where(strict, ak[:C], 0.0)
        a_rk = jnp.where(incl, ak[C:], 0.0)
        pw = a_ab
        inv = eye_all + a_ab
        for _ in range(n_levels):
            pw = mm(pw, stack_tt(pw))
            inv = inv + mm(inv, stack_tt(pw))
        ars = mm(ar, s_q, _NT)
        v_st = stack_hc(v_q)
        w_q = ars[:C] + mm(a_ak, v_st)
        u_q = mm(inv, stack_hc(w_q))
        y_q = ars[C:] + mm(a_rb, stack_hc(u_q)) + mm(a_rk, v_st)
        upd = mm(u_q, bg[:, sl], _TN) + mm(v_q, kg[:, sl], _TN)
        st_ref[q] = s_q * g_tot[:, sl] + jnp.where(mask_bd, upd, 0.0)
        ys.append(y_q)
    y = jnp.concatenate(ys, axis=1)

    (ysum,) = _headsum([y], ones_bd)
    yc = y - ysum * (1.0 / HEAD)
    (vsum,) = _headsum([yc * yc], ones_bd)
    yn = yc * lax.rsqrt(vsum * (1.0 / HEAD) + GN_EPS) * lng_ref[...] + lnb_ref[...]
    o_ref[0] = (yn + bonus) * (z * _sigmoid(z))

    @pl.when(c_idx == n_chunks - 1)
    def _state_out():
        for q in range(NQ):
            so_ref[0, q] = _mm_exact_rhs(st_ref[q], rep_t)


def _wkv(proj3d, shift_prev, s0, prm, *, chunk, t_valid, n_levels, passes):
    b, t, cols = proj3d.shape
    e = prm["w0"].shape[1]
    shift = cols - e
    nq = e // QW
    nh = e // HEAD
    s0q = s0.reshape(b, nq, QW, HEAD)
    row = lambda n: pl.BlockSpec((1, n), lambda i, j: (0, 0))
    kern = functools.partial(_wkv_kernel, chunk=chunk, t_valid=t_valid, n_levels=n_levels, passes=passes)
    o, sho, so = pl.pallas_call(
        kern,
        grid=(b, t // chunk),
        in_specs=[pl.BlockSpec((1, chunk, cols), lambda i, j: (i, j, 0)),
                  pl.BlockSpec((1, 1, shift), lambda i, j: (i, 0, 0)),
                  pl.BlockSpec((1, nq, QW, HEAD), lambda i, j: (i, 0, 0, 0)),
                  row(shift), row(e),
                  pl.BlockSpec((2 * LORA, e), lambda i, j: (0, 0)),
                  row(e),
                  pl.BlockSpec((2 * LORA, e), lambda i, j: (0, 0)),
                  row(e), row(e), row(e), row(e), row(e)],
        out_specs=[pl.BlockSpec((1, chunk, e), lambda i, j: (i, j, 0)),
                   pl.BlockSpec((1, 1, shift), lambda i, j: (i, 0, 0)),
                   pl.BlockSpec((1, nq, QW, HEAD), lambda i, j: (i, 0, 0, 0))],
        out_shape=[jax.ShapeDtypeStruct((b, t, e), F32),
                   jax.ShapeDtypeStruct((b, 1, shift), F32),
                   jax.ShapeDtypeStruct((b, nq, QW, HEAD), F32)],
        scratch_shapes=[pltpu.VMEM((1, shift), F32),
                        pltpu.VMEM((nq, QW, QW), F32)],
        compiler_params=pltpu.CompilerParams(dimension_semantics=("arbitrary", "arbitrary"),
                                             vmem_limit_bytes=VMEM_LIMIT),
        name="wkv",
    )(proj3d, shift_prev.reshape(b, 1, shift), s0q,
      prm["mu"], prm["w0"], prm["w2p"], prm["a0"], prm["a2p"], prm["kk"], prm["ka"], prm["rk"],
      prm["lng"], prm["lnb"])
    return o, sho.reshape(b, shift), so.reshape(b, nh, HEAD, HEAD)


def _out_residual_kernel(o_ref, w_ref, x_ref, y_ref):
    y_ref[...] = x_ref[...] + jnp.dot(o_ref[...].astype(BF16), w_ref[...], preferred_element_type=F32)


def _out_residual(o2d, w_bf16, x2d, tm):
    n, e = o2d.shape
    d = x2d.shape[1]
    return pl.pallas_call(
        _out_residual_kernel,
        grid=(n // tm,),
        in_specs=[pl.BlockSpec((tm, e), lambda i: (i, 0)),
                  pl.BlockSpec((e, d), lambda i: (0, 0)),
                  pl.BlockSpec((tm, d), lambda i: (i, 0))],
        out_specs=pl.BlockSpec((tm, d), lambda i: (i, 0)),
        out_shape=jax.ShapeDtypeStruct((n, d), F32),
        compiler_params=pltpu.CompilerParams(dimension_semantics=("arbitrary",), vmem_limit_bytes=VMEM_LIMIT),
        name="out_residual",
    )(o2d, w_bf16, x2d)


def _gmlp_kernel(x_ref, g_ref, win_ref, vg_ref, vb_ref, wm_ref, bs_ref, wout_ref, nf_ref, *out_refs, emit_v):
    y_ref = out_refs[0]
    tm = x_ref.shape[0]
    e = vg_ref.shape[1]
    gd = e // GM_GROUPS
    x = x_ref[...]
    h = (x * lax.rsqrt(jnp.mean(x * x, axis=-1, keepdims=True) + NORM_EPS) * g_ref[...]).astype(BF16)
    proj = jnp.dot(h, win_ref[...], preferred_element_type=F32)
    u = jax.nn.gelu(proj[:, :e])
    vf = jax.nn.gelu(proj[:, e:2 * e])
    z = proj[:, 2 * e:]
    vm = jnp.mean(vf, axis=-1, keepdims=True)
    vc = vf - vm
    vv = jnp.mean(vc * vc, axis=-1, keepdims=True)
    vn = vc * lax.rsqrt(vv + LN_EPS) * vg_ref[...] + vb_ref[...]
    if emit_v:
        out_refs[1][...] = vn
    vn16 = vn.astype(BF16)
    causal = _iota((GM_CHUNK, GM_CHUNK), 0) >= _iota((GM_CHUNK, GM_CHUNK), 1)
    rows = []
    for j in range(tm // GM_CHUNK):
        cols = []
        for gi in range(GM_GROUPS):
            wm = jnp.where(causal, wm_ref[gi], 0.0).astype(BF16)
            blk = vn16[j * GM_CHUNK:(j + 1) * GM_CHUNK, gi * gd:(gi + 1) * gd]
            cols.append(jnp.dot(wm, blk, preferred_element_type=F32) + bs_ref[gi])
        rows.append(jnp.concatenate(cols, axis=1))
    mixed = jnp.concatenate(rows, axis=0)
    o = u * mixed * (z * _sigmoid(z))
    x2 = x + jnp.dot(o.astype(BF16), wout_ref[...], preferred_element_type=F32)
    y_ref[...] = x2 * lax.rsqrt(jnp.mean(x2 * x2, axis=-1, keepdims=True) + NORM_EPS) * nf_ref[...]


def _gmlp(x2d, g, win_bf16, vg, vb, wmix, bias, wout_bf16, nf, *, tm, emit_v):
    n, d = x2d.shape
    e = vg.shape[0]
    const2 = lambda shape: pl.BlockSpec(shape, lambda i: (0, 0))
    out_shape = [jax.ShapeDtypeStruct((n, d), F32)]
    out_specs = [pl.BlockSpec((tm, d), lambda i: (i, 0))]
    if emit_v:
        out_shape.append(jax.ShapeDtypeStruct((n, e), F32))
        out_specs.append(pl.BlockSpec((tm, e), lambda i: (i, 0)))
    outs = pl.pallas_call(
        functools.partial(_gmlp_kernel, emit_v=emit_v),
        grid=(n // tm,),
        in_specs=[pl.BlockSpec((tm, d), lambda i: (i, 0)),
                  const2((1, d)),
                  pl.BlockSpec((d, 3 * e), lambda i: (0, 0), pipeline_mode=pl.Buffered(1)),
                  const2((1, e)), const2((1, e)),
                  pl.BlockSpec((GM_GROUPS, GM_CHUNK, GM_CHUNK), lambda i: (0, 0, 0)),
                  pl.BlockSpec((GM_GROUPS, GM_CHUNK, 1), lambda i: (0, 0, 0)),
                  pl.BlockSpec((e, d), lambda i: (0, 0), pipeline_mode=pl.Buffered(1)),
                  const2((1, d))],
        out_specs=out_specs,
        out_shape=out_shape,
        compiler_params=pltpu.CompilerParams(dimension_semantics=("arbitrary",), vmem_limit_bytes=VMEM_LIMIT),
        name="gmlp",
    )(x2d, g.reshape(1, d), win_bf16, vg.reshape(1, e), vb.reshape(1, e), wmix,
      bias.reshape(GM_GROUPS, GM_CHUNK, 1), wout_bf16, nf.reshape(1, d))
    return outs


def _row_tile(n, pref):
    t = pref
    while n % t:
        t //= 2
    return t


def kernel(x_prompt, x_sample, state_shift, state_wkv, norm_g, norm_f, rw_in, rw_mu, rw_w0, rw_w2, rw_a0, rw_a2, rw_kk, rw_ka, rw_rk, rw_lnx_g, rw_lnx_b, rw_out, gm_in, gm_vg, gm_vb, gm_ws, gm_bs, gm_out):
    bp, tp, d = x_prompt.shape
    bs_, ts, _ = x_sample.shape
    e = rw_w0.shape[1]
    nh = e // HEAD
    shift = 3 * e + 2 * LORA
    xp = x_prompt.reshape(bp * tp, d)
    xs = x_sample.reshape(bs_ * ts, d)

    zeros_l = jnp.zeros((LORA, e), F32)
    prm = dict(
        mu=rw_mu[0].reshape(1, shift), w0=rw_w0[0].reshape(1, e), a0=rw_a0[0].reshape(1, e),
        w2p=jnp.concatenate([rw_w2[0], zeros_l], axis=0).astype(BF16),
        a2p=jnp.concatenate([zeros_l, rw_a2[0]], axis=0).astype(BF16),
        kk=rw_kk[0].reshape(1, e), ka=rw_ka[0].reshape(1, e), rk=rw_rk[0].reshape(1, e),
        lng=rw_lnx_g[0].reshape(1, e), lnb=rw_lnx_b[0].reshape(1, e))
    w_in = rw_in[0].astype(BF16)
    w_out = rw_out[0].astype(BF16)

    proj_p = _norm_proj(xp, norm_g[0], w_in, _row_tile(bp * tp, 256)).reshape(bp, tp, shift + e)
    proj_s = _norm_proj(xs, norm_g[0], w_in, _row_tile(bs_ * ts, 256)).reshape(bs_, ts, shift + e)

    chunk_p = 64
    o_p, shift_p, wkv_p = _wkv(proj_p, jnp.zeros((bp, shift), F32), jnp.zeros((bp, nh, HEAD, HEAD), F32), prm,
                               chunk=chunk_p, t_valid=chunk_p, n_levels=5, passes=3)
    chunk_s = 8
    proj_s = jnp.pad(proj_s, ((0, 0), (0, chunk_s - ts), (0, 0)))
    o_s, shift_s, wkv_s = _wkv(proj_s, state_shift[0], state_wkv[0], prm,
                               chunk=chunk_s, t_valid=ts, n_levels=max(1, math.ceil(math.log2(ts)) - 1), passes=3)
    o_s = o_s[:, :ts]

    x1p = _out_residual(o_p.reshape(bp * tp, e), w_out, xp, _row_tile(bp * tp, 512))
    x1s = _out_residual(o_s.reshape(bs_ * ts, e), w_out, xs, _row_tile(bs_ * ts, 512))

    g_in = gm_in[0].astype(BF16)
    g_out = gm_out[0].astype(BF16)
    wm_p = gm_ws[0]
    reps = GM_CHUNK // ts
    eye = jnp.eye(reps, dtype=F32)
    wm_s = jax.vmap(lambda w: jnp.kron(eye, w))(wm_p[:, :ts, :ts])
    bias_s = jnp.tile(gm_bs[0][:, :ts], (1, reps))

    (y_p,) = _gmlp(x1p, norm_g[1], g_in, gm_vg[0], gm_vb[0], wm_p, gm_bs[0], g_out, norm_f,
                   tm=_row_tile(bp * tp, 256), emit_v=False)
    y_s, v_s = _gmlp(x1s, norm_g[1], g_in, gm_vg[0], gm_vb[0], wm_s, bias_s, g_out, norm_f,
                     tm=_row_tile(bs_ * ts, 256), emit_v=True)

    return (y_p.reshape(bp, tp, d), y_s.reshape(bs_, ts, d),
            shift_p[None], wkv_p[None], shift_s[None], wkv_s[None],
            v_s.reshape(1, bs_, ts, e))
```

```python
import functools
import math

import jax
import jax.numpy as jnp
from jax import lax
from jax.experimental import pallas as pl
from jax.experimental.pallas import tpu as pltpu

F32 = jnp.float32
BF16 = jnp.bfloat16

HEAD = 64
QUAD = 4
QW = QUAD * HEAD
LORA = 64
NORM_EPS = 1e-6
LN_EPS = 1e-5
GN_EPS = 64e-5
GM_CHUNK = 128
GM_GROUPS = 8
VMEM_LIMIT = 56 * 1024 * 1024


def _split2(x):
    hi = x.astype(BF16)
    lo = (x - hi.astype(F32)).astype(BF16)
    return hi, lo


def _split3(x):
    hi = x.astype(BF16)
    r1 = x - hi.astype(F32)
    mid = r1.astype(BF16)
    lo = (r1 - mid.astype(F32)).astype(BF16)
    return hi, mid, lo


_NN = (((1,), (0,)), ((), ()))
_NT = (((1,), (1,)), ((), ()))
_TN = (((0,), (0,)), ((), ()))


def _mm(a, b, dims=_NN, passes=1):
    if passes == 1:
        return lax.dot_general(a.astype(BF16), b.astype(BF16), dims, preferred_element_type=F32)
    ah, al = _split2(a)
    bh, bl = _split2(b)
    d = functools.partial(lax.dot_general, dimension_numbers=dims, preferred_element_type=F32)
    return d(ah, bh) + (d(ah, bl) + d(al, bh))


def _mm_exact_rhs(a, b_bf16, dims=_NN):
    d = functools.partial(lax.dot_general, dimension_numbers=dims, preferred_element_type=F32)
    h, m, l = _split3(a)
    return d(h, b_bf16) + (d(m, b_bf16) + d(l, b_bf16))


def _mm_exact_lhs(a_bf16, b, dims=_NN):
    d = functools.partial(lax.dot_general, dimension_numbers=dims, preferred_element_type=F32)
    h, m, l = _split3(b)
    return d(a_bf16, h) + (d(a_bf16, m) + d(a_bf16, l))


def _iota(shape, dim):
    return lax.broadcasted_iota(jnp.int32, shape, dim)


def _sigmoid(x):
    return 1.0 / (1.0 + jnp.exp(-x))


def _norm_proj_kernel(x_ref, g_ref, w_ref, o_ref):
    x = x_ref[...]
    y = x * lax.rsqrt(jnp.mean(x * x, axis=-1, keepdims=True) + NORM_EPS)
    h = (y * g_ref[...]).astype(BF16)
    o_ref[...] = jnp.dot(h, w_ref[...], preferred_element_type=F32)


def _norm_proj(x2d, g, w_bf16, tm):
    n, d = x2d.shape
    cols = w_bf16.shape[1]
    return pl.pallas_call(
        _norm_proj_kernel,
        grid=(n // tm,),
        in_specs=[pl.BlockSpec((tm, d), lambda i: (i, 0)),
                  pl.BlockSpec((1, d), lambda i: (0, 0)),
                  pl.BlockSpec((d, cols), lambda i: (0, 0), pipeline_mode=pl.Buffered(1))],
        out_specs=pl.BlockSpec((tm, cols), lambda i: (i, 0)),
        out_shape=jax.ShapeDtypeStruct((n, cols), F32),
        compiler_params=pltpu.CompilerParams(dimension_semantics=("arbitrary",), vmem_limit_bytes=VMEM_LIMIT),
        name="norm_proj",
    )(x2d, g.reshape(1, d), w_bf16)


def _headsum(xs, ones_bd):
    e = xs[0].shape[1]
    nq = e // QW
    rows = [x[:, q * QW:(q + 1) * QW] for x in xs for q in range(nq)]
    stacked = jnp.concatenate(rows, axis=0)
    s = _mm_exact_rhs(stacked, ones_bd)
    c = xs[0].shape[0]
    outs = []
    for i in range(len(xs)):
        outs.append(jnp.concatenate([s[(i * nq + q) * c:(i * nq + q + 1) * c] for q in range(nq)], axis=1))
    return outs


def _wkv_kernel(p_ref, sp_ref, s0_ref, mu_ref, w0_ref, w2_ref, a0_ref, a2_ref, kk_ref, ka_ref, rk_ref,
                lng_ref, lnb_ref, o_ref, sho_ref, so_ref, carry_ref, st_ref, *, chunk, t_valid, n_levels, passes):
    c_idx = pl.program_id(1)
    n_chunks = pl.num_programs(1)
    C = chunk
    TL = QUAD * C
    E = o_ref.shape[2]
    NQ = E // QW
    SHIFT = 3 * E + 2 * LORA

    rep = (_iota((HEAD, QW), 1) % HEAD == _iota((HEAD, QW), 0)).astype(BF16)
    rep_t = (_iota((QW, HEAD), 0) % HEAD == _iota((QW, HEAD), 1)).astype(BF16)
    mask_bd = (_iota((QW, QW), 0) // HEAD) == (_iota((QW, QW), 1) // HEAD)
    ones_bd = mask_bd.astype(BF16)

    @pl.when(c_idx == 0)
    def _init():
        carry_ref[...] = sp_ref[0]
        for q in range(NQ):
            st_ref[q] = jnp.where(mask_bd, _mm_exact_rhs(s0_ref[0, q], rep), 0.0)

    p = p_ref[0]
    sh = p[:, :SHIFT]
    z = p[:, SHIFT:]
    row1 = _iota((C, 1), 0)
    prev = jnp.where(row1 == 0, carry_ref[...], pltpu.roll(sh, 1, 0))
    xm = sh + (prev - sh) * mu_ref[...]
    last = sh[t_valid - 1:t_valid]
    carry_ref[...] = last

    @pl.when(c_idx == n_chunks - 1)
    def _shift_out():
        sho_ref[0] = last

    r = xm[:, :E]
    k = xm[:, E:2 * E]
    v = xm[:, 2 * E:3 * E]
    x_lora = xm[:, 3 * E:]
    lane_l = _iota((C, 2 * LORA), 1)
    t_lora = jnp.where(lane_l < LORA, jnp.tanh(x_lora), x_lora)
    wf = w0_ref[...] + _mm(t_lora, w2_ref[...])
    af = a0_ref[...] + _mm(x_lora, a2_ref[...])
    ld = -math.exp(-0.5) * _sigmoid(wf)
    a = _sigmoid(af)

    valid = row1 < t_valid
    if t_valid < C:
        ld = jnp.where(valid, ld, 0.0)
        v = jnp.where(valid, v, 0.0)

    kk0 = k * kk_ref[...]
    k2 = k * (1.0 + (a - 1.0) * ka_ref[...])
    ssq, rkk = _headsum([kk0 * kk0, r * k2 * rk_ref[...]], ones_bd)
    kk = kk0 / jnp.maximum(jnp.sqrt(ssq), 1e-12)
    bonus = rkk * v

    tri = (_iota((C, C), 0) >= _iota((C, C), 1)).astype(BF16)
    cum = _mm_exact_lhs(tri, ld)
    cum_last = cum[C - 1:C]
    g_ex = jnp.exp(cum - ld)
    g_in = jnp.exp(cum)
    g_inv = jnp.exp(-cum)
    g_rem = jnp.exp(cum_last - cum)
    g_tot = jnp.exp(cum_last)
    kka = kk * a
    at = -kk * g_ex
    bt = kka * g_inv
    kt = k2 * g_inv
    rt = r * g_in
    bg = kka * g_rem
    kg = k2 * g_rem
    if t_valid < C:
        at = jnp.where(valid, at, 0.0)
        bt = jnp.where(valid, bt, 0.0)
        kt = jnp.where(valid, kt, 0.0)
        bg = jnp.where(valid, bg, 0.0)
        kg = jnp.where(valid, kg, 0.0)

    mask_hc = (_iota((TL, QW), 0) // C) == (_iota((TL, QW), 1) // HEAD)
    mask_tt = (_iota((TL, TL), 0) // C) == (_iota((TL, TL), 1) // C)
    tok_t = _iota((C, TL), 0)
    tok_j = _iota((C, TL), 1) % C
    strict = tok_j < tok_t
    incl = tok_j <= tok_t
    eye_all = (tok_j == tok_t).astype(F32)

    narrow = (lambda x: x.astype(BF16)) if passes == 1 else (lambda x: x)

    def stack_hc(x):
        x = narrow(x)
        return jnp.where(mask_hc, jnp.concatenate([x] * QUAD, axis=0), jnp.zeros((), x.dtype))

    def stack_tt(x):
        x = narrow(x)
        return jnp.where(mask_tt, jnp.concatenate([x] * QUAD, axis=0), jnp.zeros((), x.dtype))

    mm = functools.partial(_mm, passes=passes)
    quads = range(NQ)
    sls = [slice(q * QW, (q + 1) * QW) for q in quads]
    s0s = [st_ref[q] for q in quads]
    ars = [narrow(jnp.concatenate([at[:, sl], rt[:, sl]], axis=0)) for sl in sls]
    v_sts = [stack_hc(v[:, sl]) for sl in sls]
    abs_ = [mm(ars[q], stack_hc(bt[:, sls[q]]), _NT) for q in quads]
    aks = [mm(ars[q], stack_hc(kt[:, sls[q]]), _NT) for q in quads]
    pws = [jnp.where(strict, abs_[q][:C], 0.0) for q in quads]
    a_rbs = [jnp.where(incl, abs_[q][C:], 0.0) for q in quads]
    a_aks = [jnp.where(strict, aks[q][:C], 0.0) for q in quads]
    a_rks = [jnp.where(incl, aks[q][C:], 0.0) for q in quads]
    invs = [eye_all + pws[q] for q in quads]
    bds = [stack_tt(pws[q]) for q in quads]
    for _ in range(n_levels):
        pws = [mm(pws[q], bds[q]) for q in quads]
        bds = [stack_tt(pws[q]) for q in quads]
        invs = [invs[q] + mm(invs[q], bds[q]) for q in quads]
    arss = [mm(ars[q], s0s[q], _NT) for q in quads]
    ws = [arss[q][:C] + mm(a_aks[q], v_sts[q]) for q in quads]
    us = [mm(invs[q], stack_hc(ws[q])) for q in quads]
    ys = [arss[q][C:] + mm(a_rbs[q], stack_hc(us[q])) + mm(a_rks[q], v_sts[q]) for q in quads]
    for q in quads:
        upd = mm(us[q], bg[:, sls[q]], _TN) + mm(v[:, sls[q]], kg[:, sls[q]], _TN)
        st_ref[q] = s0s[q] * g_tot[:, sls[q]] + jnp.where(mask_bd, upd, 0.0)
    y = jnp.concatenate(ys, axis=1)

    (ysum,) = _headsum([y], ones_bd)
    yc = y - ysum * (1.0 / HEAD)
    (vsum,) = _headsum([yc * yc], ones_bd)
    yn = yc * lax.rsqrt(vsum * (1.0 / HEAD) + GN_EPS) * lng_ref[...] + lnb_ref[...]
    o_ref[0] = (yn + bonus) * (z * _sigmoid(z))

    @pl.when(c_idx == n_chunks - 1)
    def _state_out():
        for q in range(NQ):
            so_ref[0, q] = _mm_exact_rhs(st_ref[q], rep_t)


def _wkv(proj3d, shift_prev, s0, prm, *, chunk, t_valid, n_levels, passes):
    b, t, cols = proj3d.shape
    e = prm["w0"].shape[1]
    shift = cols - e
    nq = e // QW
    nh = e // HEAD
    s0q = s0.reshape(b, nq, QW, HEAD)
    row = lambda n: pl.BlockSpec((1, n), lambda i, j: (0, 0))
    kern = functools.partial(_wkv_kernel, chunk=chunk, t_valid=t_valid, n_levels=n_levels, passes=passes)
    o, sho, so = pl.pallas_call(
        kern,
        grid=(b, t // chunk),
        in_specs=[pl.BlockSpec((1, chunk, cols), lambda i, j: (i, j, 0)),
                  pl.BlockSpec((1, 1, shift), lambda i, j: (i, 0, 0)),
                  pl.BlockSpec((1, nq, QW, HEAD), lambda i, j: (i, 0, 0, 0)),
                  row(shift), row(e),
                  pl.BlockSpec((2 * LORA, e), lambda i, j: (0, 0)),
                  row(e),
                  pl.BlockSpec((2 * LORA, e), lambda i, j: (0, 0)),
                  row(e), row(e), row(e), row(e), row(e)],
        out_specs=[pl.BlockSpec((1, chunk, e), lambda i, j: (i, j, 0)),
                   pl.BlockSpec((1, 1, shift), lambda i, j: (i, 0, 0)),
                   pl.BlockSpec((1, nq, QW, HEAD), lambda i, j: (i, 0, 0, 0))],
        out_shape=[jax.ShapeDtypeStruct((b, t, e), F32),
                   jax.ShapeDtypeStruct((b, 1, shift), F32),
                   jax.ShapeDtypeStruct((b, nq, QW, HEAD), F32)],
        scratch_shapes=[pltpu.VMEM((1, shift), F32),
                        pltpu.VMEM((nq, QW, QW), F32)],
        compiler_params=pltpu.CompilerParams(dimension_semantics=("arbitrary", "arbitrary"),
                                             vmem_limit_bytes=VMEM_LIMIT),
        name="wkv",
    )(proj3d, shift_prev.reshape(b, 1, shift), s0q,
      prm["mu"], prm["w0"], prm["w2p"], prm["a0"], prm["a2p"], prm["kk"], prm["ka"], prm["rk"],
      prm["lng"], prm["lnb"])
    return o, sho.reshape(b, shift), so.reshape(b, nh, HEAD, HEAD)


def _out_residual_kernel(o_ref, w_ref, x_ref, y_ref):
    y_ref[...] = x_ref[...] + jnp.dot(o_ref[...].astype(BF16), w_ref[...], preferred_element_type=F32)


def _out_residual(o2d, w_bf16, x2d, tm):
    n, e = o2d.shape
    d = x2d.shape[1]
    return pl.pallas_call(
        _out_residual_kernel,
        grid=(n // tm,),
        in_specs=[pl.BlockSpec((tm, e), lambda i: (i, 0)),
                  pl.BlockSpec((e, d), lambda i: (0, 0)),
                  pl.BlockSpec((tm, d), lambda i: (i, 0))],
        out_specs=pl.BlockSpec((tm, d), lambda i: (i, 0)),
        out_shape=jax.ShapeDtypeStruct((n, d), F32),
        compiler_params=pltpu.CompilerParams(dimension_semantics=("arbitrary",), vmem_limit_bytes=VMEM_LIMIT),
        name="out_residual",
    )(o2d, w_bf16, x2d)


def _gmlp_kernel(x_ref, g_ref, win_ref, vg_ref, vb_ref, wm_ref, bs_ref, wout_ref, nf_ref, *out_refs, emit_v):
    y_ref = out_refs[0]
    tm = x_ref.shape[0]
    e = vg_ref.shape[1]
    gd = e // GM_GROUPS
    x = x_ref[...]
    h = (x * lax.rsqrt(jnp.mean(x * x, axis=-1, keepdims=True) + NORM_EPS) * g_ref[...]).astype(BF16)
    proj = jnp.dot(h, win_ref[...], preferred_element_type=F32)
    u = jax.nn.gelu(proj[:, :e])
    vf = jax.nn.gelu(proj[:, e:2 * e])
    z = proj[:, 2 * e:]
    vm = jnp.mean(vf, axis=-1, keepdims=True)
    vc = vf - vm
    vv = jnp.mean(vc * vc, axis=-1, keepdims=True)
    vn = vc * lax.rsqrt(vv + LN_EPS) * vg_ref[...] + vb_ref[...]
    if emit_v:
        out_refs[1][...] = vn
    vn16 = vn.astype(BF16)
    causal = _iota((GM_CHUNK, GM_CHUNK), 0) >= _iota((GM_CHUNK, GM_CHUNK), 1)
    rows = []
    for j in range(tm // GM_CHUNK):
        cols = []
        for gi in range(GM_GROUPS):
            wm = jnp.where(causal, wm_ref[gi], 0.0).astype(BF16)
            blk = vn16[j * GM_CHUNK:(j + 1) * GM_CHUNK, gi * gd:(gi + 1) * gd]
            cols.append(jnp.dot(wm, blk, preferred_element_type=F32) + bs_ref[gi])
        rows.append(jnp.concatenate(cols, axis=1))
    mixed = jnp.concatenate(rows, axis=0)
    o = u * mixed * (z * _sigmoid(z))
    x2 = x + jnp.dot(o.astype(BF16), wout_ref[...], preferred_element_type=F32)
    y_ref[...] = x2 * lax.rsqrt(jnp.mean(x2 * x2, axis=-1, keepdims=True) + NORM_EPS) * nf_ref[...]


def _gmlp(x2d, g, win_bf16, vg, vb, wmix, bias, wout_bf16, nf, *, tm, emit_v):
    n, d = x2d.shape
    e = vg.shape[0]
    const2 = lambda shape: pl.BlockSpec(shape, lambda i: (0, 0))
    out_shape = [jax.ShapeDtypeStruct((n, d), F32)]
    out_specs = [pl.BlockSpec((tm, d), lambda i: (i, 0))]
    if emit_v:
        out_shape.append(jax.ShapeDtypeStruct((n, e), F32))
        out_specs.append(pl.BlockSpec((tm, e), lambda i: (i, 0)))
    outs = pl.pallas_call(
        functools.partial(_gmlp_kernel, emit_v=emit_v),
        grid=(n // tm,),
        in_specs=[pl.BlockSpec((tm, d), lambda i: (i, 0)),
                  const2((1, d)),
                  pl.BlockSpec((d, 3 * e), lambda i: (0, 0), pipeline_mode=pl.Buffered(1)),
                  const2((1, e)), const2((1, e)),
                  pl.BlockSpec((GM_GROUPS, GM_CHUNK, GM_CHUNK), lambda i: (0, 0, 0)),
                  pl.BlockSpec((GM_GROUPS, GM_CHUNK, 1), lambda i: (0, 0, 0)),
                  pl.BlockSpec((e, d), lambda i: (0, 0), pipeline_mode=pl.Buffered(1)),
                  const2((1, d))],
        out_specs=out_specs,
        out_shape=out_shape,
        compiler_params=pltpu.CompilerParams(dimension_semantics=("arbitrary",), vmem_limit_bytes=VMEM_LIMIT),
        name="gmlp",
    )(x2d, g.reshape(1, d), win_bf16, vg.reshape(1, e), vb.reshape(1, e), wmix,
      bias.reshape(GM_GROUPS, GM_CHUNK, 1), wout_bf16, nf.reshape(1, d))
    return outs


def _row_tile(n, pref):
    t = pref
    while n % t:
        t //= 2
    return t


def kernel(x_prompt, x_sample, state_shift, state_wkv, norm_g, norm_f, rw_in, rw_mu, rw_w0, rw_w2, rw_a0, rw_a2, rw_kk, rw_ka, rw_rk, rw_lnx_g, rw_lnx_b, rw_out, gm_in, gm_vg, gm_vb, gm_ws, gm_bs, gm_out):
    bp, tp, d = x_prompt.shape
    bs_, ts, _ = x_sample.shape
    e = rw_w0.shape[1]
    nh = e // HEAD
    shift = 3 * e + 2 * LORA
    xp = x_prompt.reshape(bp * tp, d)
    xs = x_sample.reshape(bs_ * ts, d)

    zeros_l = jnp.zeros((LORA, e), F32)
    prm = dict(
        mu=rw_mu[0].reshape(1, shift), w0=rw_w0[0].reshape(1, e), a0=rw_a0[0].reshape(1, e),
        w2p=jnp.concatenate([rw_w2[0], zeros_l], axis=0).astype(BF16),
        a2p=jnp.concatenate([zeros_l, rw_a2[0]], axis=0).astype(BF16),
        kk=rw_kk[0].reshape(1, e), ka=rw_ka[0].reshape(1, e), rk=rw_rk[0].reshape(1, e),
        lng=rw_lnx_g[0].reshape(1, e), lnb=rw_lnx_b[0].reshape(1, e))
    w_in = rw_in[0].astype(BF16)
    w_out = rw_out[0].astype(BF16)

    proj_p = _norm_proj(xp, norm_g[0], w_in, _row_tile(bp * tp, 256)).reshape(bp, tp, shift + e)
    proj_s = _norm_proj(xs, norm_g[0], w_in, _row_tile(bs_ * ts, 256)).reshape(bs_, ts, shift + e)

    chunk_p = 64
    o_p, shift_p, wkv_p = _wkv(proj_p, jnp.zeros((bp, shift), F32), jnp.zeros((bp, nh, HEAD, HEAD), F32), prm,
                               chunk=chunk_p, t_valid=chunk_p, n_levels=5, passes=1)
    chunk_s = 8
    proj_s = jnp.pad(proj_s, ((0, 0), (0, chunk_s - ts), (0, 0)))
    o_s, shift_s, wkv_s = _wkv(proj_s, state_shift[0], state_wkv[0], prm,
                               chunk=chunk_s, t_valid=ts, n_levels=max(1, math.ceil(math.log2(ts)) - 1), passes=1)
    o_s = o_s[:, :ts]

    x1p = _out_residual(o_p.reshape(bp * tp, e), w_out, xp, _row_tile(bp * tp, 512))
    x1s = _out_residual(o_s.reshape(bs_ * ts, e), w_out, xs, _row_tile(bs_ * ts, 512))

    g_in = gm_in[0].astype(BF16)
    g_out = gm_out[0].astype(BF16)
    wm_p = gm_ws[0]
    reps = GM_CHUNK // ts
    eye = jnp.eye(reps, dtype=F32)
    wm_s = jax.vmap(lambda w: jnp.kron(eye, w))(wm_p[:, :ts, :ts])
    bias_s = jnp.tile(gm_bs[0][:, :ts], (1, reps))

    (y_p,) = _gmlp(x1p, norm_g[1], g_in, gm_vg[0], gm_vb[0], wm_p, gm_bs[0], g_out, norm_f,
                   tm=_row_tile(bp * tp, 256), emit_v=False)
    y_s, v_s = _gmlp(x1s, norm_g[1], g_in, gm_vg[0], gm_vb[0], wm_s, bias_s, g_out, norm_f,
                     tm=_row_tile(bs_ * ts, 256), emit_v=True)

    return (y_p.reshape(bp, tp, d), y_s.reshape(bs_, ts, d),
            shift_p[None], wkv_p[None], shift_s[None], wkv_s[None],
            v_s.reshape(1, bs_, ts, e))
```

```python
import functools
import math

import jax
import jax.numpy as jnp
from jax import lax
from jax.experimental import pallas as pl
from jax.experimental.pallas import tpu as pltpu

F32 = jnp.float32
BF16 = jnp.bfloat16

HEAD = 64
QUAD = 4
QW = QUAD * HEAD
LORA = 64
NORM_EPS = 1e-6
LN_EPS = 1e-5
GN_EPS = 64e-5
GM_CHUNK = 128
GM_GROUPS = 8
VMEM_LIMIT = 56 * 1024 * 1024


def _split2(x):
    hi = x.astype(BF16)
    lo = (x - hi.astype(F32)).astype(BF16)
    return hi, lo


_NN = (((1,), (0,)), ((), ()))
_NT = (((1,), (1,)), ((), ()))
_TN = (((0,), (0,)), ((), ()))


def _mm(a, b, dims=_NN, passes=1):
    if passes == 1:
        return lax.dot_general(a.astype(BF16), b.astype(BF16), dims, preferred_element_type=F32)
    ah, al = _split2(a)
    bh, bl = _split2(b)
    d = functools.partial(lax.dot_general, dimension_numbers=dims, preferred_element_type=F32)
    return d(ah, bh) + (d(ah, bl) + d(al, bh))


def _mm_sum_rhs(a, b_bf16, dims=_NN):
    d = functools.partial(lax.dot_general, dimension_numbers=dims, preferred_element_type=F32)
    h, l = _split2(a)
    return d(h, b_bf16) + d(l, b_bf16)


def _mm_sum_lhs(a_bf16, b, dims=_NN):
    d = functools.partial(lax.dot_general, dimension_numbers=dims, preferred_element_type=F32)
    h, l = _split2(b)
    return d(a_bf16, h) + d(a_bf16, l)


def _iota(shape, dim):
    return lax.broadcasted_iota(jnp.int32, shape, dim)


def _sigmoid(x):
    return 0.5 * jnp.tanh(0.5 * x) + 0.5


def _norm_proj_kernel(x_ref, g_ref, w_ref, o_ref):
    x = x_ref[...]
    y = x * lax.rsqrt(jnp.mean(x * x, axis=-1, keepdims=True) + NORM_EPS)
    h = (y * g_ref[...]).astype(BF16)
    o_ref[...] = jnp.dot(h, w_ref[...], preferred_element_type=F32)


def _norm_proj(x2d, g, w_bf16, tm):
    n, d = x2d.shape
    cols = w_bf16.shape[1]
    return pl.pallas_call(
        _norm_proj_kernel,
        grid=(n // tm,),
        in_specs=[pl.BlockSpec((tm, d), lambda i: (i, 0)),
                  pl.BlockSpec((1, d), lambda i: (0, 0)),
                  pl.BlockSpec((d, cols), lambda i: (0, 0), pipeline_mode=pl.Buffered(1))],
        out_specs=pl.BlockSpec((tm, cols), lambda i: (i, 0)),
        out_shape=jax.ShapeDtypeStruct((n, cols), F32),
        compiler_params=pltpu.CompilerParams(dimension_semantics=("arbitrary",), vmem_limit_bytes=VMEM_LIMIT),
        name="norm_proj",
    )(x2d, g.reshape(1, d), w_bf16)


def _headsum(xs, ones_bd):
    e = xs[0].shape[1]
    nq = e // QW
    rows = [x[:, q * QW:(q + 1) * QW] for x in xs for q in range(nq)]
    stacked = jnp.concatenate(rows, axis=0)
    s = _mm_sum_rhs(stacked, ones_bd)
    c = xs[0].shape[0]
    outs = []
    for i in range(len(xs)):
        outs.append(jnp.concatenate([s[(i * nq + q) * c:(i * nq + q + 1) * c] for q in range(nq)], axis=1))
    return outs


def _wkv_kernel(p_ref, sp_ref, s0_ref, mu_ref, w0_ref, w2_ref, a0_ref, a2_ref, kk_ref, ka_ref, rk_ref,
                lng_ref, lnb_ref, o_ref, sho_ref, so_ref, carry_ref, st_ref, *, chunk, t_valid, n_levels, passes):
    c_idx = pl.program_id(1)
    n_chunks = pl.num_programs(1)
    C = chunk
    TL = QUAD * C
    E = o_ref.shape[2]
    NQ = E // QW
    SHIFT = 3 * E + 2 * LORA

    mask_bd = (_iota((QW, QW), 0) // HEAD) == (_iota((QW, QW), 1) // HEAD)
    ones_bd = mask_bd.astype(BF16)

    @pl.when(c_idx == 0)
    def _init():
        carry_ref[...] = sp_ref[0]
        zero_blk = jnp.zeros((HEAD, HEAD), F32)
        for q in range(NQ):
            st_ref[q] = jnp.concatenate(
                [jnp.concatenate([s0_ref[0, QUAD * q + h] if j == h else zero_blk for j in range(QUAD)], axis=1)
                 for h in range(QUAD)], axis=0)

    p = p_ref[0]
    sh = p[:, :SHIFT]
    z = p[:, SHIFT:]
    row1 = _iota((C, 1), 0)
    prev = jnp.where(row1 == 0, carry_ref[...], pltpu.roll(sh, 1, 0))
    xm = sh + (prev - sh) * mu_ref[...]
    last = sh[t_valid - 1:t_valid]
    carry_ref[...] = last

    @pl.when(c_idx == n_chunks - 1)
    def _shift_out():
        sho_ref[0] = last

    r = xm[:, :E]
    k = xm[:, E:2 * E]
    v = xm[:, 2 * E:3 * E]
    x_lora = xm[:, 3 * E:]
    lane_l = _iota((C, 2 * LORA), 1)
    t_lora = jnp.where(lane_l < LORA, jnp.tanh(x_lora), x_lora)
    wf = w0_ref[...] + _mm(t_lora, w2_ref[...])
    af = a0_ref[...] + _mm(x_lora, a2_ref[...])
    ld = -math.exp(-0.5) * _sigmoid(wf)
    a = _sigmoid(af)

    valid = row1 < t_valid
    if t_valid < C:
        ld = jnp.where(valid, ld, 0.0)
        v = jnp.where(valid, v, 0.0)

    kk0 = k * kk_ref[...]
    k2 = k * (1.0 + (a - 1.0) * ka_ref[...])
    ssq, rkk = _headsum([kk0 * kk0, r * k2 * rk_ref[...]], ones_bd)
    kk = kk0 * jnp.minimum(lax.rsqrt(ssq), 1e12)
    bonus = rkk * v

    tri = (_iota((C, C), 0) >= _iota((C, C), 1)).astype(BF16)
    cum = _mm_sum_lhs(tri, ld)
    cum_last = cum[C - 1:C]
    g_ex = jnp.exp(cum - ld)
    g_in = jnp.exp(cum)
    g_inv = jnp.exp(-cum)
    g_rem = jnp.exp(cum_last - cum)
    g_tot = jnp.exp(cum_last)
    kka = kk * a
    at = -kk * g_ex
    bt = kka * g_inv
    kt = k2 * g_inv
    rt = r * g_in
    bg = kka * g_rem
    kg = k2 * g_rem
    if t_valid < C:
        at = jnp.where(valid, at, 0.0)
        bt = jnp.where(valid, bt, 0.0)
        kt = jnp.where(valid, kt, 0.0)
        bg = jnp.where(valid, bg, 0.0)
        kg = jnp.where(valid, kg, 0.0)

    mask_hc = (_iota((TL, QW), 0) // C) == (_iota((TL, QW), 1) // HEAD)
    mask_tt = (_iota((TL, TL), 0) // C) == (_iota((TL, TL), 1) // C)
    tok_t = _iota((C, TL), 0)
    tok_j = _iota((C, TL), 1) % C
    strict = tok_j < tok_t
    incl = tok_j <= tok_t
    eye_all = (tok_j == tok_t).astype(F32)

    narrow = (lambda x: x.astype(BF16)) if passes == 1 else (lambda x: x)

    def stack_hc(x):
        x = narrow(x)
        return jnp.where(mask_hc, jnp.concatenate([x] * QUAD, axis=0), jnp.zeros((), x.dtype))

    def stack_tt(x):
        x = narrow(x)
        return jnp.where(mask_tt, jnp.concatenate([x] * QUAD, axis=0), jnp.zeros((), x.dtype))

    mm = functools.partial(_mm, passes=passes)
    quads = range(NQ)
    sls = [slice(q * QW, (q + 1) * QW) for q in quads]
    s0s = [st_ref[q] for q in quads]
    ars = [narrow(jnp.concatenate([at[:, sl], rt[:, sl]], axis=0)) for sl in sls]
    v_sts = [stack_hc(v[:, sl]) for sl in sls]
    abs_ = [mm(ars[q], stack_hc(bt[:, sls[q]]), _NT) for q in quads]
    aks = [mm(ars[q], stack_hc(kt[:, sls[q]]), _NT) for q in quads]
    pws = [jnp.where(strict, abs_[q][:C], 0.0) for q in quads]
    a_rbs = [jnp.where(incl, abs_[q][C:], 0.0) for q in quads]
    a_aks = [jnp.where(strict, aks[q][:C], 0.0) for q in quads]
    a_rks = [jnp.where(incl, aks[q][C:], 0.0) for q in quads]
    invs = [eye_all + pws[q] for q in quads]
    pws = [mm(pws[q], stack_tt(pws[q])) for q in quads]
    for _ in range(n_levels - 1):
        bds = [stack_tt(pws[q]) for q in quads]
        res = [mm(jnp.concatenate([pws[q], invs[q]], axis=0), bds[q]) for q in quads]
        pws = [res[q][:C] for q in quads]
        invs = [invs[q] + res[q][C:] for q in quads]
    invs = [invs[q] + mm(invs[q], stack_tt(pws[q])) for q in quads]
    arss = [mm(ars[q], s0s[q], _NT) for q in quads]
    avs = [mm(jnp.concatenate([a_aks[q], a_rks[q]], axis=0), v_sts[q]) for q in quads]
    ws = [arss[q][:C] + avs[q][:C] for q in quads]
    us = [mm(invs[q], stack_hc(ws[q])) for q in quads]
    ys = [arss[q][C:] + avs[q][C:] + mm(a_rbs[q], stack_hc(us[q])) for q in quads]
    for q in quads:
        uv = narrow(jnp.concatenate([us[q], v[:, sls[q]]], axis=0))
        bk = narrow(jnp.concatenate([bg[:, sls[q]], kg[:, sls[q]]], axis=0))
        st_ref[q] = s0s[q] * g_tot[:, sls[q]] + jnp.where(mask_bd, mm(uv, bk, _TN), 0.0)
    y = jnp.concatenate(ys, axis=1)

    (ysum,) = _headsum([y], ones_bd)
    yc = y - ysum * (1.0 / HEAD)
    (vsum,) = _headsum([yc * yc], ones_bd)
    yn = yc * lax.rsqrt(vsum * (1.0 / HEAD) + GN_EPS) * lng_ref[...] + lnb_ref[...]
    o_ref[0] = (yn + bonus) * (z * _sigmoid(z))

    @pl.when(c_idx == n_chunks - 1)
    def _state_out():
        for q in range(NQ):
            s = st_ref[q]
            for h in range(QUAD):
                so_ref[0, QUAD * q + h] = s[h * HEAD:(h + 1) * HEAD, h * HEAD:(h + 1) * HEAD]


def _wkv(proj3d, shift_prev, s0, prm, *, chunk, t_valid, n_levels, passes):
    b, t, cols = proj3d.shape
    e = prm["w0"].shape[1]
    shift = cols - e
    nq = e // QW
    nh = e // HEAD
    row = lambda n: pl.BlockSpec((1, n), lambda i, j: (0, 0))
    kern = functools.partial(_wkv_kernel, chunk=chunk, t_valid=t_valid, n_levels=n_levels, passes=passes)
    o, sho, so = pl.pallas_call(
        kern,
        grid=(b, t // chunk),
        in_specs=[pl.BlockSpec((1, chunk, cols), lambda i, j: (i, j, 0)),
                  pl.BlockSpec((1, 1, shift), lambda i, j: (i, 0, 0)),
                  pl.BlockSpec((1, nh, HEAD, HEAD), lambda i, j: (i, 0, 0, 0)),
                  row(shift), row(e),
                  pl.BlockSpec((2 * LORA, e), lambda i, j: (0, 0)),
                  row(e),
                  pl.BlockSpec((2 * LORA, e), lambda i, j: (0, 0)),
                  row(e), row(e), row(e), row(e), row(e)],
        out_specs=[pl.BlockSpec((1, chunk, e), lambda i, j: (i, j, 0)),
                   pl.BlockSpec((1, 1, shift), lambda i, j: (i, 0, 0)),
                   pl.BlockSpec((1, nh, HEAD, HEAD), lambda i, j: (i, 0, 0, 0))],
        out_shape=[jax.ShapeDtypeStruct((b, t, e), F32),
                   jax.ShapeDtypeStruct((b, 1, shift), F32),
                   jax.ShapeDtypeStruct((b, nh, HEAD, HEAD), F32)],
        scratch_shapes=[pltpu.VMEM((1, shift), F32),
                        pltpu.VMEM((nq, QW, QW), F32)],
        compiler_params=pltpu.CompilerParams(dimension_semantics=("arbitrary", "arbitrary"),
                                             vmem_limit_bytes=VMEM_LIMIT),
        name="wkv",
    )(proj3d, shift_prev.reshape(b, 1, shift), s0,
      prm["mu"], prm["w0"], prm["w2p"], prm["a0"], prm["a2p"], prm["kk"], prm["ka"], prm["rk"],
      prm["lng"], prm["lnb"])
    return o, sho.reshape(b, shift), so


def _out_residual_kernel(o_ref, w_ref, x_ref, y_ref):
    y_ref[...] = x_ref[...] + jnp.dot(o_ref[...].astype(BF16), w_ref[...], preferred_element_type=F32)


def _out_residual(o2d, w_bf16, x2d, tm):
    n, e = o2d.shape
    d = x2d.shape[1]
    return pl.pallas_call(
        _out_residual_kernel,
        grid=(n // tm,),
        in_specs=[pl.BlockSpec((tm, e), lambda i: (i, 0)),
                  pl.BlockSpec((e, d), lambda i: (0, 0)),
                  pl.BlockSpec((tm, d), lambda i: (i, 0))],
        out_specs=pl.BlockSpec((tm, d), lambda i: (i, 0)),
        out_shape=jax.ShapeDtypeStruct((n, d), F32),
        compiler_params=pltpu.CompilerParams(dimension_semantics=("arbitrary",), vmem_limit_bytes=VMEM_LIMIT),
        name="out_residual",
    )(o2d, w_bf16, x2d)


def _gmlp_kernel(x_ref, g_ref, win_ref, vg_ref, vb_ref, wm_ref, bs_ref, wout_ref, nf_ref, *out_refs, emit_v):
    y_ref = out_refs[0]
    tm = x_ref.shape[0]
    e = vg_ref.shape[1]
    gd = e // GM_GROUPS
    x = x_ref[...]
    h = (x * lax.rsqrt(jnp.mean(x * x, axis=-1, keepdims=True) + NORM_EPS) * g_ref[...]).astype(BF16)
    proj = jnp.dot(h, win_ref[...], preferred_element_type=F32)
    u = jax.nn.gelu(proj[:, :e])
    vf = jax.nn.gelu(proj[:, e:2 * e])
    z = proj[:, 2 * e:]
    vm = jnp.mean(vf, axis=-1, keepdims=True)
    vc = vf - vm
    vv = jnp.mean(vc * vc, axis=-1, keepdims=True)
    vn = vc * lax.rsqrt(vv + LN_EPS) * vg_ref[...] + vb_ref[...]
    if emit_v:
        out_refs[1][...] = vn
    vn16 = vn.astype(BF16)
    causal = _iota((GM_CHUNK, GM_CHUNK), 0) >= _iota((GM_CHUNK, GM_CHUNK), 1)
    rows = []
    for j in range(tm // GM_CHUNK):
        cols = []
        for gi in range(GM_GROUPS):
            wm = jnp.where(causal, wm_ref[gi], 0.0).astype(BF16)
            blk = vn16[j * GM_CHUNK:(j + 1) * GM_CHUNK, gi * gd:(gi + 1) * gd]
            cols.append(jnp.dot(wm, blk, preferred_element_type=F32) + bs_ref[gi])
        rows.append(jnp.concatenate(cols, axis=1))
    mixed = jnp.concatenate(rows, axis=0)
    o = u * mixed * (z * _sigmoid(z))
    x2 = x + jnp.dot(o.astype(BF16), wout_ref[...], preferred_element_type=F32)
    y_ref[...] = x2 * lax.rsqrt(jnp.mean(x2 * x2, axis=-1, keepdims=True) + NORM_EPS) * nf_ref[...]


def _gmlp(x2d, g, win_bf16, vg, vb, wmix, bias, wout_bf16, nf, *, tm, emit_v):
    n, d = x2d.shape
    e = vg.shape[0]
    const2 = lambda shape: pl.BlockSpec(shape, lambda i: (0, 0))
    out_shape = [jax.ShapeDtypeStruct((n, d), F32)]
    out_specs = [pl.BlockSpec((tm, d), lambda i: (i, 0))]
    if emit_v:
        out_shape.append(jax.ShapeDtypeStruct((n, e), F32))
        out_specs.append(pl.BlockSpec((tm, e), lambda i: (i, 0)))
    outs = pl.pallas_call(
        functools.partial(_gmlp_kernel, emit_v=emit_v),
        grid=(n // tm,),
        in_specs=[pl.BlockSpec((tm, d), lambda i: (i, 0)),
                  const2((1, d)),
                  pl.BlockSpec((d, 3 * e), lambda i: (0, 0), pipeline_mode=pl.Buffered(1)),
                  const2((1, e)), const2((1, e)),
                  pl.BlockSpec((GM_GROUPS, GM_CHUNK, GM_CHUNK), lambda i: (0, 0, 0)),
                  pl.BlockSpec((GM_GROUPS, GM_CHUNK, 1), lambda i: (0, 0, 0)),
                  pl.BlockSpec((e, d), lambda i: (0, 0), pipeline_mode=pl.Buffered(1)),
                  const2((1, d))],
        out_specs=out_specs,
        out_shape=out_shape,
        compiler_params=pltpu.CompilerParams(dimension_semantics=("arbitrary",), vmem_limit_bytes=VMEM_LIMIT),
        name="gmlp",
    )(x2d, g.reshape(1, d), win_bf16, vg.reshape(1, e), vb.reshape(1, e), wmix,
      bias.reshape(GM_GROUPS, GM_CHUNK, 1), wout_bf16, nf.reshape(1, d))
    return outs


def _row_tile(n, pref):
    t = pref
    while n % t:
        t //= 2
    return t


def kernel(x_prompt, x_sample, state_shift, state_wkv, norm_g, norm_f, rw_in, rw_mu, rw_w0, rw_w2, rw_a0, rw_a2, rw_kk, rw_ka, rw_rk, rw_lnx_g, rw_lnx_b, rw_out, gm_in, gm_vg, gm_vb, gm_ws, gm_bs, gm_out):
    bp, tp, d = x_prompt.shape
    bs_, ts, _ = x_sample.shape
    e = rw_w0.shape[1]
    nh = e // HEAD
    shift = 3 * e + 2 * LORA
    xp = x_prompt.reshape(bp * tp, d)
    xs = x_sample.reshape(bs_ * ts, d)

    zeros_l = jnp.zeros((LORA, e), F32)
    prm = dict(
        mu=rw_mu[0].reshape(1, shift), w0=rw_w0[0].reshape(1, e), a0=rw_a0[0].reshape(1, e),
        w2p=jnp.concatenate([rw_w2[0], zeros_l], axis=0).astype(BF16),
        a2p=jnp.concatenate([zeros_l, rw_a2[0]], axis=0).astype(BF16),
        kk=rw_kk[0].reshape(1, e), ka=rw_ka[0].reshape(1, e), rk=rw_rk[0].reshape(1, e),
        lng=rw_lnx_g[0].reshape(1, e), lnb=rw_lnx_b[0].reshape(1, e))
    w_in = rw_in[0].astype(BF16)
    w_out = rw_out[0].astype(BF16)

    proj_p = _norm_proj(xp, norm_g[0], w_in, _row_tile(bp * tp, 256)).reshape(bp, tp, shift + e)
    proj_s = _norm_proj(xs, norm_g[0], w_in, _row_tile(bs_ * ts, 256)).reshape(bs_, ts, shift + e)

    chunk_p = 64
    o_p, shift_p, wkv_p = _wkv(proj_p, jnp.zeros((bp, shift), F32), jnp.zeros((bp, nh, HEAD, HEAD), F32), prm,
                               chunk=chunk_p, t_valid=chunk_p, n_levels=5, passes=1)
    chunk_s = 8
    proj_s = jnp.pad(proj_s, ((0, 0), (0, chunk_s - ts), (0, 0)))
    o_s, shift_s, wkv_s = _wkv(proj_s, state_shift[0], state_wkv[0], prm,
                               chunk=chunk_s, t_valid=ts, n_levels=max(1, math.ceil(math.log2(ts)) - 1), passes=1)
    o_s = o_s[:, :ts]

    x1p = _out_residual(o_p.reshape(bp * tp, e), w_out, xp, _row_tile(bp * tp, 512))
    x1s = _out_residual(o_s.reshape(bs_ * ts, e), w_out, xs, _row_tile(bs_ * ts, 512))

    g_in = gm_in[0].astype(BF16)
    g_out = gm_out[0].astype(BF16)
    wm_p = gm_ws[0]
    reps = GM_CHUNK // ts
    eye = jnp.eye(reps, dtype=F32)
    wm_s = jax.vmap(lambda w: jnp.kron(eye, w))(wm_p[:, :ts, :ts])
    bias_s = jnp.tile(gm_bs[0][:, :ts], (1, reps))

    (y_p,) = _gmlp(x1p, norm_g[1], g_in, gm_vg[0], gm_vb[0], wm_p, gm_bs[0], g_out, norm_f,
                   tm=_row_tile(bp * tp, 256), emit_v=False)
    y_s, v_s = _gmlp(x1s, norm_g[1], g_in, gm_vg[0], gm_vb[0], wm_s, bias_s, g_out, norm_f,
                     tm=_row_tile(bs_ * ts, 256), emit_v=True)

    return (y_p.reshape(bp, tp, d), y_s.reshape(bs_, ts, d),
            shift_p[None], wkv_p[None], shift_s[None], wkv_s[None],
            v_s.reshape(1, bs_, ts, e))
```

```python
import functools
import math

import jax
import jax.numpy as jnp
from jax import lax
from jax.experimental import pallas as pl
from jax.experimental.pallas import tpu as pltpu

F32 = jnp.float32
BF16 = jnp.bfloat16

HEAD = 64
QUAD = 4
QW = QUAD * HEAD
LORA = 64
NORM_EPS = 1e-6
LN_EPS = 1e-5
GN_EPS = 64e-5
GM_CHUNK = 128
GM_GROUPS = 8
VMEM_LIMIT = 56 * 1024 * 1024


def _split2(x):
    hi = x.astype(BF16)
    lo = (x - hi.astype(F32)).astype(BF16)
    return hi, lo


_NN = (((1,), (0,)), ((), ()))
_NT = (((1,), (1,)), ((), ()))
_TN = (((0,), (0,)), ((), ()))


def _mm(a, b, dims=_NN, passes=1):
    if passes == 1:
        return lax.dot_general(a.astype(BF16), b.astype(BF16), dims, preferred_element_type=F32)
    ah, al = _split2(a)
    bh, bl = _split2(b)
    d = functools.partial(lax.dot_general, dimension_numbers=dims, preferred_element_type=F32)
    return d(ah, bh) + (d(ah, bl) + d(al, bh))


def _mm_sum_rhs(a, b_bf16, dims=_NN):
    d = functools.partial(lax.dot_general, dimension_numbers=dims, preferred_element_type=F32)
    h, l = _split2(a)
    return d(h, b_bf16) + d(l, b_bf16)


def _mm_sum_lhs(a_bf16, b, dims=_NN):
    d = functools.partial(lax.dot_general, dimension_numbers=dims, preferred_element_type=F32)
    h, l = _split2(b)
    return d(a_bf16, h) + d(a_bf16, l)


def _iota(shape, dim):
    return lax.broadcasted_iota(jnp.int32, shape, dim)


def _sigmoid(x):
    return 0.5 * jnp.tanh(0.5 * x) + 0.5


def _norm_proj_kernel(x_ref, g_ref, w_ref, o_ref):
    x = x_ref[...]
    y = x * lax.rsqrt(jnp.mean(x * x, axis=-1, keepdims=True) + NORM_EPS)
    h = (y * g_ref[...]).astype(BF16)
    o_ref[...] = jnp.dot(h, w_ref[...], preferred_element_type=F32)


def _norm_proj(x2d, g, w_bf16, tm):
    n, d = x2d.shape
    cols = w_bf16.shape[1]
    return pl.pallas_call(
        _norm_proj_kernel,
        grid=(n // tm,),
        in_specs=[pl.BlockSpec((tm, d), lambda i: (i, 0)),
                  pl.BlockSpec((1, d), lambda i: (0, 0)),
                  pl.BlockSpec((d, cols), lambda i: (0, 0), pipeline_mode=pl.Buffered(1))],
        out_specs=pl.BlockSpec((tm, cols), lambda i: (i, 0)),
        out_shape=jax.ShapeDtypeStruct((n, cols), F32),
        compiler_params=pltpu.CompilerParams(dimension_semantics=("arbitrary",), vmem_limit_bytes=VMEM_LIMIT),
        name="norm_proj",
    )(x2d, g.reshape(1, d), w_bf16)


_OPS_NARROW = ("at", "rt", "bt", "kt", "v", "bg", "kg")
_OPS_F32 = ("bonus", "gate")


def _headsum(xs, ones_bd):
    e = xs[0].shape[1]
    nq = e // QW
    rows = [x[:, q * QW:(q + 1) * QW] for x in xs for q in range(nq)]
    stacked = jnp.concatenate(rows, axis=0)
    s = _mm_sum_rhs(stacked, ones_bd)
    c = xs[0].shape[0]
    outs = []
    for i in range(len(xs)):
        outs.append(jnp.concatenate([s[(i * nq + q) * c:(i * nq + q + 1) * c] for q in range(nq)], axis=1))
    return outs


def _wkv_prep(p, carry, prm, *, t_valid, narrow):
    mu_ref, w0_ref, w2_ref, a0_ref, a2_ref, kk_ref, ka_ref, rk_ref = prm
    C = p.shape[0]
    E = w0_ref.shape[1]
    SHIFT = 3 * E + 2 * LORA
    ones_bd = ((_iota((QW, QW), 0) // HEAD) == (_iota((QW, QW), 1) // HEAD)).astype(BF16)

    sh = p[:, :SHIFT]
    z = p[:, SHIFT:]
    row1 = _iota((C, 1), 0)
    prev = jnp.where(row1 == 0, carry, pltpu.roll(sh, 1, 0))
    xm = sh + (prev - sh) * mu_ref[...]
    last = sh[t_valid - 1:t_valid]

    r = xm[:, :E]
    k = xm[:, E:2 * E]
    v = xm[:, 2 * E:3 * E]
    x_lora = xm[:, 3 * E:]
    lane_l = _iota((C, 2 * LORA), 1)
    t_lora = jnp.where(lane_l < LORA, jnp.tanh(x_lora), x_lora)
    wf = w0_ref[...] + _mm(t_lora, w2_ref[...])
    af = a0_ref[...] + _mm(x_lora, a2_ref[...])
    ld = -math.exp(-0.5) * _sigmoid(wf)
    a = _sigmoid(af)

    valid = row1 < t_valid
    if t_valid < C:
        ld = jnp.where(valid, ld, 0.0)
        v = jnp.where(valid, v, 0.0)

    kk0 = k * kk_ref[...]
    k2 = k * (1.0 + (a - 1.0) * ka_ref[...])
    ssq, rkk = _headsum([kk0 * kk0, r * k2 * rk_ref[...]], ones_bd)
    kk = kk0 * jnp.minimum(lax.rsqrt(ssq), 1e12)

    tri = (_iota((C, C), 0) >= _iota((C, C), 1)).astype(BF16)
    cum = _mm_sum_lhs(tri, ld)
    cum_last = cum[C - 1:C]
    g_rem = jnp.exp(cum_last - cum)
    g_inv = jnp.exp(-cum)
    kka = kk * a
    ops = dict(
        at=-kk * jnp.exp(cum - ld),
        rt=r * jnp.exp(cum),
        bt=kka * g_inv,
        kt=k2 * g_inv,
        v=v,
        bg=kka * g_rem,
        kg=k2 * g_rem)
    if t_valid < C:
        ops = {n: (x if n in ("rt", "v") else jnp.where(valid, x, 0.0)) for n, x in ops.items()}
    ops = {n: narrow(x) for n, x in ops.items()}
    ops.update(bonus=rkk * v, gate=z * _sigmoid(z), g_tot=jnp.exp(cum_last), last=last)
    return ops


def _wkv_core(ops, states, lng_ref, lnb_ref, *, n_levels, passes, narrow):
    C, E = ops["at"].shape
    TL = QUAD * C
    NQ = E // QW
    mask_bd = (_iota((QW, QW), 0) // HEAD) == (_iota((QW, QW), 1) // HEAD)
    ones_bd = mask_bd.astype(BF16)
    mask_hc = (_iota((TL, QW), 0) // C) == (_iota((TL, QW), 1) // HEAD)
    mask_tt = (_iota((TL, TL), 0) // C) == (_iota((TL, TL), 1) // C)
    tok_t = _iota((C, TL), 0)
    tok_j = _iota((C, TL), 1) % C
    strict = tok_j < tok_t
    incl = tok_j <= tok_t
    eye_all = (tok_j == tok_t).astype(F32)

    def block_diag(x, mask, width):
        x = narrow(x)
        zero = jnp.zeros((), x.dtype)
        if (2 * width) % 128:
            return jnp.where(mask, jnp.concatenate([x] * QUAD, axis=0), zero)
        half = _iota((C, 2 * width), 1) < width
        rows = []
        for h in range(QUAD):
            t = h // 2
            tile = jnp.where(half if h % 2 == 0 else ~half, x[:, t * 2 * width:(t + 1) * 2 * width], zero)
            pad = jnp.zeros((C, 2 * width), x.dtype)
            rows.append(jnp.concatenate([tile if i == t else pad for i in range(QUAD // 2)], axis=1))
        return jnp.concatenate(rows, axis=0)

    def stack_hc(x):
        return block_diag(x, mask_hc, HEAD)

    def stack_tt(x):
        return block_diag(x, mask_tt, C)

    at, rt, bt, kt, v, bg, kg = (ops[n] for n in _OPS_NARROW)
    mm = functools.partial(_mm, passes=passes)
    quads = range(NQ)
    sls = [slice(q * QW, (q + 1) * QW) for q in quads]
    ars = [narrow(jnp.concatenate([at[:, sl], rt[:, sl]], axis=0)) for sl in sls]
    v_sts = [stack_hc(v[:, sl]) for sl in sls]
    abs_ = [mm(ars[q], stack_hc(bt[:, sls[q]]), _NT) for q in quads]
    aks = [mm(ars[q], stack_hc(kt[:, sls[q]]), _NT) for q in quads]
    pws = [jnp.where(strict, abs_[q][:C], 0.0) for q in quads]
    a_rbs = [jnp.where(incl, abs_[q][C:], 0.0) for q in quads]
    a_aks = [jnp.where(strict, aks[q][:C], 0.0) for q in quads]
    a_rks = [jnp.where(incl, aks[q][C:], 0.0) for q in quads]
    invs = [eye_all + pws[q] for q in quads]
    pws = [mm(pws[q], stack_tt(pws[q])) for q in quads]
    for _ in range(n_levels - 1):
        bds = [stack_tt(pws[q]) for q in quads]
        res = [mm(jnp.concatenate([pws[q], invs[q]], axis=0), bds[q]) for q in quads]
        pws = [res[q][:C] for q in quads]
        invs = [invs[q] + res[q][C:] for q in quads]
    invs = [invs[q] + mm(invs[q], stack_tt(pws[q])) for q in quads]
    arss = [mm(ars[q], states[q], _NT) for q in quads]
    avs = [mm(jnp.concatenate([a_aks[q], a_rks[q]], axis=0), v_sts[q]) for q in quads]
    ws = [arss[q][:C] + avs[q][:C] for q in quads]
    us = [mm(invs[q], stack_hc(ws[q])) for q in quads]
    ys = [arss[q][C:] + avs[q][C:] + mm(a_rbs[q], stack_hc(us[q])) for q in quads]
    new_states = []
    for q in quads:
        if v.dtype == BF16:
            uv = jnp.concatenate([us[q].astype(BF16), v[:, sls[q]]], axis=0)
        else:
            uv = narrow(jnp.concatenate([us[q], v[:, sls[q]]], axis=0))
        bk = jnp.concatenate([bg[:, sls[q]], kg[:, sls[q]]], axis=0)
        new_states.append(states[q] * ops["g_tot"][:, sls[q]] + jnp.where(mask_bd, mm(uv, bk, _TN), 0.0))
    y = jnp.concatenate(ys, axis=1)

    (ysum,) = _headsum([y], ones_bd)
    yc = y - ysum * (1.0 / HEAD)
    (vsum,) = _headsum([yc * yc], ones_bd)
    yn = yc * lax.rsqrt(vsum * (1.0 / HEAD) + GN_EPS) * lng_ref[...] + lnb_ref[...]
    return (yn + ops["bonus"]) * ops["gate"], new_states


def _expand_state(s0_ref, q):
    zero_blk = jnp.zeros((HEAD, HEAD), F32)
    return jnp.concatenate(
        [jnp.concatenate([s0_ref[0, QUAD * q + h] if j == h else zero_blk for j in range(QUAD)], axis=1)
         for h in range(QUAD)], axis=0)


def _store_state(so_ref, q, s):
    for h in range(QUAD):
        so_ref[0, QUAD * q + h] = s[h * HEAD:(h + 1) * HEAD, h * HEAD:(h + 1) * HEAD]


def _wkv_kernel(p_ref, sp_ref, s0_ref, mu_ref, w0_ref, w2_ref, a0_ref, a2_ref, kk_ref, ka_ref, rk_ref,
                lng_ref, lnb_ref, o_ref, sho_ref, so_ref, carry_ref, st_ref, *, t_valid, n_levels, passes):
    c_idx = pl.program_id(1)
    n_chunks = pl.num_programs(1)
    NQ = st_ref.shape[0]
    narrow = (lambda x: x.astype(BF16)) if (passes == 1 and p_ref.shape[1] % 16 == 0) else (lambda x: x)
    narrow_core = (lambda x: x.astype(BF16)) if passes == 1 else (lambda x: x)

    @pl.when(c_idx == 0)
    def _init():
        carry_ref[...] = sp_ref[0]
        for q in range(NQ):
            st_ref[q] = _expand_state(s0_ref, q)

    prm = (mu_ref, w0_ref, w2_ref, a0_ref, a2_ref, kk_ref, ka_ref, rk_ref)
    ops = _wkv_prep(p_ref[0], carry_ref[...], prm, t_valid=t_valid, narrow=narrow)
    carry_ref[...] = ops["last"]
    o, new_states = _wkv_core(ops, [st_ref[q] for q in range(NQ)], lng_ref, lnb_ref,
                              n_levels=n_levels, passes=passes, narrow=narrow_core)
    for q in range(NQ):
        st_ref[q] = new_states[q]
    o_ref[0] = o

    @pl.when(c_idx == n_chunks - 1)
    def _finish():
        sho_ref[0] = ops["last"]
        for q in range(NQ):
            _store_state(so_ref, q, new_states[q])


def _wkv_pipe_kernel(p_ref, sp_ref, s0_ref, mu_ref, w0_ref, w2_ref, a0_ref, a2_ref, kk_ref, ka_ref, rk_ref,
                     lng_ref, lnb_ref, oe_ref, oo_ref, sho_ref, so_ref, carry_ref, st_ref, gt_ref, *pend_refs,
                     n_levels, passes):
    j = pl.program_id(1)
    last_step = pl.num_programs(1) - 1
    C = oe_ref.shape[2]
    NQ = st_ref.shape[0]
    narrow = (lambda x: x.astype(BF16)) if passes == 1 else (lambda x: x)
    names = _OPS_NARROW + _OPS_F32
    pend = dict(zip(names, pend_refs))

    @pl.when(j == 0)
    def _init():
        carry_ref[...] = sp_ref[0]
        for q in range(NQ):
            st_ref[q] = _expand_state(s0_ref, q)
        for n in names:
            pend[n][...] = jnp.zeros(pend[n].shape, pend[n].dtype)
        gt_ref[...] = jnp.ones(gt_ref.shape, F32)

    prm = (mu_ref, w0_ref, w2_ref, a0_ref, a2_ref, kk_ref, ka_ref, rk_ref)
    core = functools.partial(_wkv_core, lng_ref=lng_ref, lnb_ref=lnb_ref, n_levels=n_levels, passes=passes,
                             narrow=narrow)
    ops_prev = {n: pend[n][...] for n in names}
    ops_prev["g_tot"] = gt_ref[...]
    p = p_ref[0]
    ops_a = _wkv_prep(p[:C], carry_ref[...], prm, t_valid=C, narrow=narrow)
    o_prev, states_a = core(ops_prev, [st_ref[q] for q in range(NQ)])
    ops_b = _wkv_prep(p[C:], ops_a["last"], prm, t_valid=C, narrow=narrow)
    o_a, states_b = core(ops_a, states_a)

    for n in names:
        pend[n][...] = ops_b[n]
    gt_ref[...] = ops_b["g_tot"]
    carry_ref[...] = ops_b["last"]
    for q in range(NQ):
        st_ref[q] = states_b[q]
    oo_ref[0, 0] = o_prev

    @pl.when(j < last_step)
    def _even_out():
        oe_ref[0, 0] = o_a

    @pl.when(j == last_step - 1)
    def _shift_out():
        sho_ref[0] = ops_b["last"]

    @pl.when(j == last_step)
    def _state_out():
        for q in range(NQ):
            _store_state(so_ref, q, states_a[q])


def _wkv_param_specs(e, shift):
    row = lambda n: pl.BlockSpec((1, n), lambda i, j: (0, 0))
    lora = pl.BlockSpec((2 * LORA, e), lambda i, j: (0, 0))
    return [row(shift), row(e), lora, row(e), lora, row(e), row(e), row(e), row(e), row(e)]


def _wkv_param_args(prm):
    return (prm["mu"], prm["w0"], prm["w2p"], prm["a0"], prm["a2p"], prm["kk"], prm["ka"], prm["rk"],
            prm["lng"], prm["lnb"])


def _wkv(proj3d, shift_prev, s0, prm, *, chunk, t_valid, n_levels, passes):
    b, t, cols = proj3d.shape
    e = prm["w0"].shape[1]
    shift = cols - e
    nq = e // QW
    nh = e // HEAD
    kern = functools.partial(_wkv_kernel, t_valid=t_valid, n_levels=n_levels, passes=passes)
    o, sho, so = pl.pallas_call(
        kern,
        grid=(b, t // chunk),
        in_specs=[pl.BlockSpec((1, chunk, cols), lambda i, j: (i, j, 0)),
                  pl.BlockSpec((1, 1, shift), lambda i, j: (i, 0, 0)),
                  pl.BlockSpec((1, nh, HEAD, HEAD), lambda i, j: (i, 0, 0, 0))] + _wkv_param_specs(e, shift),
        out_specs=[pl.BlockSpec((1, chunk, e), lambda i, j: (i, j, 0)),
                   pl.BlockSpec((1, 1, shift), lambda i, j: (i, 0, 0)),
                   pl.BlockSpec((1, nh, HEAD, HEAD), lambda i, j: (i, 0, 0, 0))],
        out_shape=[jax.ShapeDtypeStruct((b, t, e), F32),
                   jax.ShapeDtypeStruct((b, 1, shift), F32),
                   jax.ShapeDtypeStruct((b, nh, HEAD, HEAD), F32)],
        scratch_shapes=[pltpu.VMEM((1, shift), F32),
                        pltpu.VMEM((nq, QW, QW), F32)],
        compiler_params=pltpu.CompilerParams(dimension_semantics=("arbitrary", "arbitrary"),
                                             vmem_limit_bytes=VMEM_LIMIT),
        name="wkv",
    )(proj3d, shift_prev.reshape(b, 1, shift), s0, *_wkv_param_args(prm))
    return o, sho.reshape(b, shift), so


def _wkv_pipe(proj3d, shift_prev, s0, prm, *, chunk, n_levels, passes):
    b, t, cols = proj3d.shape
    e = prm["w0"].shape[1]
    shift = cols - e
    nq = e // QW
    nh = e // HEAD
    npair = t // (2 * chunk)
    op_dtype = BF16 if passes == 1 else F32
    kern = functools.partial(_wkv_pipe_kernel, n_levels=n_levels, passes=passes)
    oe, oo, sho, so = pl.pallas_call(
        kern,
        grid=(b, npair + 1),
        in_specs=[pl.BlockSpec((1, 2 * chunk, cols), lambda i, j: (i, jnp.minimum(j, npair - 1), 0)),
                  pl.BlockSpec((1, 1, shift), lambda i, j: (i, 0, 0)),
                  pl.BlockSpec((1, nh, HEAD, HEAD), lambda i, j: (i, 0, 0, 0))] + _wkv_param_specs(e, shift),
        out_specs=[pl.BlockSpec((1, 1, chunk, e), lambda i, j: (i, jnp.minimum(j, npair - 1), 0, 0)),
                   pl.BlockSpec((1, 1, chunk, e), lambda i, j: (i, jnp.maximum(j - 1, 0), 0, 0)),
                   pl.BlockSpec((1, 1, shift), lambda i, j: (i, 0, 0)),
                   pl.BlockSpec((1, nh, HEAD, HEAD), lambda i, j: (i, 0, 0, 0))],
        out_shape=[jax.ShapeDtypeStruct((b, npair, chunk, e), F32),
                   jax.ShapeDtypeStruct((b, npair, chunk, e), F32),
                   jax.ShapeDtypeStruct((b, 1, shift), F32),
                   jax.ShapeDtypeStruct((b, nh, HEAD, HEAD), F32)],
        scratch_shapes=([pltpu.VMEM((1, shift), F32),
                         pltpu.VMEM((nq, QW, QW), F32),
                         pltpu.VMEM((1, e), F32)]
                        + [pltpu.VMEM((chunk, e), op_dtype) for _ in _OPS_NARROW]
                        + [pltpu.VMEM((chunk, e), F32) for _ in _OPS_F32]),
        compiler_params=pltpu.CompilerParams(dimension_semantics=("arbitrary", "arbitrary"),
                                             vmem_limit_bytes=VMEM_LIMIT),
        name="wkv_pipe",
    )(proj3d, shift_prev.reshape(b, 1, shift), s0, *_wkv_param_args(prm))
    return oe, oo, sho.reshape(b, shift), so


def _out_residual_kernel(o_ref, w_ref, x_ref, y_ref):
    y_ref[...] = x_ref[...] + jnp.dot(o_ref[...].astype(BF16), w_ref[...], preferred_element_type=F32)


def _out_residual(o2d, w_bf16, x2d, tm):
    n, e = o2d.shape
    d = x2d.shape[1]
    return pl.pallas_call(
        _out_residual_kernel,
        grid=(n // tm,),
        in_specs=[pl.BlockSpec((tm, e), lambda i: (i, 0)),
                  pl.BlockSpec((e, d), lambda i: (0, 0)),
                  pl.BlockSpec((tm, d), lambda i: (i, 0))],
        out_specs=pl.BlockSpec((tm, d), lambda i: (i, 0)),
        out_shape=jax.ShapeDtypeStruct((n, d), F32),
        compiler_params=pltpu.CompilerParams(dimension_semantics=("arbitrary",), vmem_limit_bytes=VMEM_LIMIT),
        name="out_residual",
    )(o2d, w_bf16, x2d)


def _out_residual_pairs_kernel(oe_ref, oo_ref, w_ref, x_ref, y_ref):
    npairs = oe_ref.shape[1]
    rows = []
    for i in range(npairs):
        rows += [oe_ref[0, i].astype(BF16), oo_ref[0, i].astype(BF16)]
    o = jnp.concatenate(rows, axis=0)
    y_ref[...] = x_ref[...] + jnp.dot(o, w_ref[...], preferred_element_type=F32)


def _out_residual_pairs(oe, oo, w_bf16, x2d, pairs):
    b, npair, chunk, e = oe.shape
    d = x2d.shape[1]
    tm = pairs * 2 * chunk
    steps = npair // pairs
    return pl.pallas_call(
        _out_residual_pairs_kernel,
        grid=(b, steps),
        in_specs=[pl.BlockSpec((1, pairs, chunk, e), lambda i, j: (i, j, 0, 0)),
                  pl.BlockSpec((1, pairs, chunk, e), lambda i, j: (i, j, 0, 0)),
                  pl.BlockSpec((e, d), lambda i, j: (0, 0)),
                  pl.BlockSpec((tm, d), lambda i, j: (i * steps + j, 0))],
        out_specs=pl.BlockSpec((tm, d), lambda i, j: (i * steps + j, 0)),
        out_shape=jax.ShapeDtypeStruct(x2d.shape, F32),
        compiler_params=pltpu.CompilerParams(dimension_semantics=("arbitrary", "arbitrary"),
                                             vmem_limit_bytes=VMEM_LIMIT),
        name="out_residual_pairs",
    )(oe, oo, w_bf16, x2d)


def _gmlp_kernel(x_ref, g_ref, win_ref, vg_ref, vb_ref, wm_ref, bs_ref, wout_ref, nf_ref, *out_refs, emit_v):
    y_ref = out_refs[0]
    tm = x_ref.shape[0]
    e = vg_ref.shape[1]
    gd = e // GM_GROUPS
    x = x_ref[...]
    h = (x * lax.rsqrt(jnp.mean(x * x, axis=-1, keepdims=True) + NORM_EPS) * g_ref[...]).astype(BF16)
    proj = jnp.dot(h, win_ref[...], preferred_element_type=F32)
    u = jax.nn.gelu(proj[:, :e])
    vf = jax.nn.gelu(proj[:, e:2 * e])
    z = proj[:, 2 * e:]
    vm = jnp.mean(vf, axis=-1, keepdims=True)
    vc = vf - vm
    vv = jnp.mean(vc * vc, axis=-1, keepdims=True)
    vn = vc * lax.rsqrt(vv + LN_EPS) * vg_ref[...] + vb_ref[...]
    if emit_v:
        out_refs[1][...] = vn
    vn16 = vn.astype(BF16)
    causal = _iota((GM_CHUNK, GM_CHUNK), 0) >= _iota((GM_CHUNK, GM_CHUNK), 1)
    rows = []
    for j in range(tm // GM_CHUNK):
        cols = []
        for gi in range(GM_GROUPS):
            wm = jnp.where(causal, wm_ref[gi], 0.0).astype(BF16)
            blk = vn16[j * GM_CHUNK:(j + 1) * GM_CHUNK, gi * gd:(gi + 1) * gd]
            cols.append(jnp.dot(wm, blk, preferred_element_type=F32) + bs_ref[gi])
        rows.append(jnp.concatenate(cols, axis=1))
    mixed = jnp.concatenate(rows, axis=0)
    o = u * mixed * (z * _sigmoid(z))
    x2 = x + jnp.dot(o.astype(BF16), wout_ref[...], preferred_element_type=F32)
    y_ref[...] = x2 * lax.rsqrt(jnp.mean(x2 * x2, axis=-1, keepdims=True) + NORM_EPS) * nf_ref[...]


def _gmlp(x2d, g, win_bf16, vg, vb, wmix, bias, wout_bf16, nf, *, tm, emit_v):
    n, d = x2d.shape
    e = vg.shape[0]
    const2 = lambda shape: pl.BlockSpec(shape, lambda i: (0, 0))
    out_shape = [jax.ShapeDtypeStruct((n, d), F32)]
    out_specs = [pl.BlockSpec((tm, d), lambda i: (i, 0))]
    if emit_v:
        out_shape.append(jax.ShapeDtypeStruct((n, e), F32))
        out_specs.append(pl.BlockSpec((tm, e), lambda i: (i, 0)))
    outs = pl.pallas_call(
        functools.partial(_gmlp_kernel, emit_v=emit_v),
        grid=(n // tm,),
        in_specs=[pl.BlockSpec((tm, d), lambda i: (i, 0)),
                  const2((1, d)),
                  pl.BlockSpec((d, 3 * e), lambda i: (0, 0), pipeline_mode=pl.Buffered(1)),
                  const2((1, e)), const2((1, e)),
                  pl.BlockSpec((GM_GROUPS, GM_CHUNK, GM_CHUNK), lambda i: (0, 0, 0)),
                  pl.BlockSpec((GM_GROUPS, GM_CHUNK, 1), lambda i: (0, 0, 0)),
                  pl.BlockSpec((e, d), lambda i: (0, 0), pipeline_mode=pl.Buffered(1)),
                  const2((1, d))],
        out_specs=out_specs,
        out_shape=out_shape,
        compiler_params=pltpu.CompilerParams(dimension_semantics=("arbitrary",), vmem_limit_bytes=VMEM_LIMIT),
        name="gmlp",
    )(x2d, g.reshape(1, d), win_bf16, vg.reshape(1, e), vb.reshape(1, e), wmix,
      bias.reshape(GM_GROUPS, GM_CHUNK, 1), wout_bf16, nf.reshape(1, d))
    return outs


def _row_tile(n, pref):
    t = pref
    while n % t:
        t //= 2
    return t


def kernel(x_prompt, x_sample, state_shift, state_wkv, norm_g, norm_f, rw_in, rw_mu, rw_w0, rw_w2, rw_a0, rw_a2, rw_kk, rw_ka, rw_rk, rw_lnx_g, rw_lnx_b, rw_out, gm_in, gm_vg, gm_vb, gm_ws, gm_bs, gm_out):
    bp, tp, d = x_prompt.shape
    bs_, ts, _ = x_sample.shape
    e = rw_w0.shape[1]
    nh = e // HEAD
    shift = 3 * e + 2 * LORA
    xp = x_prompt.reshape(bp * tp, d)
    xs = x_sample.reshape(bs_ * ts, d)

    zeros_l = jnp.zeros((LORA, e), F32)
    prm = dict(
        mu=rw_mu[0].reshape(1, shift), w0=rw_w0[0].reshape(1, e), a0=rw_a0[0].reshape(1, e),
        w2p=jnp.concatenate([rw_w2[0], zeros_l], axis=0).astype(BF16),
        a2p=jnp.concatenate([zeros_l, rw_a2[0]], axis=0).astype(BF16),
        kk=rw_kk[0].reshape(1, e), ka=rw_ka[0].reshape(1, e), rk=rw_rk[0].reshape(1, e),
        lng=rw_lnx_g[0].reshape(1, e), lnb=rw_lnx_b[0].reshape(1, e))
    w_in = rw_in[0].astype(BF16)
    w_out = rw_out[0].astype(BF16)

    proj_p = _norm_proj(xp, norm_g[0], w_in, _row_tile(bp * tp, 256)).reshape(bp, tp, shift + e)
    proj_s = _norm_proj(xs, norm_g[0], w_in, _row_tile(bs_ * ts, 256)).reshape(bs_, ts, shift + e)

    chunk_p = 64
    oe_p, oo_p, shift_p, wkv_p = _wkv_pipe(proj_p, jnp.zeros((bp, shift), F32),
                                           jnp.zeros((bp, nh, HEAD, HEAD), F32), prm,
                                           chunk=chunk_p, n_levels=5, passes=1)
    chunk_s = 8
    proj_s = jnp.pad(proj_s, ((0, 0), (0, chunk_s - ts), (0, 0)))
    o_s, shift_s, wkv_s = _wkv(proj_s, state_shift[0], state_wkv[0], prm,
                               chunk=chunk_s, t_valid=ts, n_levels=max(1, math.ceil(math.log2(ts)) - 1), passes=1)
    o_s = o_s[:, :ts]

    npair = tp // (2 * chunk_p)
    x1p = _out_residual_pairs(oe_p, oo_p, w_out, xp, _row_tile(npair, 4))
    x1s = _out_residual(o_s.reshape(bs_ * ts, e), w_out, xs, _row_tile(bs_ * ts, 512))

    g_in = gm_in[0].astype(BF16)
    g_out = gm_out[0].astype(BF16)
    wm_p = gm_ws[0]
    reps = GM_CHUNK // ts
    eye = jnp.eye(reps, dtype=F32)
    wm_s = jax.vmap(lambda w: jnp.kron(eye, w))(wm_p[:, :ts, :ts])
    bias_s = jnp.tile(gm_bs[0][:, :ts], (1, reps))

    (y_p,) = _gmlp(x1p, norm_g[1], g_in, gm_vg[0], gm_vb[0], wm_p, gm_bs[0], g_out, norm_f,
                   tm=_row_tile(bp * tp, 256), emit_v=False)
    y_s, v_s = _gmlp(x1s, norm_g[1], g_in, gm_vg[0], gm_vb[0], wm_s, bias_s, g_out, norm_f,
                     tm=_row_tile(bs_ * ts, 256), emit_v=True)

    return (y_p.reshape(bp, tp, d), y_s.reshape(bs_, ts, d),
            shift_p[None], wkv_p[None], shift_s[None], wkv_s[None],
            v_s.reshape(1, bs_, ts, e))
```

```python
import functools
import math

import jax
import jax.numpy as jnp
from jax import lax
from jax.experimental import pallas as pl
from jax.experimental.pallas import tpu as pltpu

F32 = jnp.float32
BF16 = jnp.bfloat16

HEAD = 64
QUAD = 4
QW = QUAD * HEAD
LORA = 64
NORM_EPS = 1e-6
LN_EPS = 1e-5
GN_EPS = 64e-5
GM_CHUNK = 128
GM_GROUPS = 8
VMEM_LIMIT = 56 * 1024 * 1024


def _split2(x):
    hi = x.astype(BF16)
    lo = (x - hi.astype(F32)).astype(BF16)
    return hi, lo


_NN = (((1,), (0,)), ((), ()))
_NT = (((1,), (1,)), ((), ()))
_TN = (((0,), (0,)), ((), ()))


def _mm(a, b, dims=_NN, passes=1):
    if passes == 1:
        return lax.dot_general(a.astype(BF16), b.astype(BF16), dims, preferred_element_type=F32)
    ah, al = _split2(a)
    bh, bl = _split2(b)
    d = functools.partial(lax.dot_general, dimension_numbers=dims, preferred_element_type=F32)
    return d(ah, bh) + (d(ah, bl) + d(al, bh))


def _mm_sum_rhs(a, b_bf16, dims=_NN):
    d = functools.partial(lax.dot_general, dimension_numbers=dims, preferred_element_type=F32)
    h, l = _split2(a)
    return d(h, b_bf16) + d(l, b_bf16)


def _mm_sum_lhs(a_bf16, b, dims=_NN):
    d = functools.partial(lax.dot_general, dimension_numbers=dims, preferred_element_type=F32)
    h, l = _split2(b)
    return d(a_bf16, h) + d(a_bf16, l)


def _iota(shape, dim):
    return lax.broadcasted_iota(jnp.int32, shape, dim)


def _sigmoid(x):
    return 0.5 * jnp.tanh(0.5 * x) + 0.5


def _norm_proj_kernel(x_ref, g_ref, w_ref, o_ref):
    x = x_ref[...]
    y = x * lax.rsqrt(jnp.mean(x * x, axis=-1, keepdims=True) + NORM_EPS)
    h = (y * g_ref[...]).astype(BF16)
    o_ref[...] = jnp.dot(h, w_ref[...], preferred_element_type=F32)


def _norm_proj(x2d, g, w_bf16, tm):
    n, d = x2d.shape
    cols = w_bf16.shape[1]
    return pl.pallas_call(
        _norm_proj_kernel,
        grid=(n // tm,),
        in_specs=[pl.BlockSpec((tm, d), lambda i: (i, 0)),
                  pl.BlockSpec((1, d), lambda i: (0, 0)),
                  pl.BlockSpec((d, cols), lambda i: (0, 0), pipeline_mode=pl.Buffered(1))],
        out_specs=pl.BlockSpec((tm, cols), lambda i: (i, 0)),
        out_shape=jax.ShapeDtypeStruct((n, cols), F32),
        compiler_params=pltpu.CompilerParams(dimension_semantics=("arbitrary",), vmem_limit_bytes=VMEM_LIMIT),
        name="norm_proj",
    )(x2d, g.reshape(1, d), w_bf16)


_OPS_NARROW = ("at", "rt", "bt", "kt", "v", "bg", "kg")
_OPS_F32 = ("bonus", "gate")


def _headsum(xs, ones_bd):
    e = xs[0].shape[1]
    nq = e // QW
    rows = [x[:, q * QW:(q + 1) * QW] for x in xs for q in range(nq)]
    stacked = jnp.concatenate(rows, axis=0)
    s = _mm_sum_rhs(stacked, ones_bd)
    c = xs[0].shape[0]
    outs = []
    for i in range(len(xs)):
        outs.append(jnp.concatenate([s[(i * nq + q) * c:(i * nq + q + 1) * c] for q in range(nq)], axis=1))
    return outs


def _wkv_prep(p, carry, prm, *, t_valid, narrow):
    mu_ref, w0_ref, w2_ref, a0_ref, a2_ref, kk_ref, ka_ref, rk_ref = prm
    C = p.shape[0]
    E = w0_ref.shape[1]
    SHIFT = 3 * E + 2 * LORA
    ones_bd = ((_iota((QW, QW), 0) // HEAD) == (_iota((QW, QW), 1) // HEAD)).astype(BF16)

    sh = p[:, :SHIFT]
    z = p[:, SHIFT:]
    row1 = _iota((C, 1), 0)
    prev = jnp.where(row1 == 0, carry, pltpu.roll(sh, 1, 0))
    xm = sh + (prev - sh) * mu_ref[...]
    last = sh[t_valid - 1:t_valid]

    r = xm[:, :E]
    k = xm[:, E:2 * E]
    v = xm[:, 2 * E:3 * E]
    x_lora = xm[:, 3 * E:]
    lane_l = _iota((C, 2 * LORA), 1)
    t_lora = jnp.where(lane_l < LORA, jnp.tanh(x_lora), x_lora)
    wf = w0_ref[...] + _mm(t_lora, w2_ref[...])
    af = a0_ref[...] + _mm(x_lora, a2_ref[...])
    ld = -math.exp(-0.5) * _sigmoid(wf)
    a = _sigmoid(af)

    valid = row1 < t_valid
    if t_valid < C:
        ld = jnp.where(valid, ld, 0.0)
        v = jnp.where(valid, v, 0.0)

    kk0 = k * kk_ref[...]
    k2 = k * (1.0 + (a - 1.0) * ka_ref[...])
    ssq, rkk = _headsum([kk0 * kk0, r * k2 * rk_ref[...]], ones_bd)
    kk = kk0 * jnp.minimum(lax.rsqrt(ssq), 1e12)

    tri = (_iota((C, C), 0) >= _iota((C, C), 1)).astype(BF16)
    cum = _mm_sum_lhs(tri, ld)
    cum_last = cum[C - 1:C]
    g_rem = jnp.exp(cum_last - cum)
    g_inv = jnp.exp(-cum)
    kka = kk * a
    ops = dict(
        at=-kk * jnp.exp(cum - ld),
        rt=r * jnp.exp(cum),
        bt=kka * g_inv,
        kt=k2 * g_inv,
        v=v,
        bg=kka * g_rem,
        kg=k2 * g_rem)
    if t_valid < C:
        ops = {n: (x if n in ("rt", "v") else jnp.where(valid, x, 0.0)) for n, x in ops.items()}
    ops = {n: narrow(x) for n, x in ops.items()}
    ops.update(bonus=rkk * v, gate=z * _sigmoid(z), g_tot=jnp.exp(cum_last), last=last)
    return ops


def _wkv_core(ops, states, lng_ref, lnb_ref, *, n_levels, passes, narrow):
    C, E = ops["at"].shape
    TL = QUAD * C
    NQ = E // QW
    mask_bd = (_iota((QW, QW), 0) // HEAD) == (_iota((QW, QW), 1) // HEAD)
    ones_bd = mask_bd.astype(BF16)
    mask_hc = (_iota((TL, QW), 0) // C) == (_iota((TL, QW), 1) // HEAD)
    mask_tt = (_iota((TL, TL), 0) // C) == (_iota((TL, TL), 1) // C)
    tok_t = _iota((C, TL), 0)
    tok_j = _iota((C, TL), 1) % C
    strict = tok_j < tok_t
    incl = tok_j <= tok_t
    eye_all = (tok_j == tok_t).astype(F32)

    def block_diag(x, mask, width):
        x = narrow(x)
        zero = jnp.zeros((), x.dtype)
        if (2 * width) % 128:
            return jnp.where(mask, jnp.concatenate([x] * QUAD, axis=0), zero)
        half = _iota((C, 2 * width), 1) < width
        rows = []
        for h in range(QUAD):
            t = h // 2
            tile = jnp.where(half if h % 2 == 0 else ~half, x[:, t * 2 * width:(t + 1) * 2 * width], zero)
            pad = jnp.zeros((C, 2 * width), x.dtype)
            rows.append(jnp.concatenate([tile if i == t else pad for i in range(QUAD // 2)], axis=1))
        return jnp.concatenate(rows, axis=0)

    def stack_hc(x):
        return block_diag(x, mask_hc, HEAD)

    def stack_tt(x):
        return block_diag(x, mask_tt, C)

    at, rt, bt, kt, v, bg, kg = (ops[n] for n in _OPS_NARROW)
    mm = functools.partial(_mm, passes=passes)
    quads = range(NQ)
    sls = [slice(q * QW, (q + 1) * QW) for q in quads]
    ars = [narrow(jnp.concatenate([at[:, sl], rt[:, sl]], axis=0)) for sl in sls]
    v_sts = [stack_hc(v[:, sl]) for sl in sls]
    abs_ = [mm(ars[q], stack_hc(bt[:, sls[q]]), _NT) for q in quads]
    aks = [mm(ars[q], stack_hc(kt[:, sls[q]]), _NT) for q in quads]
    pws = [jnp.where(strict, abs_[q][:C], 0.0) for q in quads]
    a_rbs = [jnp.where(incl, abs_[q][C:], 0.0) for q in quads]
    a_aks = [jnp.where(strict, aks[q][:C], 0.0) for q in quads]
    a_rks = [jnp.where(incl, aks[q][C:], 0.0) for q in quads]
    invs = [eye_all + pws[q] for q in quads]
    pws = [mm(pws[q], stack_tt(pws[q])) for q in quads]
    for _ in range(n_levels - 1):
        bds = [stack_tt(pws[q]) for q in quads]
        res = [mm(jnp.concatenate([pws[q], invs[q]], axis=0), bds[q]) for q in quads]
        pws = [res[q][:C] for q in quads]
        invs = [invs[q] + res[q][C:] for q in quads]
    invs = [invs[q] + mm(invs[q], stack_tt(pws[q])) for q in quads]
    arss = [mm(ars[q], states[q], _NT) for q in quads]
    avs = [mm(jnp.concatenate([a_aks[q], a_rks[q]], axis=0), v_sts[q]) for q in quads]
    ws = [arss[q][:C] + avs[q][:C] for q in quads]
    us = [mm(invs[q], stack_hc(ws[q])) for q in quads]
    ys = [arss[q][C:] + avs[q][C:] + mm(a_rbs[q], stack_hc(us[q])) for q in quads]
    new_states = []
    for q in quads:
        if v.dtype == BF16:
            uv = jnp.concatenate([us[q].astype(BF16), v[:, sls[q]]], axis=0)
        else:
            uv = narrow(jnp.concatenate([us[q], v[:, sls[q]]], axis=0))
        bk = jnp.concatenate([bg[:, sls[q]], kg[:, sls[q]]], axis=0)
        new_states.append(states[q] * ops["g_tot"][:, sls[q]] + jnp.where(mask_bd, mm(uv, bk, _TN), 0.0))
    y = jnp.concatenate(ys, axis=1)

    (ysum,) = _headsum([y], ones_bd)
    yc = y - ysum * (1.0 / HEAD)
    (vsum,) = _headsum([yc * yc], ones_bd)
    yn = yc * lax.rsqrt(vsum * (1.0 / HEAD) + GN_EPS) * lng_ref[...] + lnb_ref[...]
    return (yn + ops["bonus"]) * ops["gate"], new_states


def _expand_state(s0_ref, q):
    zero_blk = jnp.zeros((HEAD, HEAD), F32)
    return jnp.concatenate(
        [jnp.concatenate([s0_ref[0, QUAD * q + h] if j == h else zero_blk for j in range(QUAD)], axis=1)
         for h in range(QUAD)], axis=0)


def _store_state(so_ref, q, s):
    for h in range(QUAD):
        so_ref[0, QUAD * q + h] = s[h * HEAD:(h + 1) * HEAD, h * HEAD:(h + 1) * HEAD]


def _wkv_kernel(p_ref, sp_ref, s0_ref, mu_ref, w0_ref, w2_ref, a0_ref, a2_ref, kk_ref, ka_ref, rk_ref,
                lng_ref, lnb_ref, o_ref, sho_ref, so_ref, carry_ref, st_ref, *, t_valid, n_levels, passes):
    c_idx = pl.program_id(1)
    n_chunks = pl.num_programs(1)
    NQ = st_ref.shape[0]
    narrow = (lambda x: x.astype(BF16)) if (passes == 1 and p_ref.shape[1] % 16 == 0) else (lambda x: x)
    narrow_core = (lambda x: x.astype(BF16)) if passes == 1 else (lambda x: x)

    @pl.when(c_idx == 0)
    def _init():
        carry_ref[...] = sp_ref[0]
        for q in range(NQ):
            st_ref[q] = _expand_state(s0_ref, q)

    prm = (mu_ref, w0_ref, w2_ref, a0_ref, a2_ref, kk_ref, ka_ref, rk_ref)
    ops = _wkv_prep(p_ref[0], carry_ref[...], prm, t_valid=t_valid, narrow=narrow)
    carry_ref[...] = ops["last"]
    o, new_states = _wkv_core(ops, [st_ref[q] for q in range(NQ)], lng_ref, lnb_ref,
                              n_levels=n_levels, passes=passes, narrow=narrow_core)
    for q in range(NQ):
        st_ref[q] = new_states[q]
    o_ref[0] = o

    @pl.when(c_idx == n_chunks - 1)
    def _finish():
        sho_ref[0] = ops["last"]
        for q in range(NQ):
            _store_state(so_ref, q, new_states[q])


def _wkv_pipe_kernel(p_ref, sp_ref, s0_ref, mu_ref, w0_ref, w2_ref, a0_ref, a2_ref, kk_ref, ka_ref, rk_ref,
                     lng_ref, lnb_ref, oe_ref, oo_ref, sho_ref, so_ref, carry_ref, st_ref, gt_ref, *pend_refs,
                     n_levels, passes):
    j = pl.program_id(1)
    last_step = pl.num_programs(1) - 1
    C = oe_ref.shape[2]
    NQ = st_ref.shape[0]
    narrow = (lambda x: x.astype(BF16)) if passes == 1 else (lambda x: x)
    names = _OPS_NARROW + _OPS_F32
    pend = dict(zip(names, pend_refs))

    @pl.when(j == 0)
    def _init():
        carry_ref[...] = sp_ref[0]
        for q in range(NQ):
            st_ref[q] = _expand_state(s0_ref, q)
        for n in names:
            pend[n][...] = jnp.zeros(pend[n].shape, pend[n].dtype)
        gt_ref[...] = jnp.ones(gt_ref.shape, F32)

    prm = (mu_ref, w0_ref, w2_ref, a0_ref, a2_ref, kk_ref, ka_ref, rk_ref)
    core = functools.partial(_wkv_core, lng_ref=lng_ref, lnb_ref=lnb_ref, n_levels=n_levels, passes=passes,
                             narrow=narrow)
    ops_prev = {n: pend[n][...] for n in names}
    ops_prev["g_tot"] = gt_ref[...]
    p = p_ref[0]
    ops_a = _wkv_prep(p[:C], carry_ref[...], prm, t_valid=C, narrow=narrow)
    o_prev, states_a = core(ops_prev, [st_ref[q] for q in range(NQ)])
    ops_b = _wkv_prep(p[C:], ops_a["last"], prm, t_valid=C, narrow=narrow)
    o_a, states_b = core(ops_a, states_a)

    for n in names:
        pend[n][...] = ops_b[n]
    gt_ref[...] = ops_b["g_tot"]
    carry_ref[...] = ops_b["last"]
    for q in range(NQ):
        st_ref[q] = states_b[q]
    oo_ref[0, 0] = o_prev

    @pl.when(j < last_step)
    def _even_out():
        oe_ref[0, 0] = o_a

    @pl.when(j == last_step - 1)
    def _shift_out():
        sho_ref[0] = ops_b["last"]

    @pl.when(j == last_step)
    def _state_out():
        for q in range(NQ):
            _store_state(so_ref, q, states_a[q])


def _wkv_param_specs(e, shift):
    row = lambda n: pl.BlockSpec((1, n), lambda i, j: (0, 0))
    lora = pl.BlockSpec((2 * LORA, e), lambda i, j: (0, 0))
    return [row(shift), row(e), lora, row(e), lora, row(e), row(e), row(e), row(e), row(e)]


def _wkv_param_args(prm):
    return (prm["mu"], prm["w0"], prm["w2p"], prm["a0"], prm["a2p"], prm["kk"], prm["ka"], prm["rk"],
            prm["lng"], prm["lnb"])


def _wkv(proj3d, shift_prev, s0, prm, *, chunk, t_valid, n_levels, passes):
    b, t, cols = proj3d.shape
    e = prm["w0"].shape[1]
    shift = cols - e
    nq = e // QW
    nh = e // HEAD
    kern = functools.partial(_wkv_kernel, t_valid=t_valid, n_levels=n_levels, passes=passes)
    o, sho, so = pl.pallas_call(
        kern,
        grid=(b, t // chunk),
        in_specs=[pl.BlockSpec((1, chunk, cols), lambda i, j: (i, j, 0)),
                  pl.BlockSpec((1, 1, shift), lambda i, j: (i, 0, 0)),
                  pl.BlockSpec((1, nh, HEAD, HEAD), lambda i, j: (i, 0, 0, 0))] + _wkv_param_specs(e, shift),
        out_specs=[pl.BlockSpec((1, chunk, e), lambda i, j: (i, j, 0)),
                   pl.BlockSpec((1, 1, shift), lambda i, j: (i, 0, 0)),
                   pl.BlockSpec((1, nh, HEAD, HEAD), lambda i, j: (i, 0, 0, 0))],
        out_shape=[jax.ShapeDtypeStruct((b, t, e), F32),
                   jax.ShapeDtypeStruct((b, 1, shift), F32),
                   jax.ShapeDtypeStruct((b, nh, HEAD, HEAD), F32)],
        scratch_shapes=[pltpu.VMEM((1, shift), F32),
                        pltpu.VMEM((nq, QW, QW), F32)],
        compiler_params=pltpu.CompilerParams(dimension_semantics=("arbitrary", "arbitrary"),
                                             vmem_limit_bytes=VMEM_LIMIT),
        name="wkv",
    )(proj3d, shift_prev.reshape(b, 1, shift), s0, *_wkv_param_args(prm))
    return o, sho.reshape(b, shift), so


def _wkv_pipe(proj3d, shift_prev, s0, prm, *, chunk, n_levels, passes):
    b, t, cols = proj3d.shape
    e = prm["w0"].shape[1]
    shift = cols - e
    nq = e // QW
    nh = e // HEAD
    npair = t // (2 * chunk)
    op_dtype = BF16 if passes == 1 else F32
    kern = functools.partial(_wkv_pipe_kernel, n_levels=n_levels, passes=passes)
    oe, oo, sho, so = pl.pallas_call(
        kern,
        grid=(b, npair + 1),
        in_specs=[pl.BlockSpec((1, 2 * chunk, cols), lambda i, j: (i, jnp.minimum(j, npair - 1), 0)),
                  pl.BlockSpec((1, 1, shift), lambda i, j: (i, 0, 0)),
                  pl.BlockSpec((1, nh, HEAD, HEAD), lambda i, j: (i, 0, 0, 0))] + _wkv_param_specs(e, shift),
        out_specs=[pl.BlockSpec((1, 1, chunk, e), lambda i, j: (i, jnp.minimum(j, npair - 1), 0, 0)),
                   pl.BlockSpec((1, 1, chunk, e), lambda i, j: (i, jnp.maximum(j - 1, 0), 0, 0)),
                   pl.BlockSpec((1, 1, shift), lambda i, j: (i, 0, 0)),
                   pl.BlockSpec((1, nh, HEAD, HEAD), lambda i, j: (i, 0, 0, 0))],
        out_shape=[jax.ShapeDtypeStruct((b, npair, chunk, e), F32),
                   jax.ShapeDtypeStruct((b, npair, chunk, e), F32),
                   jax.ShapeDtypeStruct((b, 1, shift), F32),
                   jax.ShapeDtypeStruct((b, nh, HEAD, HEAD), F32)],
        scratch_shapes=([pltpu.VMEM((1, shift), F32),
                         pltpu.VMEM((nq, QW, QW), F32),
                         pltpu.VMEM((1, e), F32)]
                        + [pltpu.VMEM((chunk, e), op_dtype) for _ in _OPS_NARROW]
                        + [pltpu.VMEM((chunk, e), F32) for _ in _OPS_F32]),
        compiler_params=pltpu.CompilerParams(dimension_semantics=("arbitrary", "arbitrary"),
                                             vmem_limit_bytes=VMEM_LIMIT),
        name="wkv_pipe",
    )(proj3d, shift_prev.reshape(b, 1, shift), s0, *_wkv_param_args(prm))
    return oe, oo, sho.reshape(b, shift), so


HPAIR = 2 * HEAD
_VT = ("nkk", "w", "b", "k", "r", "v")


def _wkv_seq_kernel(pr_ref, pk_ref, pv_ref, pl_ref, pz_ref, sr_ref, sk_ref, sv_ref, sl_ref,
                    mur_ref, muk_ref, muv_ref, mul_ref, w0_ref, w2_ref, a0_ref, a2_ref, kk_ref, ka_ref, rk_ref,
                    lng_ref, lnb_ref, s_ref, o_ref, so_ref, vt_ref, yt_ref):
    T, B, _ = pr_ref.shape
    ones2 = ((_iota((HPAIR, HPAIR), 0) // HEAD) == (_iota((HPAIR, HPAIR), 1) // HEAD)).astype(BF16)

    def shifted(p_ref, prev_ref, mu_ref):
        sh = p_ref[...].reshape(T * B, HPAIR)
        prev = jnp.concatenate([prev_ref[...], sh[:(T - 1) * B]], axis=0)
        return sh + (prev - sh) * mu_ref[...]

    r = shifted(pr_ref, sr_ref, mur_ref)
    k = shifted(pk_ref, sk_ref, muk_ref)
    v = shifted(pv_ref, sv_ref, muv_ref)
    x_lora = shifted(pl_ref, sl_ref, mul_ref)
    z = pz_ref[...].reshape(T * B, HPAIR)
    lane_l = _iota((T * B, 2 * LORA), 1)
    t_lora = jnp.where(lane_l < LORA, jnp.tanh(x_lora), x_lora)
    wf = w0_ref[...] + _mm(t_lora, w2_ref[...])
    af = a0_ref[...] + _mm(x_lora, a2_ref[...])
    w = jnp.exp(-math.exp(-0.5) * _sigmoid(wf))
    a = _sigmoid(af)
    kk0 = k * kk_ref[...]
    k2 = k * (1.0 + (a - 1.0) * ka_ref[...])
    sums = _mm_sum_rhs(jnp.concatenate([kk0 * kk0, r * k2 * rk_ref[...]], axis=0), ones2)
    kk = kk0 * jnp.minimum(lax.rsqrt(sums[:T * B]), 1e12)
    bonus = sums[T * B:] * v
    vecs = dict(nkk=-kk, w=w, b=kk * a, k=k2, r=r, v=v)
    for i, n in enumerate(_VT):
        for t in range(T):
            vt_ref[i, t] = vecs[n][t * B:(t + 1) * B].T

    for h in range(2):
        ch = slice(h * HEAD, (h + 1) * HEAD)

        def body(g, carry, h=h, ch=ch):
            v0 = pl.multiple_of(g * 8, 8)
            vrows = [vt_ref[5, t, pl.ds(h * HEAD + v0, 8), :] for t in range(T)]
            ys = [[] for _ in range(T)]
            for u in range(8):
                s = s_ref[h, v0 + u]
                for t in range(T):
                    sa = jnp.sum(s * vt_ref[0, t, ch, :], axis=0, keepdims=True)
                    s = s * vt_ref[1, t, ch, :] + sa * vt_ref[2, t, ch, :] + vrows[t][u:u + 1] * vt_ref[3, t, ch, :]
                    ys[t].append(jnp.sum(s * vt_ref[4, t, ch, :], axis=0, keepdims=True))
                so_ref[h, v0 + u] = s
            for t in range(T):
                yt_ref[t, pl.ds(h * HEAD + v0, 8), :] = jnp.concatenate(ys[t], axis=0)
            return carry

        lax.fori_loop(0, HEAD // 8, body, 0)

    y = jnp.concatenate([yt_ref[t].T for t in range(T)], axis=0)
    yc = y - _mm_sum_rhs(y, ones2) * (1.0 / HEAD)
    var = _mm_sum_rhs(yc * yc, ones2) * (1.0 / HEAD)
    yn = yc * lax.rsqrt(var + GN_EPS) * lng_ref[...] + lnb_ref[...]
    o_ref[...] = ((yn + bonus) * (z * _sigmoid(z))).reshape(T, B, HPAIR)


def _wkv_seq(proj_t, shift_prev, s_t, prm):
    t, b, cols = proj_t.shape
    e = prm["w0"].shape[1]
    nh = e // HEAD
    kb = e // HPAIR
    bt = 128
    off = dict(r=0, k=kb, v=2 * kb, l=3 * kb)
    pspec = lambda o: pl.BlockSpec((t, bt, HPAIR), lambda i, j, o=o: (0, j, o + i))
    pfix = lambda o: pl.BlockSpec((t, bt, HPAIR), lambda i, j, o=o: (0, j, o))
    sspec = lambda o: pl.BlockSpec((bt, HPAIR), lambda i, j, o=o: (j, o + i))
    sfix = lambda o: pl.BlockSpec((bt, HPAIR), lambda i, j, o=o: (j, o))
    mspec = lambda o: pl.BlockSpec((1, HPAIR), lambda i, j, o=o: (0, o + i))
    mfix = lambda o: pl.BlockSpec((1, HPAIR), lambda i, j, o=o: (0, o))
    row = pl.BlockSpec((1, HPAIR), lambda i, j: (0, i))
    lora = pl.BlockSpec((2 * LORA, HPAIR), lambda i, j: (0, i))
    sblk = pl.BlockSpec((2, HEAD, HEAD, bt), lambda i, j: (i, 0, 0, j))
    return pl.pallas_call(
        _wkv_seq_kernel,
        grid=(nh // 2, b // bt),
        in_specs=[pspec(off["r"]), pspec(off["k"]), pspec(off["v"]), pfix(off["l"]), pspec(off["l"] + 1),
                  sspec(off["r"]), sspec(off["k"]), sspec(off["v"]), sfix(off["l"]),
                  mspec(off["r"]), mspec(off["k"]), mspec(off["v"]), mfix(off["l"]),
                  row, lora, row, lora, row, row, row, row, row, sblk],
        out_specs=[pl.BlockSpec((t, bt, HPAIR), lambda i, j: (0, j, i)), sblk],
        out_shape=[jax.ShapeDtypeStruct((t, b, e), F32), jax.ShapeDtypeStruct(s_t.shape, F32)],
        scratch_shapes=[pltpu.VMEM((len(_VT), t, HPAIR, bt), F32),
                        pltpu.VMEM((t, HPAIR, bt), F32)],
        compiler_params=pltpu.CompilerParams(dimension_semantics=("arbitrary", "arbitrary"),
                                             vmem_limit_bytes=VMEM_LIMIT),
        name="wkv_seq",
    )(proj_t, proj_t, proj_t, proj_t, proj_t, shift_prev, shift_prev, shift_prev, shift_prev,
      prm["mu"], prm["mu"], prm["mu"], prm["mu"], prm["w0"], prm["w2p"], prm["a0"], prm["a2p"],
      prm["kk"], prm["ka"], prm["rk"], prm["lng"], prm["lnb"], s_t)


def _out_residual_kernel(o_ref, w_ref, x_ref, y_ref):
    y_ref[...] = x_ref[...] + jnp.dot(o_ref[...].astype(BF16), w_ref[...], preferred_element_type=F32)


def _out_residual(o2d, w_bf16, x2d, tm):
    n, e = o2d.shape
    d = x2d.shape[1]
    return pl.pallas_call(
        _out_residual_kernel,
        grid=(n // tm,),
        in_specs=[pl.BlockSpec((tm, e), lambda i: (i, 0)),
                  pl.BlockSpec((e, d), lambda i: (0, 0)),
                  pl.BlockSpec((tm, d), lambda i: (i, 0))],
        out_specs=pl.BlockSpec((tm, d), lambda i: (i, 0)),
        out_shape=jax.ShapeDtypeStruct((n, d), F32),
        compiler_params=pltpu.CompilerParams(dimension_semantics=("arbitrary",), vmem_limit_bytes=VMEM_LIMIT),
        name="out_residual",
    )(o2d, w_bf16, x2d)


def _out_residual_pairs_kernel(oe_ref, oo_ref, w_ref, x_ref, y_ref):
    npairs = oe_ref.shape[1]
    rows = []
    for i in range(npairs):
        rows += [oe_ref[0, i].astype(BF16), oo_ref[0, i].astype(BF16)]
    o = jnp.concatenate(rows, axis=0)
    y_ref[...] = x_ref[...] + jnp.dot(o, w_ref[...], preferred_element_type=F32)


def _out_residual_pairs(oe, oo, w_bf16, x2d, pairs):
    b, npair, chunk, e = oe.shape
    d = x2d.shape[1]
    tm = pairs * 2 * chunk
    steps = npair // pairs
    return pl.pallas_call(
        _out_residual_pairs_kernel,
        grid=(b, steps),
        in_specs=[pl.BlockSpec((1, pairs, chunk, e), lambda i, j: (i, j, 0, 0)),
                  pl.BlockSpec((1, pairs, chunk, e), lambda i, j: (i, j, 0, 0)),
                  pl.BlockSpec((e, d), lambda i, j: (0, 0)),
                  pl.BlockSpec((tm, d), lambda i, j: (i * steps + j, 0))],
        out_specs=pl.BlockSpec((tm, d), lambda i, j: (i * steps + j, 0)),
        out_shape=jax.ShapeDtypeStruct(x2d.shape, F32),
        compiler_params=pltpu.CompilerParams(dimension_semantics=("arbitrary", "arbitrary"),
                                             vmem_limit_bytes=VMEM_LIMIT),
        name="out_residual_pairs",
    )(oe, oo, w_bf16, x2d)


def _gmlp_kernel(x_ref, g_ref, win_ref, vg_ref, vb_ref, wm_ref, bs_ref, wout_ref, nf_ref, *out_refs, emit_v):
    y_ref = out_refs[0]
    tm = x_ref.shape[0]
    e = vg_ref.shape[1]
    gd = e // GM_GROUPS
    x = x_ref[...]
    h = (x * lax.rsqrt(jnp.mean(x * x, axis=-1, keepdims=True) + NORM_EPS) * g_ref[...]).astype(BF16)
    proj = jnp.dot(h, win_ref[...], preferred_element_type=F32)
    u = jax.nn.gelu(proj[:, :e])
    vf = jax.nn.gelu(proj[:, e:2 * e])
    z = proj[:, 2 * e:]
    vm = jnp.mean(vf, axis=-1, keepdims=True)
    vc = vf - vm
    vv = jnp.mean(vc * vc, axis=-1, keepdims=True)
    vn = vc * lax.rsqrt(vv + LN_EPS) * vg_ref[...] + vb_ref[...]
    if emit_v:
        out_refs[1][...] = vn
    vn16 = vn.astype(BF16)
    causal = _iota((GM_CHUNK, GM_CHUNK), 0) >= _iota((GM_CHUNK, GM_CHUNK), 1)
    rows = []
    for j in range(tm // GM_CHUNK):
        cols = []
        for gi in range(GM_GROUPS):
            wm = jnp.where(causal, wm_ref[gi], 0.0).astype(BF16)
            blk = vn16[j * GM_CHUNK:(j + 1) * GM_CHUNK, gi * gd:(gi + 1) * gd]
            cols.append(jnp.dot(wm, blk, preferred_element_type=F32) + bs_ref[gi])
        rows.append(jnp.concatenate(cols, axis=1))
    mixed = jnp.concatenate(rows, axis=0)
    o = u * mixed * (z * _sigmoid(z))
    x2 = x + jnp.dot(o.astype(BF16), wout_ref[...], preferred_element_type=F32)
    y_ref[...] = x2 * lax.rsqrt(jnp.mean(x2 * x2, axis=-1, keepdims=True) + NORM_EPS) * nf_ref[...]


def _gmlp(x2d, g, win_bf16, vg, vb, wmix, bias, wout_bf16, nf, *, tm, emit_v):
    n, d = x2d.shape
    e = vg.shape[0]
    const2 = lambda shape: pl.BlockSpec(shape, lambda i: (0, 0))
    out_shape = [jax.ShapeDtypeStruct((n, d), F32)]
    out_specs = [pl.BlockSpec((tm, d), lambda i: (i, 0))]
    if emit_v:
        out_shape.append(jax.ShapeDtypeStruct((n, e), F32))
        out_specs.append(pl.BlockSpec((tm, e), lambda i: (i, 0)))
    outs = pl.pallas_call(
        functools.partial(_gmlp_kernel, emit_v=emit_v),
        grid=(n // tm,),
        in_specs=[pl.BlockSpec((tm, d), lambda i: (i, 0)),
                  const2((1, d)),
                  pl.BlockSpec((d, 3 * e), lambda i: (0, 0), pipeline_mode=pl.Buffered(1)),
                  const2((1, e)), const2((1, e)),
                  pl.BlockSpec((GM_GROUPS, GM_CHUNK, GM_CHUNK), lambda i: (0, 0, 0)),
                  pl.BlockSpec((GM_GROUPS, GM_CHUNK, 1), lambda i: (0, 0, 0)),
                  pl.BlockSpec((e, d), lambda i: (0, 0), pipeline_mode=pl.Buffered(1)),
                  const2((1, d))],
        out_specs=out_specs,
        out_shape=out_shape,
        compiler_params=pltpu.CompilerParams(dimension_semantics=("arbitrary",), vmem_limit_bytes=VMEM_LIMIT),
        name="gmlp",
    )(x2d, g.reshape(1, d), win_bf16, vg.reshape(1, e), vb.reshape(1, e), wmix,
      bias.reshape(GM_GROUPS, GM_CHUNK, 1), wout_bf16, nf.reshape(1, d))
    return outs


def _row_tile(n, pref):
    t = pref
    while n % t:
        t //= 2
    return t


def kernel(x_prompt, x_sample, state_shift, state_wkv, norm_g, norm_f, rw_in, rw_mu, rw_w0, rw_w2, rw_a0, rw_a2, rw_kk, rw_ka, rw_rk, rw_lnx_g, rw_lnx_b, rw_out, gm_in, gm_vg, gm_vb, gm_ws, gm_bs, gm_out):
    bp, tp, d = x_prompt.shape
    bs_, ts, _ = x_sample.shape
    e = rw_w0.shape[1]
    nh = e // HEAD
    shift = 3 * e + 2 * LORA
    xp = x_prompt.reshape(bp * tp, d)
    xs = x_sample.reshape(bs_ * ts, d)

    zeros_l = jnp.zeros((LORA, e), F32)
    prm = dict(
        mu=rw_mu[0].reshape(1, shift), w0=rw_w0[0].reshape(1, e), a0=rw_a0[0].reshape(1, e),
        w2p=jnp.concatenate([rw_w2[0], zeros_l], axis=0).astype(BF16),
        a2p=jnp.concatenate([zeros_l, rw_a2[0]], axis=0).astype(BF16),
        kk=rw_kk[0].reshape(1, e), ka=rw_ka[0].reshape(1, e), rk=rw_rk[0].reshape(1, e),
        lng=rw_lnx_g[0].reshape(1, e), lnb=rw_lnx_b[0].reshape(1, e))
    w_in = rw_in[0].astype(BF16)
    w_out = rw_out[0].astype(BF16)

    proj_p = _norm_proj(xp, norm_g[0], w_in, _row_tile(bp * tp, 256)).reshape(bp, tp, shift + e)
    xs_t = jnp.transpose(x_sample, (1, 0, 2)).reshape(ts * bs_, d)
    proj_s = _norm_proj(xs_t, norm_g[0], w_in, _row_tile(bs_ * ts, 256)).reshape(ts, bs_, shift + e)

    chunk_p = 64
    oe_p, oo_p, shift_p, wkv_p = _wkv_pipe(proj_p, jnp.zeros((bp, shift), F32),
                                           jnp.zeros((bp, nh, HEAD, HEAD), F32), prm,
                                           chunk=chunk_p, n_levels=5, passes=1)
    o_s, wkv_s_t = _wkv_seq(proj_s, state_shift[0], jnp.transpose(state_wkv[0], (1, 2, 3, 0)), prm)
    wkv_s = jnp.transpose(wkv_s_t, (3, 0, 1, 2))
    shift_s = proj_s[ts - 1, :, :shift]

    npair = tp // (2 * chunk_p)
    x1p = _out_residual_pairs(oe_p, oo_p, w_out, xp, _row_tile(npair, 4))
    x1s_t = _out_residual(o_s.reshape(ts * bs_, e), w_out, xs_t, _row_tile(bs_ * ts, 512))
    x1s = jnp.transpose(x1s_t.reshape(ts, bs_, d), (1, 0, 2)).reshape(bs_ * ts, d)

    g_in = gm_in[0].astype(BF16)
    g_out = gm_out[0].astype(BF16)
    wm_p = gm_ws[0]
    reps = GM_CHUNK // ts
    eye = jnp.eye(reps, dtype=F32)
    wm_s = jax.vmap(lambda w: jnp.kron(eye, w))(wm_p[:, :ts, :ts])
    bias_s = jnp.tile(gm_bs[0][:, :ts], (1, reps))

    (y_p,) = _gmlp(x1p, norm_g[1], g_in, gm_vg[0], gm_vb[0], wm_p, gm_bs[0], g_out, norm_f,
                   tm=_row_tile(bp * tp, 256), emit_v=False)
    y_s, v_s = _gmlp(x1s, norm_g[1], g_in, gm_vg[0], gm_vb[0], wm_s, bias_s, g_out, norm_f,
                     tm=_row_tile(bs_ * ts, 256), emit_v=True)

    return (y_p.reshape(bp, tp, d), y_s.reshape(bs_, ts, d),
            shift_p[None], wkv_p[None], shift_s[None], wkv_s[None],
            v_s.reshape(1, bs_, ts, e))
```

```python
import functools
import math

import jax
import jax.numpy as jnp
from jax import lax
from jax.experimental import pallas as pl
from jax.experimental.pallas import tpu as pltpu

F32 = jnp.float32
BF16 = jnp.bfloat16

HEAD = 64
QUAD = 4
QW = QUAD * HEAD
HPAIR = 2 * HEAD
LORA = 64
NORM_EPS = 1e-6
LN_EPS = 1e-5
GN_EPS = 64e-5
GM_CHUNK = 128
GM_GROUPS = 8
WKV_CHUNK = 64
VMEM_LIMIT = 56 * 1024 * 1024


def _split2(x):
    hi = x.astype(BF16)
    lo = (x - hi.astype(F32)).astype(BF16)
    return hi, lo


_NN = (((1,), (0,)), ((), ()))
_NT = (((1,), (1,)), ((), ()))
_TN = (((0,), (0,)), ((), ()))


def _mm(a, b, dims=_NN):
    return lax.dot_general(a.astype(BF16), b.astype(BF16), dims, preferred_element_type=F32)


def _mm_sum_rhs(a, b_bf16, dims=_NN):
    d = functools.partial(lax.dot_general, dimension_numbers=dims, preferred_element_type=F32)
    h, l = _split2(a)
    return d(h, b_bf16) + d(l, b_bf16)


def _mm_sum_lhs(a_bf16, b, dims=_NN):
    d = functools.partial(lax.dot_general, dimension_numbers=dims, preferred_element_type=F32)
    h, l = _split2(b)
    return d(a_bf16, h) + d(a_bf16, l)


def _iota(shape, dim):
    return lax.broadcasted_iota(jnp.int32, shape, dim)


def _sigmoid(x):
    return 0.5 * jnp.tanh(0.5 * x) + 0.5


def _rms_rows(x, g):
    return x * lax.rsqrt(jnp.mean(x * x, axis=-1, keepdims=True) + NORM_EPS) * g


def _head_ones(width):
    return ((_iota((width, width), 0) // HEAD) == (_iota((width, width), 1) // HEAD)).astype(BF16)


def _norm_proj_kernel(x_ref, g_ref, w_ref, o_ref):
    h = _rms_rows(x_ref[...], g_ref[...]).astype(BF16)
    o_ref[...] = jnp.dot(h, w_ref[...], preferred_element_type=F32)


def _norm_proj(x2d, g, w_bf16, tm):
    n, d = x2d.shape
    cols = w_bf16.shape[1]
    return pl.pallas_call(
        _norm_proj_kernel,
        grid=(n // tm,),
        in_specs=[pl.BlockSpec((tm, d), lambda i: (i, 0)),
                  pl.BlockSpec((1, d), lambda i: (0, 0)),
                  pl.BlockSpec((d, cols), lambda i: (0, 0), pipeline_mode=pl.Buffered(1))],
        out_specs=pl.BlockSpec((tm, cols), lambda i: (i, 0)),
        out_shape=jax.ShapeDtypeStruct((n, cols), F32),
        compiler_params=pltpu.CompilerParams(dimension_semantics=("arbitrary",), vmem_limit_bytes=VMEM_LIMIT),
        name="norm_proj",
    )(x2d, g.reshape(1, d), w_bf16)


_OPS_MM = ("at", "rt", "bt", "kt", "v", "bg", "kg")
_OPS_OUT = ("bonus", "gate")


def _proj_prep_kernel(x_ref, g_ref, w_ref, sp_ref, mu_ref, w0_ref, w2_ref, a0_ref, a2_ref, kk_ref, ka_ref, rk_ref,
                      at_ref, rt_ref, bt_ref, kt_ref, v_ref, bg_ref, kg_ref, bonus_ref, gate_ref, gt_ref, sho_ref,
                      carry_ref, *, chunk):
    j = pl.program_id(1)
    tm = x_ref.shape[0]
    E = w0_ref.shape[1]
    SHIFT = 3 * E + 2 * LORA
    C = chunk
    n_ch = tm // C
    outs = dict(zip(_OPS_MM + _OPS_OUT,
                    (at_ref, rt_ref, bt_ref, kt_ref, v_ref, bg_ref, kg_ref, bonus_ref, gate_ref)))

    @pl.when(j == 0)
    def _init():
        carry_ref[...] = sp_ref[0]

    h = _rms_rows(x_ref[...], g_ref[...]).astype(BF16)
    first_row = _iota((tm, 1), 0) == 0
    ones_bd = _head_ones(QW)
    row = _iota((tm, tm), 0)
    col = _iota((tm, tm), 1)
    tri = ((row >= col) & (row // C == col // C)).astype(BF16)

    def proj(lo, hi):
        return jnp.dot(h, w_ref[:, lo:hi], preferred_element_type=F32)

    def shifted(lo, hi):
        sh = proj(lo, hi)
        prev = jnp.where(first_row, carry_ref[:, lo:hi], pltpu.roll(sh, 1, 0))
        carry_ref[:, lo:hi] = sh[tm - 1:tm]
        return sh + (prev - sh) * mu_ref[:, lo:hi]

    def per_chunk_last(x):
        return jnp.concatenate(
            [jnp.broadcast_to(x[c * C + C - 1:c * C + C], (C, x.shape[1])) for c in range(n_ch)], axis=0)

    x_lora = shifted(3 * E, SHIFT)
    lane_l = _iota((tm, 2 * LORA), 1)
    t_lora = jnp.where(lane_l < LORA, jnp.tanh(x_lora), x_lora)
    wf = w0_ref[...] + _mm(t_lora, w2_ref[...])
    af = a0_ref[...] + _mm(x_lora, a2_ref[...])

    for q in range(E // QW):
        lo = q * QW
        sl = slice(lo, lo + QW)
        r = shifted(lo, lo + QW)
        k = shifted(E + lo, E + lo + QW)
        v = shifted(2 * E + lo, 2 * E + lo + QW)
        z = proj(SHIFT + lo, SHIFT + lo + QW)
        ld = -math.exp(-0.5) * _sigmoid(wf[:, sl])
        a = _sigmoid(af[:, sl])
        kk0 = k * kk_ref[:, sl]
        k2 = k * (1.0 + (a - 1.0) * ka_ref[:, sl])
        sums = _mm_sum_rhs(jnp.concatenate([kk0 * kk0, r * k2 * rk_ref[:, sl]], axis=0), ones_bd)
        kk = kk0 * jnp.minimum(lax.rsqrt(sums[:tm]), 1e12)
        cum = _mm_sum_lhs(tri, ld)
        cum_last = per_chunk_last(cum)
        g_rem = jnp.exp(cum_last - cum)
        g_inv = jnp.exp(-cum)
        kka = kk * a
        quad = dict(at=-kk * jnp.exp(cum - ld), rt=r * jnp.exp(cum), bt=kka * g_inv, kt=k2 * g_inv, v=v,
                    bg=kka * g_rem, kg=k2 * g_rem, bonus=sums[tm:] * v, gate=z * _sigmoid(z))
        for n, x in quad.items():
            outs[n][:, sl] = x.astype(outs[n].dtype)
        for c in range(n_ch):
            gt_ref[c, :, sl] = jnp.exp(cum[c * C + C - 1:c * C + C])

    @pl.when(j == pl.num_programs(1) - 1)
    def _shift_out():
        sho_ref[0] = carry_ref[...]


def _proj_prep(x3d, g, w_bf16, shift_prev, prm, *, chunk, tm):
    b, t, d = x3d.shape
    e = prm["w0"].shape[1]
    shift = 3 * e + 2 * LORA
    cols = w_bf16.shape[1]
    steps = t // tm
    n_ch = tm // chunk
    c2 = lambda shape: pl.BlockSpec(shape, lambda i, j: (0, 0))
    rows = pl.BlockSpec((tm, e), lambda i, j: (i * steps + j, 0))
    outs = pl.pallas_call(
        functools.partial(_proj_prep_kernel, chunk=chunk),
        grid=(b, steps),
        in_specs=[pl.BlockSpec((tm, d), lambda i, j: (i * steps + j, 0)),
                  c2((1, d)),
                  pl.BlockSpec((d, cols), lambda i, j: (0, 0), pipeline_mode=pl.Buffered(1)),
                  pl.BlockSpec((1, 1, shift), lambda i, j: (i, 0, 0)),
                  c2((1, shift)), c2((1, e)), c2((2 * LORA, e)), c2((1, e)), c2((2 * LORA, e)),
                  c2((1, e)), c2((1, e)), c2((1, e))],
        out_specs=([rows] * (len(_OPS_MM) + len(_OPS_OUT))
                   + [pl.BlockSpec((n_ch, 1, e), lambda i, j: (i * steps + j, 0, 0)),
                      pl.BlockSpec((1, 1, shift), lambda i, j: (i, 0, 0))]),
        out_shape=([jax.ShapeDtypeStruct((b * t, e), BF16)] * len(_OPS_MM)
                   + [jax.ShapeDtypeStruct((b * t, e), F32)] * len(_OPS_OUT)
                   + [jax.ShapeDtypeStruct((b * t // chunk, 1, e), F32),
                      jax.ShapeDtypeStruct((b, 1, shift), F32)]),
        scratch_shapes=[pltpu.VMEM((1, shift), F32)],
        compiler_params=pltpu.CompilerParams(dimension_semantics=("arbitrary", "arbitrary"),
                                             vmem_limit_bytes=VMEM_LIMIT),
        name="proj_prep",
    )(x3d.reshape(b * t, d), g.reshape(1, d), w_bf16, shift_prev.reshape(b, 1, shift),
      prm["mu"], prm["w0"], prm["w2p"], prm["a0"], prm["a2p"], prm["kk"], prm["ka"], prm["rk"])
    ops = dict(zip(_OPS_MM + _OPS_OUT + ("g_tot",), outs[:-1]))
    return ops, outs[-1].reshape(b, shift)


def _headsum(xs, ones_bd):
    e = xs[0].shape[1]
    nq = e // QW
    rows = [x[:, q * QW:(q + 1) * QW] for x in xs for q in range(nq)]
    stacked = jnp.concatenate(rows, axis=0)
    s = _mm_sum_rhs(stacked, ones_bd)
    c = xs[0].shape[0]
    outs = []
    for i in range(len(xs)):
        outs.append(jnp.concatenate([s[(i * nq + q) * c:(i * nq + q + 1) * c] for q in range(nq)], axis=1))
    return outs


def _wkv_core(ops, states, lng_ref, lnb_ref, *, n_levels):
    C, E = ops["at"].shape
    TL = QUAD * C
    NQ = E // QW
    mask_bd = (_iota((QW, QW), 0) // HEAD) == (_iota((QW, QW), 1) // HEAD)
    ones_bd = mask_bd.astype(BF16)
    tok_t = _iota((C, TL), 0)
    tok_j = _iota((C, TL), 1) % C
    strict = tok_j < tok_t
    incl = tok_j <= tok_t
    eye_all = (tok_j == tok_t).astype(F32)

    def block_diag(x, width):
        assert 2 * width == 128
        x = x.astype(BF16)
        zero = jnp.zeros((), BF16)
        half = _iota((C, 128), 1) < width
        pad = jnp.zeros((C, 128), BF16)
        rows = []
        for h in range(QUAD):
            t = h // 2
            tile = jnp.where(half if h % 2 == 0 else ~half, x[:, t * 128:(t + 1) * 128], zero)
            rows.append(jnp.concatenate([tile if i == t else pad for i in range(QUAD // 2)], axis=1))
        return jnp.concatenate(rows, axis=0)

    stack_hc = functools.partial(block_diag, width=HEAD)
    stack_tt = functools.partial(block_diag, width=C)

    at, rt, bt, kt, v, bg, kg = (ops[n] for n in _OPS_MM)
    quads = range(NQ)
    sls = [slice(q * QW, (q + 1) * QW) for q in quads]
    ars = [jnp.concatenate([at[:, sl], rt[:, sl]], axis=0) for sl in sls]
    v_sts = [stack_hc(v[:, sl]) for sl in sls]
    abs_ = [_mm(ars[q], stack_hc(bt[:, sls[q]]), _NT) for q in quads]
    aks = [_mm(ars[q], stack_hc(kt[:, sls[q]]), _NT) for q in quads]
    pws = [jnp.where(strict, abs_[q][:C], 0.0) for q in quads]
    a_rbs = [jnp.where(incl, abs_[q][C:], 0.0) for q in quads]
    a_aks = [jnp.where(strict, aks[q][:C], 0.0) for q in quads]
    a_rks = [jnp.where(incl, aks[q][C:], 0.0) for q in quads]
    invs = [eye_all + pws[q] for q in quads]
    pws = [_mm(pws[q], stack_tt(pws[q])) for q in quads]
    for _ in range(n_levels - 1):
        bds = [stack_tt(pws[q]) for q in quads]
        res = [_mm(jnp.concatenate([pws[q], invs[q]], axis=0), bds[q]) for q in quads]
        pws = [res[q][:C] for q in quads]
        invs = [invs[q] + res[q][C:] for q in quads]
    invs = [invs[q] + _mm(invs[q], stack_tt(pws[q])) for q in quads]
    arss = [_mm(ars[q], states[q], _NT) for q in quads]
    avs = [_mm(jnp.concatenate([a_aks[q], a_rks[q]], axis=0), v_sts[q]) for q in quads]
    ws = [arss[q][:C] + avs[q][:C] for q in quads]
    us = [_mm(invs[q], stack_hc(ws[q])) for q in quads]
    ys = [arss[q][C:] + avs[q][C:] + _mm(a_rbs[q], stack_hc(us[q])) for q in quads]
    new_states = []
    for q in quads:
        uv = jnp.concatenate([us[q].astype(BF16), v[:, sls[q]]], axis=0)
        bk = jnp.concatenate([bg[:, sls[q]], kg[:, sls[q]]], axis=0)
        new_states.append(states[q] * ops["g_tot"][:, sls[q]] + jnp.where(mask_bd, _mm(uv, bk, _TN), 0.0))
    y = jnp.concatenate(ys, axis=1)

    (ysum,) = _headsum([y], ones_bd)
    yc = y - ysum * (1.0 / HEAD)
    (vsum,) = _headsum([yc * yc], ones_bd)
    yn = yc * lax.rsqrt(vsum * (1.0 / HEAD) + GN_EPS) * lng_ref[...] + lnb_ref[...]
    return (yn + ops["bonus"]) * ops["gate"], new_states


def _wkv_chunks_kernel(at_ref, rt_ref, bt_ref, kt_ref, v_ref, bg_ref, kg_ref, bonus_ref, gate_ref, gt_ref, s0_ref,
                       lng_ref, lnb_ref, o_ref, so_ref, st_ref, *, chunk, n_levels):
    j = pl.program_id(1)
    NQ = st_ref.shape[0]
    refs = dict(zip(_OPS_MM + _OPS_OUT,
                    (at_ref, rt_ref, bt_ref, kt_ref, v_ref, bg_ref, kg_ref, bonus_ref, gate_ref)))

    @pl.when(j == 0)
    def _init():
        zero_blk = jnp.zeros((HEAD, HEAD), F32)
        for q in range(NQ):
            st_ref[q] = jnp.concatenate(
                [jnp.concatenate([s0_ref[0, QUAD * q + h] if i == h else zero_blk for i in range(QUAD)], axis=1)
                 for h in range(QUAD)], axis=0)

    states = [st_ref[q] for q in range(NQ)]
    for c in range(at_ref.shape[0] // chunk):
        rows = slice(c * chunk, (c + 1) * chunk)
        ops = {n: ref[rows, :] for n, ref in refs.items()}
        ops["g_tot"] = gt_ref[c]
        o, states = _wkv_core(ops, states, lng_ref, lnb_ref, n_levels=n_levels)
        o_ref[rows, :] = o.astype(o_ref.dtype)
    for q in range(NQ):
        st_ref[q] = states[q]

    @pl.when(j == pl.num_programs(1) - 1)
    def _state_out():
        for q in range(NQ):
            for h in range(QUAD):
                so_ref[0, QUAD * q + h] = states[q][h * HEAD:(h + 1) * HEAD, h * HEAD:(h + 1) * HEAD]


def _wkv_chunks(ops, s0, lng, lnb, *, seq_len, chunk, chunks_per_step):
    n, e = ops["at"].shape
    b = n // seq_len
    nh = e // HEAD
    tm = chunk * chunks_per_step
    steps = seq_len // tm
    rows = pl.BlockSpec((tm, e), lambda i, j: (i * steps + j, 0))
    c2 = lambda shape: pl.BlockSpec(shape, lambda i, j: (0, 0))
    sblk = pl.BlockSpec((1, nh, HEAD, HEAD), lambda i, j: (i, 0, 0, 0))
    n_levels = max(1, math.ceil(math.log2(chunk)) - 1)
    return pl.pallas_call(
        functools.partial(_wkv_chunks_kernel, chunk=chunk, n_levels=n_levels),
        grid=(b, steps),
        in_specs=([rows] * (len(_OPS_MM) + len(_OPS_OUT))
                  + [pl.BlockSpec((chunks_per_step, 1, e), lambda i, j: (i * steps + j, 0, 0)),
                     sblk, c2((1, e)), c2((1, e))]),
        out_specs=[rows, sblk],
        out_shape=[jax.ShapeDtypeStruct((n, e), BF16), jax.ShapeDtypeStruct((b, nh, HEAD, HEAD), F32)],
        scratch_shapes=[pltpu.VMEM((e // QW, QW, QW), F32)],
        compiler_params=pltpu.CompilerParams(dimension_semantics=("arbitrary", "arbitrary"),
                                             vmem_limit_bytes=VMEM_LIMIT),
        name="wkv_chunks",
    )(*(ops[n_] for n_ in _OPS_MM + _OPS_OUT), ops["g_tot"], s0, lng, lnb)


_VT = ("nkk", "w", "b", "k", "r", "v")


def _wkv_seq_kernel(pr_ref, pk_ref, pv_ref, pl_ref, pz_ref, sr_ref, sk_ref, sv_ref, sl_ref,
                    mur_ref, muk_ref, muv_ref, mul_ref, w0_ref, w2_ref, a0_ref, a2_ref, kk_ref, ka_ref, rk_ref,
                    lng_ref, lnb_ref, s_ref, o_ref, so_ref, vt_ref, yt_ref):
    T, B, _ = pr_ref.shape
    ones2 = _head_ones(HPAIR)

    def shifted(p_ref, prev_ref, mu_ref):
        sh = p_ref[...].reshape(T * B, HPAIR)
        prev = jnp.concatenate([prev_ref[...], sh[:(T - 1) * B]], axis=0)
        return sh + (prev - sh) * mu_ref[...]

    r = shifted(pr_ref, sr_ref, mur_ref)
    k = shifted(pk_ref, sk_ref, muk_ref)
    v = shifted(pv_ref, sv_ref, muv_ref)
    x_lora = shifted(pl_ref, sl_ref, mul_ref)
    z = pz_ref[...].reshape(T * B, HPAIR)
    lane_l = _iota((T * B, 2 * LORA), 1)
    t_lora = jnp.where(lane_l < LORA, jnp.tanh(x_lora), x_lora)
    wf = w0_ref[...] + _mm(t_lora, w2_ref[...])
    af = a0_ref[...] + _mm(x_lora, a2_ref[...])
    w = jnp.exp(-math.exp(-0.5) * _sigmoid(wf))
    a = _sigmoid(af)
    kk0 = k * kk_ref[...]
    k2 = k * (1.0 + (a - 1.0) * ka_ref[...])
    sums = _mm_sum_rhs(jnp.concatenate([kk0 * kk0, r * k2 * rk_ref[...]], axis=0), ones2)
    kk = kk0 * jnp.minimum(lax.rsqrt(sums[:T * B]), 1e12)
    bonus = sums[T * B:] * v
    vecs = dict(nkk=-kk, w=w, b=kk * a, k=k2, r=r, v=v)
    for i, n in enumerate(_VT):
        for t in range(T):
            vt_ref[i, t] = vecs[n][t * B:(t + 1) * B].T

    for h in range(2):
        ch = slice(h * HEAD, (h + 1) * HEAD)

        def body(g, carry, h=h, ch=ch):
            v0 = pl.multiple_of(g * 8, 8)
            vrows = [vt_ref[5, t, pl.ds(h * HEAD + v0, 8), :] for t in range(T)]
            ys = [[] for _ in range(T)]
            for u in range(8):
                s = s_ref[h, v0 + u]
                for t in range(T):
                    sa = jnp.sum(s * vt_ref[0, t, ch, :], axis=0, keepdims=True)
                    s = s * vt_ref[1, t, ch, :] + sa * vt_ref[2, t, ch, :] + vrows[t][u:u + 1] * vt_ref[3, t, ch, :]
                    ys[t].append(jnp.sum(s * vt_ref[4, t, ch, :], axis=0, keepdims=True))
                so_ref[h, v0 + u] = s
            for t in range(T):
                yt_ref[t, pl.ds(h * HEAD + v0, 8), :] = jnp.concatenate(ys[t], axis=0)
            return carry

        lax.fori_loop(0, HEAD // 8, body, 0)

    y = jnp.concatenate([yt_ref[t].T for t in range(T)], axis=0)
    yc = y - _mm_sum_rhs(y, ones2) * (1.0 / HEAD)
    var = _mm_sum_rhs(yc * yc, ones2) * (1.0 / HEAD)
    yn = yc * lax.rsqrt(var + GN_EPS) * lng_ref[...] + lnb_ref[...]
    o_ref[...] = ((yn + bonus) * (z * _sigmoid(z))).reshape(T, B, HPAIR)


def _wkv_seq(proj_t, shift_prev, s_t, prm):
    t, b, cols = proj_t.shape
    e = prm["w0"].shape[1]
    nh = e // HEAD
    kb = e // HPAIR
    bt = 128
    off = dict(r=0, k=kb, v=2 * kb, l=3 * kb)
    pspec = lambda o: pl.BlockSpec((t, bt, HPAIR), lambda i, j, o=o: (0, j, o + i))
    pfix = lambda o: pl.BlockSpec((t, bt, HPAIR), lambda i, j, o=o: (0, j, o))
    sspec = lambda o: pl.BlockSpec((bt, HPAIR), lambda i, j, o=o: (j, o + i))
    sfix = lambda o: pl.BlockSpec((bt, HPAIR), lambda i, j, o=o: (j, o))
    mspec = lambda o: pl.BlockSpec((1, HPAIR), lambda i, j, o=o: (0, o + i))
    mfix = lambda o: pl.BlockSpec((1, HPAIR), lambda i, j, o=o: (0, o))
    row = pl.BlockSpec((1, HPAIR), lambda i, j: (0, i))
    lora = pl.BlockSpec((2 * LORA, HPAIR), lambda i, j: (0, i))
    sblk = pl.BlockSpec((2, HEAD, HEAD, bt), lambda i, j: (i, 0, 0, j))
    return pl.pallas_call(
        _wkv_seq_kernel,
        grid=(nh // 2, b // bt),
        in_specs=[pspec(off["r"]), pspec(off["k"]), pspec(off["v"]), pfix(off["l"]), pspec(off["l"] + 1),
                  sspec(off["r"]), sspec(off["k"]), sspec(off["v"]), sfix(off["l"]),
                  mspec(off["r"]), mspec(off["k"]), mspec(off["v"]), mfix(off["l"]),
                  row, lora, row, lora, row, row, row, row, row, sblk],
        out_specs=[pl.BlockSpec((t, bt, HPAIR), lambda i, j: (0, j, i)), sblk],
        out_shape=[jax.ShapeDtypeStruct((t, b, e), F32), jax.ShapeDtypeStruct(s_t.shape, F32)],
        scratch_shapes=[pltpu.VMEM((len(_VT), t, HPAIR, bt), F32),
                        pltpu.VMEM((t, HPAIR, bt), F32)],
        compiler_params=pltpu.CompilerParams(dimension_semantics=("arbitrary", "arbitrary"),
                                             vmem_limit_bytes=VMEM_LIMIT),
        name="wkv_seq",
    )(proj_t, proj_t, proj_t, proj_t, proj_t, shift_prev, shift_prev, shift_prev, shift_prev,
      prm["mu"], prm["mu"], prm["mu"], prm["mu"], prm["w0"], prm["w2p"], prm["a0"], prm["a2p"],
      prm["kk"], prm["ka"], prm["rk"], prm["lng"], prm["lnb"], s_t)


def _out_residual_kernel(o_ref, w_ref, x_ref, y_ref):
    y_ref[...] = x_ref[...] + jnp.dot(o_ref[...].astype(BF16), w_ref[...], preferred_element_type=F32)


def _out_residual(o2d, w_bf16, x2d, tm):
    n, e = o2d.shape
    d = x2d.shape[1]
    return pl.pallas_call(
        _out_residual_kernel,
        grid=(n // tm,),
        in_specs=[pl.BlockSpec((tm, e), lambda i: (i, 0)),
                  pl.BlockSpec((e, d), lambda i: (0, 0)),
                  pl.BlockSpec((tm, d), lambda i: (i, 0))],
        out_specs=pl.BlockSpec((tm, d), lambda i: (i, 0)),
        out_shape=jax.ShapeDtypeStruct((n, d), F32),
        compiler_params=pltpu.CompilerParams(dimension_semantics=("arbitrary",), vmem_limit_bytes=VMEM_LIMIT),
        name="out_residual",
    )(o2d, w_bf16, x2d)


def _gmlp_kernel(x_ref, g_ref, win_ref, vg_ref, vb_ref, wm_ref, bs_ref, wout_ref, nf_ref, *out_refs, emit_v):
    y_ref = out_refs[0]
    tm = x_ref.shape[0]
    e = vg_ref.shape[1]
    gd = e // GM_GROUPS
    x = x_ref[...]
    h = _rms_rows(x, g_ref[...]).astype(BF16)
    proj = jnp.dot(h, win_ref[...], preferred_element_type=F32)
    u = jax.nn.gelu(proj[:, :e])
    vf = jax.nn.gelu(proj[:, e:2 * e])
    z = proj[:, 2 * e:]
    vm = jnp.mean(vf, axis=-1, keepdims=True)
    vc = vf - vm
    vv = jnp.mean(vc * vc, axis=-1, keepdims=True)
    vn = vc * lax.rsqrt(vv + LN_EPS) * vg_ref[...] + vb_ref[...]
    if emit_v:
        out_refs[1][...] = vn
    vn16 = vn.astype(BF16)
    causal = _iota((GM_CHUNK, GM_CHUNK), 0) >= _iota((GM_CHUNK, GM_CHUNK), 1)
    rows = []
    for j in range(tm // GM_CHUNK):
        cols = []
        for gi in range(GM_GROUPS):
            wm = jnp.where(causal, wm_ref[gi], 0.0).astype(BF16)
            blk = vn16[j * GM_CHUNK:(j + 1) * GM_CHUNK, gi * gd:(gi + 1) * gd]
            cols.append(jnp.dot(wm, blk, preferred_element_type=F32) + bs_ref[gi])
        rows.append(jnp.concatenate(cols, axis=1))
    mixed = jnp.concatenate(rows, axis=0)
    o = u * mixed * (z * _sigmoid(z))
    x2 = x + jnp.dot(o.astype(BF16), wout_ref[...], preferred_element_type=F32)
    y_ref[...] = _rms_rows(x2, nf_ref[...])


def _gmlp(x2d, g, win_bf16, vg, vb, wmix, bias, wout_bf16, nf, *, tm, emit_v):
    n, d = x2d.shape
    e = vg.shape[0]
    const2 = lambda shape: pl.BlockSpec(shape, lambda i: (0, 0))
    out_shape = [jax.ShapeDtypeStruct((n, d), F32)]
    out_specs = [pl.BlockSpec((tm, d), lambda i: (i, 0))]
    if emit_v:
        out_shape.append(jax.ShapeDtypeStruct((n, e), F32))
        out_specs.append(pl.BlockSpec((tm, e), lambda i: (i, 0)))
    outs = pl.pallas_call(
        functools.partial(_gmlp_kernel, emit_v=emit_v),
        grid=(n // tm,),
        in_specs=[pl.BlockSpec((tm, d), lambda i: (i, 0)),
                  const2((1, d)),
                  pl.BlockSpec((d, 3 * e), lambda i: (0, 0), pipeline_mode=pl.Buffered(1)),
                  const2((1, e)), const2((1, e)),
                  pl.BlockSpec((GM_GROUPS, GM_CHUNK, GM_CHUNK), lambda i: (0, 0, 0)),
                  pl.BlockSpec((GM_GROUPS, GM_CHUNK, 1), lambda i: (0, 0, 0)),
                  pl.BlockSpec((e, d), lambda i: (0, 0), pipeline_mode=pl.Buffered(1)),
                  const2((1, d))],
        out_specs=out_specs,
        out_shape=out_shape,
        compiler_params=pltpu.CompilerParams(dimension_semantics=("arbitrary",), vmem_limit_bytes=VMEM_LIMIT),
        name="gmlp",
    )(x2d, g.reshape(1, d), win_bf16, vg.reshape(1, e), vb.reshape(1, e), wmix,
      bias.reshape(GM_GROUPS, GM_CHUNK, 1), wout_bf16, nf.reshape(1, d))
    return outs


def _row_tile(n, pref):
    t = pref
    while n % t:
        t //= 2
    return t


def kernel(x_prompt, x_sample, state_shift, state_wkv, norm_g, norm_f, rw_in, rw_mu, rw_w0, rw_w2, rw_a0, rw_a2, rw_kk, rw_ka, rw_rk, rw_lnx_g, rw_lnx_b, rw_out, gm_in, gm_vg, gm_vb, gm_ws, gm_bs, gm_out):
    bp, tp, d = x_prompt.shape
    bs_, ts, _ = x_sample.shape
    e = rw_w0.shape[1]
    nh = e // HEAD
    shift = 3 * e + 2 * LORA
    xp = x_prompt.reshape(bp * tp, d)

    zeros_l = jnp.zeros((LORA, e), F32)
    prm = dict(
        mu=rw_mu[0].reshape(1, shift), w0=rw_w0[0].reshape(1, e), a0=rw_a0[0].reshape(1, e),
        w2p=jnp.concatenate([rw_w2[0], zeros_l], axis=0).astype(BF16),
        a2p=jnp.concatenate([zeros_l, rw_a2[0]], axis=0).astype(BF16),
        kk=rw_kk[0].reshape(1, e), ka=rw_ka[0].reshape(1, e), rk=rw_rk[0].reshape(1, e),
        lng=rw_lnx_g[0].reshape(1, e), lnb=rw_lnx_b[0].reshape(1, e))
    w_in = rw_in[0].astype(BF16)
    w_out = rw_out[0].astype(BF16)

    ops_p, shift_p = _proj_prep(x_prompt, norm_g[0], w_in, jnp.zeros((bp, shift), F32), prm,
                                chunk=WKV_CHUNK, tm=_row_tile(tp, 4 * WKV_CHUNK))
    o_p, wkv_p = _wkv_chunks(ops_p, jnp.zeros((bp, nh, HEAD, HEAD), F32), prm["lng"], prm["lnb"],
                             seq_len=tp, chunk=WKV_CHUNK, chunks_per_step=2)
    x1p = _out_residual(o_p, w_out, xp, _row_tile(bp * tp, 512))

    xs_t = jnp.transpose(x_sample, (1, 0, 2)).reshape(ts * bs_, d)
    proj_s = _norm_proj(xs_t, norm_g[0], w_in, _row_tile(bs_ * ts, 256)).reshape(ts, bs_, shift + e)
    o_s, wkv_s_t = _wkv_seq(proj_s, state_shift[0], jnp.transpose(state_wkv[0], (1, 2, 3, 0)), prm)
    wkv_s = jnp.transpose(wkv_s_t, (3, 0, 1, 2))
    shift_s = proj_s[ts - 1, :, :shift]
    x1s_t = _out_residual(o_s.reshape(ts * bs_, e), w_out, xs_t, _row_tile(bs_ * ts, 512))
    x1s = jnp.transpose(x1s_t.reshape(ts, bs_, d), (1, 0, 2)).reshape(bs_ * ts, d)

    g_in = gm_in[0].astype(BF16)
    g_out = gm_out[0].astype(BF16)
    wm_p = gm_ws[0]
    reps = GM_CHUNK // ts
    eye = jnp.eye(reps, dtype=F32)
    wm_s = jax.vmap(lambda w: jnp.kron(eye, w))(wm_p[:, :ts, :ts])
    bias_s = jnp.tile(gm_bs[0][:, :ts], (1, reps))

    (y_p,) = _gmlp(x1p, norm_g[1], g_in, gm_vg[0], gm_vb[0], wm_p, gm_bs[0], g_out, norm_f,
                   tm=_row_tile(bp * tp, 256), emit_v=False)
    y_s, v_s = _gmlp(x1s, norm_g[1], g_in, gm_vg[0], gm_vb[0], wm_s, bias_s, g_out, norm_f,
                     tm=_row_tile(bs_ * ts, 256), emit_v=True)

    return (y_p.reshape(bp, tp, d), y_s.reshape(bs_, ts, d),
            shift_p[None], wkv_p[None], shift_s[None], wkv_s[None],
            v_s.reshape(1, bs_, ts, e))
```

```python
import functools
import math

import jax
import jax.numpy as jnp
from jax import lax
from jax.experimental import pallas as pl
from jax.experimental.pallas import tpu as pltpu

F32 = jnp.float32
BF16 = jnp.bfloat16

HEAD = 64
QUAD = 4
QW = QUAD * HEAD
HPAIR = 2 * HEAD
LORA = 64
NORM_EPS = 1e-6
LN_EPS = 1e-5
GN_EPS = 64e-5
GM_CHUNK = 128
GM_GROUPS = 8
WKV_CHUNK = 64
VMEM_LIMIT = 56 * 1024 * 1024


def _split2(x):
    hi = x.astype(BF16)
    lo = (x - hi.astype(F32)).astype(BF16)
    return hi, lo


_NN = (((1,), (0,)), ((), ()))
_NT = (((1,), (1,)), ((), ()))
_TN = (((0,), (0,)), ((), ()))


def _mm(a, b, dims=_NN):
    return lax.dot_general(a.astype(BF16), b.astype(BF16), dims, preferred_element_type=F32)


def _mm_sum_rhs(a, b_bf16, dims=_NN):
    d = functools.partial(lax.dot_general, dimension_numbers=dims, preferred_element_type=F32)
    h, l = _split2(a)
    return d(h, b_bf16) + d(l, b_bf16)


def _mm_sum_lhs(a_bf16, b, dims=_NN):
    d = functools.partial(lax.dot_general, dimension_numbers=dims, preferred_element_type=F32)
    h, l = _split2(b)
    return d(a_bf16, h) + d(a_bf16, l)


def _iota(shape, dim):
    return lax.broadcasted_iota(jnp.int32, shape, dim)


def _sigmoid(x):
    return 0.5 * jnp.tanh(0.5 * x) + 0.5


def _rms_rows(x, g):
    return x * lax.rsqrt(jnp.mean(x * x, axis=-1, keepdims=True) + NORM_EPS) * g


def _head_ones(width):
    return ((_iota((width, width), 0) // HEAD) == (_iota((width, width), 1) // HEAD)).astype(BF16)


def _norm_proj_kernel(x_ref, g_ref, w_ref, o_ref):
    h = _rms_rows(x_ref[...], g_ref[...]).astype(BF16)
    o_ref[...] = jnp.dot(h, w_ref[...], preferred_element_type=F32)


def _norm_proj(x2d, g, w_bf16, tm):
    n, d = x2d.shape
    cols = w_bf16.shape[1]
    return pl.pallas_call(
        _norm_proj_kernel,
        grid=(n // tm,),
        in_specs=[pl.BlockSpec((tm, d), lambda i: (i, 0)),
                  pl.BlockSpec((1, d), lambda i: (0, 0)),
                  pl.BlockSpec((d, cols), lambda i: (0, 0), pipeline_mode=pl.Buffered(1))],
        out_specs=pl.BlockSpec((tm, cols), lambda i: (i, 0)),
        out_shape=jax.ShapeDtypeStruct((n, cols), F32),
        compiler_params=pltpu.CompilerParams(dimension_semantics=("arbitrary",), vmem_limit_bytes=VMEM_LIMIT),
        name="norm_proj",
    )(x2d, g.reshape(1, d), w_bf16)


_OPS_MM = ("at", "rt", "bt", "kt", "v", "bg", "kg")
_OPS_OUT = ("bonus", "gate")


def _proj_prep_kernel(x_ref, g_ref, w_ref, sp_ref, mu_ref, w0_ref, w2_ref, a0_ref, a2_ref, kk_ref, ka_ref, rk_ref,
                      at_ref, rt_ref, bt_ref, kt_ref, v_ref, bg_ref, kg_ref, bonus_ref, gate_ref, gt_ref, sho_ref,
                      carry_ref, *, chunk):
    j = pl.program_id(1)
    tm = x_ref.shape[0]
    E = w0_ref.shape[1]
    SHIFT = 3 * E + 2 * LORA
    C = chunk
    n_ch = tm // C
    outs = dict(zip(_OPS_MM + _OPS_OUT,
                    (at_ref, rt_ref, bt_ref, kt_ref, v_ref, bg_ref, kg_ref, bonus_ref, gate_ref)))

    @pl.when(j == 0)
    def _init():
        carry_ref[...] = sp_ref[0]

    h = _rms_rows(x_ref[...], g_ref[...]).astype(BF16)
    first_row = _iota((tm, 1), 0) == 0
    ones_bd = _head_ones(QW)
    row = _iota((tm, tm), 0)
    col = _iota((tm, tm), 1)
    tri = ((row >= col) & (row // C == col // C)).astype(BF16)

    def proj(lo, hi):
        return jnp.dot(h, w_ref[:, lo:hi], preferred_element_type=F32)

    def shifted(lo, hi):
        sh = proj(lo, hi)
        prev = jnp.where(first_row, carry_ref[:, lo:hi], pltpu.roll(sh, 1, 0))
        carry_ref[:, lo:hi] = sh[tm - 1:tm]
        return sh + (prev - sh) * mu_ref[:, lo:hi]

    def per_chunk_last(x):
        return jnp.concatenate(
            [jnp.broadcast_to(x[c * C + C - 1:c * C + C], (C, x.shape[1])) for c in range(n_ch)], axis=0)

    x_lora = shifted(3 * E, SHIFT)
    lane_l = _iota((tm, 2 * LORA), 1)
    t_lora = jnp.where(lane_l < LORA, jnp.tanh(x_lora), x_lora)
    wf = w0_ref[...] + _mm(t_lora, w2_ref[...])
    af = a0_ref[...] + _mm(x_lora, a2_ref[...])

    for q in range(E // QW):
        lo = q * QW
        sl = slice(lo, lo + QW)
        r = shifted(lo, lo + QW)
        k = shifted(E + lo, E + lo + QW)
        v = shifted(2 * E + lo, 2 * E + lo + QW)
        z = proj(SHIFT + lo, SHIFT + lo + QW)
        ld = -math.exp(-0.5) * _sigmoid(wf[:, sl])
        a = _sigmoid(af[:, sl])
        kk0 = k * kk_ref[:, sl]
        k2 = k * (1.0 + (a - 1.0) * ka_ref[:, sl])
        sums = _mm(jnp.concatenate([kk0 * kk0, r * k2 * rk_ref[:, sl]], axis=0), ones_bd)
        kk = kk0 * jnp.minimum(lax.rsqrt(sums[:tm]), 1e12)
        cum = _mm_sum_lhs(tri, ld)
        cum_last = per_chunk_last(cum)
        g_rem = jnp.exp(cum_last - cum)
        g_inv = jnp.exp(-cum)
        kka = kk * a
        quad = dict(at=-kk * jnp.exp(cum - ld), rt=r * jnp.exp(cum), bt=kka * g_inv, kt=k2 * g_inv, v=v,
                    bg=kka * g_rem, kg=k2 * g_rem, bonus=sums[tm:] * v, gate=z * _sigmoid(z))
        for n, x in quad.items():
            outs[n][:, sl] = x.astype(outs[n].dtype)
        for c in range(n_ch):
            gt_ref[c, :, sl] = jnp.exp(cum[c * C + C - 1:c * C + C])

    @pl.when(j == pl.num_programs(1) - 1)
    def _shift_out():
        sho_ref[0] = carry_ref[...]


def _proj_prep(x3d, g, w_bf16, shift_prev, prm, *, chunk, tm):
    b, t, d = x3d.shape
    e = prm["w0"].shape[1]
    shift = 3 * e + 2 * LORA
    cols = w_bf16.shape[1]
    steps = t // tm
    n_ch = tm // chunk
    c2 = lambda shape: pl.BlockSpec(shape, lambda i, j: (0, 0))
    rows = pl.BlockSpec((tm, e), lambda i, j: (i * steps + j, 0))
    outs = pl.pallas_call(
        functools.partial(_proj_prep_kernel, chunk=chunk),
        grid=(b, steps),
        in_specs=[pl.BlockSpec((tm, d), lambda i, j: (i * steps + j, 0)),
                  c2((1, d)),
                  pl.BlockSpec((d, cols), lambda i, j: (0, 0), pipeline_mode=pl.Buffered(1)),
                  pl.BlockSpec((1, 1, shift), lambda i, j: (i, 0, 0)),
                  c2((1, shift)), c2((1, e)), c2((2 * LORA, e)), c2((1, e)), c2((2 * LORA, e)),
                  c2((1, e)), c2((1, e)), c2((1, e))],
        out_specs=([rows] * (len(_OPS_MM) + len(_OPS_OUT))
                   + [pl.BlockSpec((n_ch, 1, e), lambda i, j: (i * steps + j, 0, 0)),
                      pl.BlockSpec((1, 1, shift), lambda i, j: (i, 0, 0))]),
        out_shape=([jax.ShapeDtypeStruct((b * t, e), BF16)] * len(_OPS_MM)
                   + [jax.ShapeDtypeStruct((b * t, e), F32)] * len(_OPS_OUT)
                   + [jax.ShapeDtypeStruct((b * t // chunk, 1, e), F32),
                      jax.ShapeDtypeStruct((b, 1, shift), F32)]),
        scratch_shapes=[pltpu.VMEM((1, shift), F32)],
        compiler_params=pltpu.CompilerParams(dimension_semantics=("arbitrary", "arbitrary"),
                                             vmem_limit_bytes=VMEM_LIMIT),
        name="proj_prep",
    )(x3d.reshape(b * t, d), g.reshape(1, d), w_bf16, shift_prev.reshape(b, 1, shift),
      prm["mu"], prm["w0"], prm["w2p"], prm["a0"], prm["a2p"], prm["kk"], prm["ka"], prm["rk"])
    ops = dict(zip(_OPS_MM + _OPS_OUT + ("g_tot",), outs[:-1]))
    return ops, outs[-1].reshape(b, shift)


def _headsum(xs, ones_bd, split):
    e = xs[0].shape[1]
    nq = e // QW
    rows = [x[:, q * QW:(q + 1) * QW] for x in xs for q in range(nq)]
    stacked = jnp.concatenate(rows, axis=0)
    s = _mm_sum_rhs(stacked, ones_bd) if split else _mm(stacked, ones_bd)
    c = xs[0].shape[0]
    outs = []
    for i in range(len(xs)):
        outs.append(jnp.concatenate([s[(i * nq + q) * c:(i * nq + q + 1) * c] for q in range(nq)], axis=1))
    return outs


def _wkv_core(ops, states, lng_ref, lnb_ref, *, n_levels):
    C, E = ops["at"].shape
    TL = QUAD * C
    NQ = E // QW
    mask_bd = (_iota((QW, QW), 0) // HEAD) == (_iota((QW, QW), 1) // HEAD)
    ones_bd = mask_bd.astype(BF16)
    tok_t = _iota((C, TL), 0)
    tok_j = _iota((C, TL), 1) % C
    strict = tok_j < tok_t
    incl = tok_j <= tok_t
    eye_all = (tok_j == tok_t).astype(F32)

    def block_diag(x, width):
        assert 2 * width == 128
        x = x.astype(BF16)
        zero = jnp.zeros((), BF16)
        half = _iota((C, 128), 1) < width
        pad = jnp.zeros((C, 128), BF16)
        rows = []
        for h in range(QUAD):
            t = h // 2
            tile = jnp.where(half if h % 2 == 0 else ~half, x[:, t * 128:(t + 1) * 128], zero)
            rows.append(jnp.concatenate([tile if i == t else pad for i in range(QUAD // 2)], axis=1))
        return jnp.concatenate(rows, axis=0)

    stack_hc = functools.partial(block_diag, width=HEAD)
    stack_tt = functools.partial(block_diag, width=C)

    at, rt, bt, kt, v, bg, kg = (ops[n] for n in _OPS_MM)
    quads = range(NQ)
    sls = [slice(q * QW, (q + 1) * QW) for q in quads]
    ars = [jnp.concatenate([at[:, sl], rt[:, sl]], axis=0) for sl in sls]
    v_sts = [stack_hc(v[:, sl]) for sl in sls]
    abs_ = [_mm(ars[q], stack_hc(bt[:, sls[q]]), _NT) for q in quads]
    aks = [_mm(ars[q], stack_hc(kt[:, sls[q]]), _NT) for q in quads]
    pws = [jnp.where(strict, abs_[q][:C], 0.0) for q in quads]
    a_rbs = [jnp.where(incl, abs_[q][C:], 0.0) for q in quads]
    a_aks = [jnp.where(strict, aks[q][:C], 0.0) for q in quads]
    a_rks = [jnp.where(incl, aks[q][C:], 0.0) for q in quads]
    invs = [eye_all + pws[q] for q in quads]
    pws = [_mm(pws[q], stack_tt(pws[q])) for q in quads]
    for _ in range(n_levels - 1):
        bds = [stack_tt(pws[q]) for q in quads]
        res = [_mm(jnp.concatenate([pws[q], invs[q]], axis=0), bds[q]) for q in quads]
        pws = [res[q][:C] for q in quads]
        invs = [invs[q] + res[q][C:] for q in quads]
    invs = [invs[q] + _mm(invs[q], stack_tt(pws[q])) for q in quads]
    arss = [_mm(ars[q], states[q], _NT) for q in quads]
    avs = [_mm(jnp.concatenate([a_aks[q], a_rks[q]], axis=0), v_sts[q]) for q in quads]
    ws = [arss[q][:C] + avs[q][:C] for q in quads]
    us = [_mm(invs[q], stack_hc(ws[q])) for q in quads]
    ys = [arss[q][C:] + avs[q][C:] + _mm(a_rbs[q], stack_hc(us[q])) for q in quads]
    new_states = []
    for q in quads:
        uv = jnp.concatenate([us[q].astype(BF16), v[:, sls[q]]], axis=0)
        bk = jnp.concatenate([bg[:, sls[q]], kg[:, sls[q]]], axis=0)
        new_states.append(states[q] * ops["g_tot"][:, sls[q]] + jnp.where(mask_bd, _mm(uv, bk, _TN), 0.0))
    y = jnp.concatenate(ys, axis=1)

    (ysum,) = _headsum([y], ones_bd, split=True)
    yc = y - ysum * (1.0 / HEAD)
    (vsum,) = _headsum([yc * yc], ones_bd, split=False)
    yn = yc * lax.rsqrt(vsum * (1.0 / HEAD) + GN_EPS) * lng_ref[...] + lnb_ref[...]
    return (yn + ops["bonus"]) * ops["gate"], new_states


def _wkv_chunks_kernel(at_ref, rt_ref, bt_ref, kt_ref, v_ref, bg_ref, kg_ref, bonus_ref, gate_ref, gt_ref, s0_ref,
                       lng_ref, lnb_ref, o_ref, so_ref, st_ref, *, chunk, n_levels):
    j = pl.program_id(1)
    NQ = st_ref.shape[0]
    refs = dict(zip(_OPS_MM + _OPS_OUT,
                    (at_ref, rt_ref, bt_ref, kt_ref, v_ref, bg_ref, kg_ref, bonus_ref, gate_ref)))

    @pl.when(j == 0)
    def _init():
        zero_blk = jnp.zeros((HEAD, HEAD), F32)
        for q in range(NQ):
            st_ref[q] = jnp.concatenate(
                [jnp.concatenate([s0_ref[0, QUAD * q + h] if i == h else zero_blk for i in range(QUAD)], axis=1)
                 for h in range(QUAD)], axis=0)

    states = [st_ref[q] for q in range(NQ)]
    for c in range(at_ref.shape[0] // chunk):
        rows = slice(c * chunk, (c + 1) * chunk)
        ops = {n: ref[rows, :] for n, ref in refs.items()}
        ops["g_tot"] = gt_ref[c]
        o, states = _wkv_core(ops, states, lng_ref, lnb_ref, n_levels=n_levels)
        o_ref[rows, :] = o.astype(o_ref.dtype)
    for q in range(NQ):
        st_ref[q] = states[q]

    @pl.when(j == pl.num_programs(1) - 1)
    def _state_out():
        for q in range(NQ):
            for h in range(QUAD):
                so_ref[0, QUAD * q + h] = states[q][h * HEAD:(h + 1) * HEAD, h * HEAD:(h + 1) * HEAD]


def _wkv_chunks(ops, s0, lng, lnb, *, seq_len, chunk, chunks_per_step):
    n, e = ops["at"].shape
    b = n // seq_len
    nh = e // HEAD
    tm = chunk * chunks_per_step
    steps = seq_len // tm
    rows = pl.BlockSpec((tm, e), lambda i, j: (i * steps + j, 0))
    c2 = lambda shape: pl.BlockSpec(shape, lambda i, j: (0, 0))
    sblk = pl.BlockSpec((1, nh, HEAD, HEAD), lambda i, j: (i, 0, 0, 0))
    n_levels = max(1, math.ceil(math.log2(chunk)) - 1)
    return pl.pallas_call(
        functools.partial(_wkv_chunks_kernel, chunk=chunk, n_levels=n_levels),
        grid=(b, steps),
        in_specs=([rows] * (len(_OPS_MM) + len(_OPS_OUT))
                  + [pl.BlockSpec((chunks_per_step, 1, e), lambda i, j: (i * steps + j, 0, 0)),
                     sblk, c2((1, e)), c2((1, e))]),
        out_specs=[rows, sblk],
        out_shape=[jax.ShapeDtypeStruct((n, e), BF16), jax.ShapeDtypeStruct((b, nh, HEAD, HEAD), F32)],
        scratch_shapes=[pltpu.VMEM((e // QW, QW, QW), F32)],
        compiler_params=pltpu.CompilerParams(dimension_semantics=("arbitrary", "arbitrary"),
                                             vmem_limit_bytes=VMEM_LIMIT),
        name="wkv_chunks",
    )(*(ops[n_] for n_ in _OPS_MM + _OPS_OUT), ops["g_tot"], s0, lng, lnb)


_VT = ("nkk", "w", "b", "k", "r", "v")


def _wkv_seq_kernel(pr_ref, pk_ref, pv_ref, pl_ref, pz_ref, sr_ref, sk_ref, sv_ref, sl_ref,
                    mur_ref, muk_ref, muv_ref, mul_ref, w0_ref, w2_ref, a0_ref, a2_ref, kk_ref, ka_ref, rk_ref,
                    lng_ref, lnb_ref, s_ref, o_ref, so_ref, vt_ref, yt_ref):
    T, B, _ = pr_ref.shape
    ones2 = _head_ones(HPAIR)

    def shifted(p_ref, prev_ref, mu_ref):
        sh = p_ref[...].reshape(T * B, HPAIR)
        prev = jnp.concatenate([prev_ref[...], sh[:(T - 1) * B]], axis=0)
        return sh + (prev - sh) * mu_ref[...]

    r = shifted(pr_ref, sr_ref, mur_ref)
    k = shifted(pk_ref, sk_ref, muk_ref)
    v = shifted(pv_ref, sv_ref, muv_ref)
    x_lora = shifted(pl_ref, sl_ref, mul_ref)
    z = pz_ref[...].reshape(T * B, HPAIR)
    lane_l = _iota((T * B, 2 * LORA), 1)
    t_lora = jnp.where(lane_l < LORA, jnp.tanh(x_lora), x_lora)
    wf = w0_ref[...] + _mm(t_lora, w2_ref[...])
    af = a0_ref[...] + _mm(x_lora, a2_ref[...])
    w = jnp.exp(-math.exp(-0.5) * _sigmoid(wf))
    a = _sigmoid(af)
    kk0 = k * kk_ref[...]
    k2 = k * (1.0 + (a - 1.0) * ka_ref[...])
    sums = _mm_sum_rhs(jnp.concatenate([kk0 * kk0, r * k2 * rk_ref[...]], axis=0), ones2)
    kk = kk0 * jnp.minimum(lax.rsqrt(sums[:T * B]), 1e12)
    bonus = sums[T * B:] * v
    vecs = dict(nkk=-kk, w=w, b=kk * a, k=k2, r=r, v=v)
    for i, n in enumerate(_VT):
        for t in range(T):
            vt_ref[i, t] = vecs[n][t * B:(t + 1) * B].T

    for h in range(2):
        ch = slice(h * HEAD, (h + 1) * HEAD)

        def body(g, carry, h=h, ch=ch):
            v0 = pl.multiple_of(g * 8, 8)
            vrows = [vt_ref[5, t, pl.ds(h * HEAD + v0, 8), :] for t in range(T)]
            ys = [[] for _ in range(T)]
            for u in range(8):
                s = s_ref[h, v0 + u]
                for t in range(T):
                    sa = jnp.sum(s * vt_ref[0, t, ch, :], axis=0, keepdims=True)
                    s = s * vt_ref[1, t, ch, :] + sa * vt_ref[2, t, ch, :] + vrows[t][u:u + 1] * vt_ref[3, t, ch, :]
                    ys[t].append(jnp.sum(s * vt_ref[4, t, ch, :], axis=0, keepdims=True))
                so_ref[h, v0 + u] = s
            for t in range(T):
                yt_ref[t, pl.ds(h * HEAD + v0, 8), :] = jnp.concatenate(ys[t], axis=0)
            return carry

        lax.fori_loop(0, HEAD // 8, body, 0)

    y = jnp.concatenate([yt_ref[t].T for t in range(T)], axis=0)
    yc = y - _mm_sum_rhs(y, ones2) * (1.0 / HEAD)
    var = _mm_sum_rhs(yc * yc, ones2) * (1.0 / HEAD)
    yn = yc * lax.rsqrt(var + GN_EPS) * lng_ref[...] + lnb_ref[...]
    o_ref[...] = ((yn + bonus) * (z * _sigmoid(z))).reshape(T, B, HPAIR)


def _wkv_seq(proj_t, shift_prev, s_t, prm):
    t, b, cols = proj_t.shape
    e = prm["w0"].shape[1]
    nh = e // HEAD
    kb = e // HPAIR
    bt = 128
    off = dict(r=0, k=kb, v=2 * kb, l=3 * kb)
    pspec = lambda o: pl.BlockSpec((t, bt, HPAIR), lambda i, j, o=o: (0, j, o + i))
    pfix = lambda o: pl.BlockSpec((t, bt, HPAIR), lambda i, j, o=o: (0, j, o))
    sspec = lambda o: pl.BlockSpec((bt, HPAIR), lambda i, j, o=o: (j, o + i))
    sfix = lambda o: pl.BlockSpec((bt, HPAIR), lambda i, j, o=o: (j, o))
    mspec = lambda o: pl.BlockSpec((1, HPAIR), lambda i, j, o=o: (0, o + i))
    mfix = lambda o: pl.BlockSpec((1, HPAIR), lambda i, j, o=o: (0, o))
    row = pl.BlockSpec((1, HPAIR), lambda i, j: (0, i))
    lora = pl.BlockSpec((2 * LORA, HPAIR), lambda i, j: (0, i))
    sblk = pl.BlockSpec((2, HEAD, HEAD, bt), lambda i, j: (i, 0, 0, j))
    return pl.pallas_call(
        _wkv_seq_kernel,
        grid=(nh // 2, b // bt),
        in_specs=[pspec(off["r"]), pspec(off["k"]), pspec(off["v"]), pfix(off["l"]), pspec(off["l"] + 1),
                  sspec(off["r"]), sspec(off["k"]), sspec(off["v"]), sfix(off["l"]),
                  mspec(off["r"]), mspec(off["k"]), mspec(off["v"]), mfix(off["l"]),
                  row, lora, row, lora, row, row, row, row, row, sblk],
        out_specs=[pl.BlockSpec((t, bt, HPAIR), lambda i, j: (0, j, i)), sblk],
        out_shape=[jax.ShapeDtypeStruct((t, b, e), F32), jax.ShapeDtypeStruct(s_t.shape, F32)],
        scratch_shapes=[pltpu.VMEM((len(_VT), t, HPAIR, bt), F32),
                        pltpu.VMEM((t, HPAIR, bt), F32)],
        compiler_params=pltpu.CompilerParams(dimension_semantics=("arbitrary", "arbitrary"),
                                             vmem_limit_bytes=VMEM_LIMIT),
        name="wkv_seq",
    )(proj_t, proj_t, proj_t, proj_t, proj_t, shift_prev, shift_prev, shift_prev, shift_prev,
      prm["mu"], prm["mu"], prm["mu"], prm["mu"], prm["w0"], prm["w2p"], prm["a0"], prm["a2p"],
      prm["kk"], prm["ka"], prm["rk"], prm["lng"], prm["lnb"], s_t)


def _out_residual_kernel(o_ref, w_ref, x_ref, y_ref):
    y_ref[...] = x_ref[...] + jnp.dot(o_ref[...].astype(BF16), w_ref[...], preferred_element_type=F32)


def _out_residual(o2d, w_bf16, x2d, tm):
    n, e = o2d.shape
    d = x2d.shape[1]
    return pl.pallas_call(
        _out_residual_kernel,
        grid=(n // tm,),
        in_specs=[pl.BlockSpec((tm, e), lambda i: (i, 0)),
                  pl.BlockSpec((e, d), lambda i: (0, 0)),
                  pl.BlockSpec((tm, d), lambda i: (i, 0))],
        out_specs=pl.BlockSpec((tm, d), lambda i: (i, 0)),
        out_shape=jax.ShapeDtypeStruct((n, d), F32),
        compiler_params=pltpu.CompilerParams(dimension_semantics=("arbitrary",), vmem_limit_bytes=VMEM_LIMIT),
        name="out_residual",
    )(o2d, w_bf16, x2d)


def _gmlp_kernel(x_ref, g_ref, win_ref, vg_ref, vb_ref, wm_ref, bs_ref, wout_ref, nf_ref, *out_refs, emit_v):
    y_ref = out_refs[0]
    tm = x_ref.shape[0]
    e = vg_ref.shape[1]
    gd = e // GM_GROUPS
    x = x_ref[...]
    h = _rms_rows(x, g_ref[...]).astype(BF16)
    proj = jnp.dot(h, win_ref[...], preferred_element_type=F32)
    u = jax.nn.gelu(proj[:, :e])
    vf = jax.nn.gelu(proj[:, e:2 * e])
    z = proj[:, 2 * e:]
    vm = jnp.mean(vf, axis=-1, keepdims=True)
    vc = vf - vm
    vv = jnp.mean(vc * vc, axis=-1, keepdims=True)
    vn = vc * lax.rsqrt(vv + LN_EPS) * vg_ref[...] + vb_ref[...]
    if emit_v:
        out_refs[1][...] = vn
    vn16 = vn.astype(BF16)
    causal = _iota((GM_CHUNK, GM_CHUNK), 0) >= _iota((GM_CHUNK, GM_CHUNK), 1)
    rows = []
    for j in range(tm // GM_CHUNK):
        cols = []
        for gi in range(GM_GROUPS):
            wm = jnp.where(causal, wm_ref[gi], 0.0).astype(BF16)
            blk = vn16[j * GM_CHUNK:(j + 1) * GM_CHUNK, gi * gd:(gi + 1) * gd]
            cols.append(jnp.dot(wm, blk, preferred_element_type=F32) + bs_ref[gi])
        rows.append(jnp.concatenate(cols, axis=1))
    mixed = jnp.concatenate(rows, axis=0)
    o = u * mixed * (z * _sigmoid(z))
    x2 = x + jnp.dot(o.astype(BF16), wout_ref[...], preferred_element_type=F32)
    y_ref[...] = _rms_rows(x2, nf_ref[...])


def _gmlp(x2d, g, win_bf16, vg, vb, wmix, bias, wout_bf16, nf, *, tm, emit_v):
    n, d = x2d.shape
    e = vg.shape[0]
    const2 = lambda shape: pl.BlockSpec(shape, lambda i: (0, 0))
    out_shape = [jax.ShapeDtypeStruct((n, d), F32)]
    out_specs = [pl.BlockSpec((tm, d), lambda i: (i, 0))]
    if emit_v:
        out_shape.append(jax.ShapeDtypeStruct((n, e), F32))
        out_specs.append(pl.BlockSpec((tm, e), lambda i: (i, 0)))
    outs = pl.pallas_call(
        functools.partial(_gmlp_kernel, emit_v=emit_v),
        grid=(n // tm,),
        in_specs=[pl.BlockSpec((tm, d), lambda i: (i, 0)),
                  const2((1, d)),
                  pl.BlockSpec((d, 3 * e), lambda i: (0, 0), pipeline_mode=pl.Buffered(1)),
                  const2((1, e)), const2((1, e)),
                  pl.BlockSpec((GM_GROUPS, GM_CHUNK, GM_CHUNK), lambda i: (0, 0, 0)),
                  pl.BlockSpec((GM_GROUPS, GM_CHUNK, 1), lambda i: (0, 0, 0)),
                  pl.BlockSpec((e, d), lambda i: (0, 0), pipeline_mode=pl.Buffered(1)),
                  const2((1, d))],
        out_specs=out_specs,
        out_shape=out_shape,
        compiler_params=pltpu.CompilerParams(dimension_semantics=("arbitrary",), vmem_limit_bytes=VMEM_LIMIT),
        name="gmlp",
    )(x2d, g.reshape(1, d), win_bf16, vg.reshape(1, e), vb.reshape(1, e), wmix,
      bias.reshape(GM_GROUPS, GM_CHUNK, 1), wout_bf16, nf.reshape(1, d))
    return outs


def _row_tile(n, pref):
    t = pref
    while n % t:
        t //= 2
    return t


def kernel(x_prompt, x_sample, state_shift, state_wkv, norm_g, norm_f, rw_in, rw_mu, rw_w0, rw_w2, rw_a0, rw_a2, rw_kk, rw_ka, rw_rk, rw_lnx_g, rw_lnx_b, rw_out, gm_in, gm_vg, gm_vb, gm_ws, gm_bs, gm_out):
    bp, tp, d = x_prompt.shape
    bs_, ts, _ = x_sample.shape
    e = rw_w0.shape[1]
    nh = e // HEAD
    shift = 3 * e + 2 * LORA
    xp = x_prompt.reshape(bp * tp, d)

    zeros_l = jnp.zeros((LORA, e), F32)
    prm = dict(
        mu=rw_mu[0].reshape(1, shift), w0=rw_w0[0].reshape(1, e), a0=rw_a0[0].reshape(1, e),
        w2p=jnp.concatenate([rw_w2[0], zeros_l], axis=0).astype(BF16),
        a2p=jnp.concatenate([zeros_l, rw_a2[0]], axis=0).astype(BF16),
        kk=rw_kk[0].reshape(1, e), ka=rw_ka[0].reshape(1, e), rk=rw_rk[0].reshape(1, e),
        lng=rw_lnx_g[0].reshape(1, e), lnb=rw_lnx_b[0].reshape(1, e))
    w_in = rw_in[0].astype(BF16)
    w_out = rw_out[0].astype(BF16)

    ops_p, shift_p = _proj_prep(x_prompt, norm_g[0], w_in, jnp.zeros((bp, shift), F32), prm,
                                chunk=WKV_CHUNK, tm=_row_tile(tp, 4 * WKV_CHUNK))
    o_p, wkv_p = _wkv_chunks(ops_p, jnp.zeros((bp, nh, HEAD, HEAD), F32), prm["lng"], prm["lnb"],
                             seq_len=tp, chunk=WKV_CHUNK, chunks_per_step=4)
    x1p = _out_residual(o_p, w_out, xp, _row_tile(bp * tp, 512))

    xs_t = jnp.transpose(x_sample, (1, 0, 2)).reshape(ts * bs_, d)
    proj_s = _norm_proj(xs_t, norm_g[0], w_in, _row_tile(bs_ * ts, 256)).reshape(ts, bs_, shift + e)
    o_s, wkv_s_t = _wkv_seq(proj_s, state_shift[0], jnp.transpose(state_wkv[0], (1, 2, 3, 0)), prm)
    wkv_s = jnp.transpose(wkv_s_t, (3, 0, 1, 2))
    shift_s = proj_s[ts - 1, :, :shift]
    x1s_t = _out_residual(o_s.reshape(ts * bs_, e), w_out, xs_t, _row_tile(bs_ * ts, 512))
    x1s = jnp.transpose(x1s_t.reshape(ts, bs_, d), (1, 0, 2)).reshape(bs_ * ts, d)

    g_in = gm_in[0].astype(BF16)
    g_out = gm_out[0].astype(BF16)
    wm_p = gm_ws[0]
    reps = GM_CHUNK // ts
    eye = jnp.eye(reps, dtype=F32)
    wm_s = jax.vmap(lambda w: jnp.kron(eye, w))(wm_p[:, :ts, :ts])
    bias_s = jnp.tile(gm_bs[0][:, :ts], (1, reps))

    (y_p,) = _gmlp(x1p, norm_g[1], g_in, gm_vg[0], gm_vb[0], wm_p, gm_bs[0], g_out, norm_f,
                   tm=_row_tile(bp * tp, 512), emit_v=False)
    y_s, v_s = _gmlp(x1s, norm_g[1], g_in, gm_vg[0], gm_vb[0], wm_s, bias_s, g_out, norm_f,
                     tm=_row_tile(bs_ * ts, 256), emit_v=True)

    return (y_p.reshape(bp, tp, d), y_s.reshape(bs_, ts, d),
            shift_p[None], wkv_p[None], shift_s[None], wkv_s[None],
            v_s.reshape(1, bs_, ts, e))
```

```python
import functools
import math

import jax
import jax.numpy as jnp
from jax import lax
from jax.experimental import pallas as pl
from jax.experimental.pallas import tpu as pltpu

F32 = jnp.float32
BF16 = jnp.bfloat16

HEAD = 64
QUAD = 4
QW = QUAD * HEAD
HPAIR = 2 * HEAD
LORA = 64
NORM_EPS = 1e-6
LN_EPS = 1e-5
GN_EPS = 64e-5
GM_CHUNK = 128
GM_GROUPS = 8
WKV_CHUNK = 64
VMEM_LIMIT = 56 * 1024 * 1024


def _split2(x):
    hi = x.astype(BF16)
    lo = (x - hi.astype(F32)).astype(BF16)
    return hi, lo


_NN = (((1,), (0,)), ((), ()))
_NT = (((1,), (1,)), ((), ()))
_TN = (((0,), (0,)), ((), ()))


def _mm(a, b, dims=_NN):
    return lax.dot_general(a.astype(BF16), b.astype(BF16), dims, preferred_element_type=F32)


def _mm_sum_rhs(a, b_bf16, dims=_NN):
    d = functools.partial(lax.dot_general, dimension_numbers=dims, preferred_element_type=F32)
    h, l = _split2(a)
    return d(h, b_bf16) + d(l, b_bf16)


def _mm_sum_lhs(a_bf16, b, dims=_NN):
    d = functools.partial(lax.dot_general, dimension_numbers=dims, preferred_element_type=F32)
    h, l = _split2(b)
    return d(a_bf16, h) + d(a_bf16, l)


def _iota(shape, dim):
    return lax.broadcasted_iota(jnp.int32, shape, dim)


def _sigmoid(x):
    return 0.5 * jnp.tanh(0.5 * x) + 0.5


def _rms_rows(x, g):
    return x * lax.rsqrt(jnp.mean(x * x, axis=-1, keepdims=True) + NORM_EPS) * g


def _head_ones(width):
    return ((_iota((width, width), 0) // HEAD) == (_iota((width, width), 1) // HEAD)).astype(BF16)


def _norm_proj_kernel(x_ref, g_ref, w_ref, o_ref):
    h = _rms_rows(x_ref[...], g_ref[...]).astype(BF16)
    o_ref[...] = jnp.dot(h, w_ref[...], preferred_element_type=F32)


def _norm_proj(x2d, g, w_bf16, tm):
    n, d = x2d.shape
    cols = w_bf16.shape[1]
    return pl.pallas_call(
        _norm_proj_kernel,
        grid=(n // tm,),
        in_specs=[pl.BlockSpec((tm, d), lambda i: (i, 0)),
                  pl.BlockSpec((1, d), lambda i: (0, 0)),
                  pl.BlockSpec((d, cols), lambda i: (0, 0), pipeline_mode=pl.Buffered(1))],
        out_specs=pl.BlockSpec((tm, cols), lambda i: (i, 0)),
        out_shape=jax.ShapeDtypeStruct((n, cols), F32),
        compiler_params=pltpu.CompilerParams(dimension_semantics=("arbitrary",), vmem_limit_bytes=VMEM_LIMIT),
        name="norm_proj",
    )(x2d, g.reshape(1, d), w_bf16)


_OPS_MM = ("at", "rt", "bt", "kt", "v", "bg", "kg")
_OPS_T = ("bt", "kt")
_OPS_OUT = ("bonus", "gate")


def _proj_prep_kernel(x_ref, g_ref, w_ref, sp_ref, mu_ref, w0_ref, w2_ref, a0_ref, a2_ref, kk_ref, ka_ref, rk_ref,
                      at_ref, rt_ref, bt_ref, kt_ref, v_ref, bg_ref, kg_ref, bonus_ref, gate_ref, gt_ref, sho_ref,
                      carry_ref, *, chunk):
    j = pl.program_id(1)
    tm = x_ref.shape[0]
    E = w0_ref.shape[1]
    SHIFT = 3 * E + 2 * LORA
    C = chunk
    n_ch = tm // C
    outs = dict(zip(_OPS_MM + _OPS_OUT,
                    (at_ref, rt_ref, bt_ref, kt_ref, v_ref, bg_ref, kg_ref, bonus_ref, gate_ref)))

    @pl.when(j == 0)
    def _init():
        carry_ref[...] = sp_ref[0]

    h = _rms_rows(x_ref[...], g_ref[...]).astype(BF16)
    first_row = _iota((tm, 1), 0) == 0
    ones_bd = _head_ones(QW)
    row = _iota((tm, tm), 0)
    col = _iota((tm, tm), 1)
    tri = ((row >= col) & (row // C == col // C)).astype(BF16)

    def proj(lo, hi):
        return jnp.dot(h, w_ref[:, lo:hi], preferred_element_type=F32)

    def shifted(lo, hi):
        sh = proj(lo, hi)
        prev = jnp.where(first_row, carry_ref[:, lo:hi], pltpu.roll(sh, 1, 0))
        carry_ref[:, lo:hi] = sh[tm - 1:tm]
        return sh + (prev - sh) * mu_ref[:, lo:hi]

    def per_chunk_last(x):
        return jnp.concatenate(
            [jnp.broadcast_to(x[c * C + C - 1:c * C + C], (C, x.shape[1])) for c in range(n_ch)], axis=0)

    x_lora = shifted(3 * E, SHIFT)
    lane_l = _iota((tm, 2 * LORA), 1)
    t_lora = jnp.where(lane_l < LORA, jnp.tanh(x_lora), x_lora)
    wf = w0_ref[...] + _mm(t_lora, w2_ref[...])
    af = a0_ref[...] + _mm(x_lora, a2_ref[...])

    for q in range(E // QW):
        lo = q * QW
        sl = slice(lo, lo + QW)
        r = shifted(lo, lo + QW)
        k = shifted(E + lo, E + lo + QW)
        v = shifted(2 * E + lo, 2 * E + lo + QW)
        z = proj(SHIFT + lo, SHIFT + lo + QW)
        ld = -math.exp(-0.5) * _sigmoid(wf[:, sl])
        a = _sigmoid(af[:, sl])
        kk0 = k * kk_ref[:, sl]
        k2 = k * (1.0 + (a - 1.0) * ka_ref[:, sl])
        sums = _mm(jnp.concatenate([kk0 * kk0, r * k2 * rk_ref[:, sl]], axis=0), ones_bd)
        kk = kk0 * jnp.minimum(lax.rsqrt(sums[:tm]), 1e12)
        cum = _mm_sum_lhs(tri, ld)
        cum_last = per_chunk_last(cum)
        g_rem = jnp.exp(cum_last - cum)
        g_inv = jnp.exp(-cum)
        kka = kk * a
        quad = dict(at=-kk * jnp.exp(cum - ld), rt=r * jnp.exp(cum), bt=kka * g_inv, kt=k2 * g_inv, v=v,
                    bg=kka * g_rem, kg=k2 * g_rem, bonus=sums[tm:] * v, gate=z * _sigmoid(z))
        for n, x in quad.items():
            if n in _OPS_T:
                outs[n][0, sl, :] = x.astype(BF16).T
            else:
                outs[n][:, sl] = x.astype(outs[n].dtype)
        for c in range(n_ch):
            gt_ref[c, :, sl] = jnp.exp(cum[c * C + C - 1:c * C + C])

    @pl.when(j == pl.num_programs(1) - 1)
    def _shift_out():
        sho_ref[0] = carry_ref[...]


def _proj_prep(x3d, g, w_bf16, shift_prev, prm, *, chunk, tm):
    b, t, d = x3d.shape
    e = prm["w0"].shape[1]
    shift = 3 * e + 2 * LORA
    cols = w_bf16.shape[1]
    steps = t // tm
    n_ch = tm // chunk
    c2 = lambda shape: pl.BlockSpec(shape, lambda i, j: (0, 0))
    rows = pl.BlockSpec((tm, e), lambda i, j: (i * steps + j, 0))
    cols_t = pl.BlockSpec((1, e, tm), lambda i, j: (i * steps + j, 0, 0))
    outs = pl.pallas_call(
        functools.partial(_proj_prep_kernel, chunk=chunk),
        grid=(b, steps),
        in_specs=[pl.BlockSpec((tm, d), lambda i, j: (i * steps + j, 0)),
                  c2((1, d)),
                  pl.BlockSpec((d, cols), lambda i, j: (0, 0), pipeline_mode=pl.Buffered(1)),
                  pl.BlockSpec((1, 1, shift), lambda i, j: (i, 0, 0)),
                  c2((1, shift)), c2((1, e)), c2((2 * LORA, e)), c2((1, e)), c2((2 * LORA, e)),
                  c2((1, e)), c2((1, e)), c2((1, e))],
        out_specs=([cols_t if n in _OPS_T else rows for n in _OPS_MM + _OPS_OUT]
                   + [pl.BlockSpec((n_ch, 1, e), lambda i, j: (i * steps + j, 0, 0)),
                      pl.BlockSpec((1, 1, shift), lambda i, j: (i, 0, 0))]),
        out_shape=([jax.ShapeDtypeStruct((b * steps, e, tm) if n in _OPS_T else (b * t, e), BF16)
                    for n in _OPS_MM + _OPS_OUT]
                   + [jax.ShapeDtypeStruct((b * t // chunk, 1, e), F32),
                      jax.ShapeDtypeStruct((b, 1, shift), F32)]),
        scratch_shapes=[pltpu.VMEM((1, shift), F32)],
        compiler_params=pltpu.CompilerParams(dimension_semantics=("arbitrary", "arbitrary"),
                                             vmem_limit_bytes=VMEM_LIMIT),
        name="proj_prep",
    )(x3d.reshape(b * t, d), g.reshape(1, d), w_bf16, shift_prev.reshape(b, 1, shift),
      prm["mu"], prm["w0"], prm["w2p"], prm["a0"], prm["a2p"], prm["kk"], prm["ka"], prm["rk"])
    ops = dict(zip(_OPS_MM + _OPS_OUT + ("g_tot",), outs[:-1]))
    return ops, outs[-1].reshape(b, shift)


def _headsum(xs, ones_bd, split):
    e = xs[0].shape[1]
    nq = e // QW
    rows = [x[:, q * QW:(q + 1) * QW] for x in xs for q in range(nq)]
    stacked = jnp.concatenate(rows, axis=0)
    s = _mm_sum_rhs(stacked, ones_bd) if split else _mm(stacked, ones_bd)
    c = xs[0].shape[0]
    outs = []
    for i in range(len(xs)):
        outs.append(jnp.concatenate([s[(i * nq + q) * c:(i * nq + q + 1) * c] for q in range(nq)], axis=1))
    return outs


def _wkv_core(ops, states, lng_ref, lnb_ref, *, n_levels):
    C, E = ops["at"].shape
    TL = QUAD * C
    NQ = E // QW
    mask_bd = (_iota((QW, QW), 0) // HEAD) == (_iota((QW, QW), 1) // HEAD)
    ones_bd = mask_bd.astype(BF16)
    tok_t = _iota((C, TL), 0)
    tok_j = _iota((C, TL), 1) % C
    strict = tok_j < tok_t
    incl = tok_j <= tok_t
    eye_all = (tok_j == tok_t).astype(F32)

    def block_diag(x, width):
        assert 2 * width == 128
        x = x.astype(BF16)
        zero = jnp.zeros((), BF16)
        half = _iota((C, 128), 1) < width
        pad = jnp.zeros((C, 128), BF16)
        rows = []
        for h in range(QUAD):
            t = h // 2
            tile = jnp.where(half if h % 2 == 0 else ~half, x[:, t * 128:(t + 1) * 128], zero)
            rows.append(jnp.concatenate([tile if i == t else pad for i in range(QUAD // 2)], axis=1))
        return jnp.concatenate(rows, axis=0)

    def cols_bd(xt):
        zero = jnp.zeros((HEAD, C), xt.dtype)
        return jnp.concatenate(
            [jnp.concatenate([xt[h * HEAD:(h + 1) * HEAD] if i == h else zero for i in range(QUAD)], axis=1)
             for h in range(QUAD)], axis=0)

    stack_hc = functools.partial(block_diag, width=HEAD)
    stack_tt = functools.partial(block_diag, width=C)

    at, rt, bt, kt, v, bg, kg = (ops[n] for n in _OPS_MM)
    quads = range(NQ)
    sls = [slice(q * QW, (q + 1) * QW) for q in quads]
    ars = [jnp.concatenate([at[:, sl], rt[:, sl]], axis=0) for sl in sls]
    v_sts = [stack_hc(v[:, sl]) for sl in sls]
    abs_ = [_mm(ars[q], cols_bd(bt[sls[q], :])) for q in quads]
    aks = [_mm(ars[q], cols_bd(kt[sls[q], :])) for q in quads]
    pws = [jnp.where(strict, abs_[q][:C], 0.0) for q in quads]
    a_rbs = [jnp.where(incl, abs_[q][C:], 0.0) for q in quads]
    a_aks = [jnp.where(strict, aks[q][:C], 0.0) for q in quads]
    a_rks = [jnp.where(incl, aks[q][C:], 0.0) for q in quads]
    invs = [eye_all + pws[q] for q in quads]
    pws = [_mm(pws[q], stack_tt(pws[q])) for q in quads]
    for _ in range(n_levels - 1):
        bds = [stack_tt(pws[q]) for q in quads]
        res = [_mm(jnp.concatenate([pws[q], invs[q]], axis=0), bds[q]) for q in quads]
        pws = [res[q][:C] for q in quads]
        invs = [invs[q] + res[q][C:] for q in quads]
    invs = [invs[q] + _mm(invs[q], stack_tt(pws[q])) for q in quads]
    arss = [_mm(ars[q], states[q], _NT) for q in quads]
    avs = [_mm(jnp.concatenate([a_aks[q], a_rks[q]], axis=0), v_sts[q]) for q in quads]
    ws = [arss[q][:C] + avs[q][:C] for q in quads]
    us = [_mm(invs[q], stack_hc(ws[q])) for q in quads]
    ys = [arss[q][C:] + avs[q][C:] + _mm(a_rbs[q], stack_hc(us[q])) for q in quads]
    new_states = []
    for q in quads:
        uv = jnp.concatenate([us[q].astype(BF16), v[:, sls[q]]], axis=0)
        bk = jnp.concatenate([bg[:, sls[q]], kg[:, sls[q]]], axis=0)
        new_states.append(states[q] * ops["g_tot"][:, sls[q]] + jnp.where(mask_bd, _mm(uv, bk, _TN), 0.0))
    y = jnp.concatenate(ys, axis=1)

    (ysum,) = _headsum([y], ones_bd, split=True)
    yc = y - ysum * (1.0 / HEAD)
    (vsum,) = _headsum([yc * yc], ones_bd, split=False)
    yn = yc * lax.rsqrt(vsum * (1.0 / HEAD) + GN_EPS) * lng_ref[...] + lnb_ref[...]
    return (yn + ops["bonus"]) * ops["gate"], new_states


def _wkv_chunks_kernel(at_ref, rt_ref, bt_ref, kt_ref, v_ref, bg_ref, kg_ref, bonus_ref, gate_ref, gt_ref, s0_ref,
                       lng_ref, lnb_ref, o_ref, so_ref, st_ref, *, chunk, n_levels):
    j = pl.program_id(1)
    NQ = st_ref.shape[0]
    refs = dict(zip(_OPS_MM + _OPS_OUT,
                    (at_ref, rt_ref, bt_ref, kt_ref, v_ref, bg_ref, kg_ref, bonus_ref, gate_ref)))

    @pl.when(j == 0)
    def _init():
        zero_blk = jnp.zeros((HEAD, HEAD), F32)
        for q in range(NQ):
            st_ref[q] = jnp.concatenate(
                [jnp.concatenate([s0_ref[0, QUAD * q + h] if i == h else zero_blk for i in range(QUAD)], axis=1)
                 for h in range(QUAD)], axis=0)

    states = [st_ref[q] for q in range(NQ)]
    for c in range(at_ref.shape[0] // chunk):
        rows = slice(c * chunk, (c + 1) * chunk)
        ops = {n: (ref[0, :, rows] if n in _OPS_T else ref[rows, :]) for n, ref in refs.items()}
        ops["g_tot"] = gt_ref[c]
        o, states = _wkv_core(ops, states, lng_ref, lnb_ref, n_levels=n_levels)
        o_ref[rows, :] = o.astype(o_ref.dtype)
    for q in range(NQ):
        st_ref[q] = states[q]

    @pl.when(j == pl.num_programs(1) - 1)
    def _state_out():
        for q in range(NQ):
            for h in range(QUAD):
                so_ref[0, QUAD * q + h] = states[q][h * HEAD:(h + 1) * HEAD, h * HEAD:(h + 1) * HEAD]


def _wkv_chunks(ops, s0, lng, lnb, *, seq_len, chunk, chunks_per_step):
    n, e = ops["at"].shape
    b = n // seq_len
    nh = e // HEAD
    tm = chunk * chunks_per_step
    steps = seq_len // tm
    assert ops["bt"].shape[2] == tm
    rows = pl.BlockSpec((tm, e), lambda i, j: (i * steps + j, 0))
    cols_t = pl.BlockSpec((1, e, tm), lambda i, j: (i * steps + j, 0, 0))
    c2 = lambda shape: pl.BlockSpec(shape, lambda i, j: (0, 0))
    sblk = pl.BlockSpec((1, nh, HEAD, HEAD), lambda i, j: (i, 0, 0, 0))
    n_levels = max(1, math.ceil(math.log2(chunk)) - 1)
    return pl.pallas_call(
        functools.partial(_wkv_chunks_kernel, chunk=chunk, n_levels=n_levels),
        grid=(b, steps),
        in_specs=([cols_t if n_ in _OPS_T else rows for n_ in _OPS_MM + _OPS_OUT]
                  + [pl.BlockSpec((chunks_per_step, 1, e), lambda i, j: (i * steps + j, 0, 0)),
                     sblk, c2((1, e)), c2((1, e))]),
        out_specs=[rows, sblk],
        out_shape=[jax.ShapeDtypeStruct((n, e), BF16), jax.ShapeDtypeStruct((b, nh, HEAD, HEAD), F32)],
        scratch_shapes=[pltpu.VMEM((e // QW, QW, QW), F32)],
        compiler_params=pltpu.CompilerParams(dimension_semantics=("arbitrary", "arbitrary"),
                                             vmem_limit_bytes=VMEM_LIMIT),
        name="wkv_chunks",
    )(*(ops[n_] for n_ in _OPS_MM + _OPS_OUT), ops["g_tot"], s0, lng, lnb)


_VT = ("nkk", "w", "b", "k", "r", "v")


def _wkv_seq_kernel(pr_ref, pk_ref, pv_ref, pl_ref, pz_ref, sr_ref, sk_ref, sv_ref, sl_ref,
                    mur_ref, muk_ref, muv_ref, mul_ref, w0_ref, w2_ref, a0_ref, a2_ref, kk_ref, ka_ref, rk_ref,
                    lng_ref, lnb_ref, s_ref, o_ref, so_ref, vt_ref, yt_ref):
    T, B, _ = pr_ref.shape
    ones2 = _head_ones(HPAIR)

    def shifted(p_ref, prev_ref, mu_ref):
        sh = p_ref[...].reshape(T * B, HPAIR)
        prev = jnp.concatenate([prev_ref[...], sh[:(T - 1) * B]], axis=0)
        return sh + (prev - sh) * mu_ref[...]

    r = shifted(pr_ref, sr_ref, mur_ref)
    k = shifted(pk_ref, sk_ref, muk_ref)
    v = shifted(pv_ref, sv_ref, muv_ref)
    x_lora = shifted(pl_ref, sl_ref, mul_ref)
    z = pz_ref[...].reshape(T * B, HPAIR)
    lane_l = _iota((T * B, 2 * LORA), 1)
    t_lora = jnp.where(lane_l < LORA, jnp.tanh(x_lora), x_lora)
    wf = w0_ref[...] + _mm(t_lora, w2_ref[...])
    af = a0_ref[...] + _mm(x_lora, a2_ref[...])
    w = jnp.exp(-math.exp(-0.5) * _sigmoid(wf))
    a = _sigmoid(af)
    kk0 = k * kk_ref[...]
    k2 = k * (1.0 + (a - 1.0) * ka_ref[...])
    sums = _mm_sum_rhs(jnp.concatenate([kk0 * kk0, r * k2 * rk_ref[...]], axis=0), ones2)
    kk = kk0 * jnp.minimum(lax.rsqrt(sums[:T * B]), 1e12)
    bonus = sums[T * B:] * v
    vecs = dict(nkk=-kk, w=w, b=kk * a, k=k2, r=r, v=v)
    for i, n in enumerate(_VT):
        for t in range(T):
            vt_ref[i, t] = vecs[n][t * B:(t + 1) * B].T

    for h in range(2):
        ch = slice(h * HEAD, (h + 1) * HEAD)

        def body(g, carry, h=h, ch=ch):
            v0 = pl.multiple_of(g * 8, 8)
            vrows = [vt_ref[5, t, pl.ds(h * HEAD + v0, 8), :] for t in range(T)]
            ys = [[] for _ in range(T)]
            for u in range(8):
                s = s_ref[h, v0 + u]
                for t in range(T):
                    sa = jnp.sum(s * vt_ref[0, t, ch, :], axis=0, keepdims=True)
                    s = s * vt_ref[1, t, ch, :] + sa * vt_ref[2, t, ch, :] + vrows[t][u:u + 1] * vt_ref[3, t, ch, :]
                    ys[t].append(jnp.sum(s * vt_ref[4, t, ch, :], axis=0, keepdims=True))
                so_ref[h, v0 + u] = s
            for t in range(T):
                yt_ref[t, pl.ds(h * HEAD + v0, 8), :] = jnp.concatenate(ys[t], axis=0)
            return carry

        lax.fori_loop(0, HEAD // 8, body, 0)

    y = jnp.concatenate([yt_ref[t].T for t in range(T)], axis=0)
    yc = y - _mm_sum_rhs(y, ones2) * (1.0 / HEAD)
    var = _mm_sum_rhs(yc * yc, ones2) * (1.0 / HEAD)
    yn = yc * lax.rsqrt(var + GN_EPS) * lng_ref[...] + lnb_ref[...]
    o_ref[...] = ((yn + bonus) * (z * _sigmoid(z))).reshape(T, B, HPAIR)


def _wkv_seq(proj_t, shift_prev, s_t, prm):
    t, b, cols = proj_t.shape
    e = prm["w0"].shape[1]
    nh = e // HEAD
    kb = e // HPAIR
    bt = 128
    off = dict(r=0, k=kb, v=2 * kb, l=3 * kb)
    pspec = lambda o: pl.BlockSpec((t, bt, HPAIR), lambda i, j, o=o: (0, j, o + i))
    pfix = lambda o: pl.BlockSpec((t, bt, HPAIR), lambda i, j, o=o: (0, j, o))
    sspec = lambda o: pl.BlockSpec((bt, HPAIR), lambda i, j, o=o: (j, o + i))
    sfix = lambda o: pl.BlockSpec((bt, HPAIR), lambda i, j, o=o: (j, o))
    mspec = lambda o: pl.BlockSpec((1, HPAIR), lambda i, j, o=o: (0, o + i))
    mfix = lambda o: pl.BlockSpec((1, HPAIR), lambda i, j, o=o: (0, o))
    row = pl.BlockSpec((1, HPAIR), lambda i, j: (0, i))
    lora = pl.BlockSpec((2 * LORA, HPAIR), lambda i, j: (0, i))
    sblk = pl.BlockSpec((2, HEAD, HEAD, bt), lambda i, j: (i, 0, 0, j))
    return pl.pallas_call(
        _wkv_seq_kernel,
        grid=(nh // 2, b // bt),
        in_specs=[pspec(off["r"]), pspec(off["k"]), pspec(off["v"]), pfix(off["l"]), pspec(off["l"] + 1),
                  sspec(off["r"]), sspec(off["k"]), sspec(off["v"]), sfix(off["l"]),
                  mspec(off["r"]), mspec(off["k"]), mspec(off["v"]), mfix(off["l"]),
                  row, lora, row, lora, row, row, row, row, row, sblk],
        out_specs=[pl.BlockSpec((t, bt, HPAIR), lambda i, j: (0, j, i)), sblk],
        out_shape=[jax.ShapeDtypeStruct((t, b, e), F32), jax.ShapeDtypeStruct(s_t.shape, F32)],
        scratch_shapes=[pltpu.VMEM((len(_VT), t, HPAIR, bt), F32),
                        pltpu.VMEM((t, HPAIR, bt), F32)],
        compiler_params=pltpu.CompilerParams(dimension_semantics=("arbitrary", "arbitrary"),
                                             vmem_limit_bytes=VMEM_LIMIT),
        name="wkv_seq",
    )(proj_t, proj_t, proj_t, proj_t, proj_t, shift_prev, shift_prev, shift_prev, shift_prev,
      prm["mu"], prm["mu"], prm["mu"], prm["mu"], prm["w0"], prm["w2p"], prm["a0"], prm["a2p"],
      prm["kk"], prm["ka"], prm["rk"], prm["lng"], prm["lnb"], s_t)


def _out_residual_kernel(o_ref, w_ref, x_ref, y_ref):
    y_ref[...] = x_ref[...] + jnp.dot(o_ref[...].astype(BF16), w_ref[...], preferred_element_type=F32)


def _out_residual(o2d, w_bf16, x2d, tm):
    n, e = o2d.shape
    d = x2d.shape[1]
    return pl.pallas_call(
        _out_residual_kernel,
        grid=(n // tm,),
        in_specs=[pl.BlockSpec((tm, e), lambda i: (i, 0)),
                  pl.BlockSpec((e, d), lambda i: (0, 0)),
                  pl.BlockSpec((tm, d), lambda i: (i, 0))],
        out_specs=pl.BlockSpec((tm, d), lambda i: (i, 0)),
        out_shape=jax.ShapeDtypeStruct((n, d), F32),
        compiler_params=pltpu.CompilerParams(dimension_semantics=("arbitrary",), vmem_limit_bytes=VMEM_LIMIT),
        name="out_residual",
    )(o2d, w_bf16, x2d)


def _gmlp_kernel(x_ref, g_ref, win_ref, vg_ref, vb_ref, wm_ref, bs_ref, wout_ref, nf_ref, *out_refs, emit_v):
    y_ref = out_refs[0]
    tm = x_ref.shape[0]
    e = vg_ref.shape[1]
    gd = e // GM_GROUPS
    x = x_ref[...]
    h = _rms_rows(x, g_ref[...]).astype(BF16)
    proj = jnp.dot(h, win_ref[...], preferred_element_type=F32)
    u = jax.nn.gelu(proj[:, :e])
    vf = jax.nn.gelu(proj[:, e:2 * e])
    z = proj[:, 2 * e:]
    vm = jnp.mean(vf, axis=-1, keepdims=True)
    vc = vf - vm
    vv = jnp.mean(vc * vc, axis=-1, keepdims=True)
    vn = vc * lax.rsqrt(vv + LN_EPS) * vg_ref[...] + vb_ref[...]
    if emit_v:
        out_refs[1][...] = vn
    vn16 = vn.astype(BF16)
    causal = _iota((GM_CHUNK, GM_CHUNK), 0) >= _iota((GM_CHUNK, GM_CHUNK), 1)
    rows = []
    for j in range(tm // GM_CHUNK):
        cols = []
        for gi in range(GM_GROUPS):
            wm = jnp.where(causal, wm_ref[gi], 0.0).astype(BF16)
            blk = vn16[j * GM_CHUNK:(j + 1) * GM_CHUNK, gi * gd:(gi + 1) * gd]
            cols.append(jnp.dot(wm, blk, preferred_element_type=F32) + bs_ref[gi])
        rows.append(jnp.concatenate(cols, axis=1))
    mixed = jnp.concatenate(rows, axis=0)
    o = u * mixed * (z * _sigmoid(z))
    x2 = x + jnp.dot(o.astype(BF16), wout_ref[...], preferred_element_type=F32)
    y_ref[...] = _rms_rows(x2, nf_ref[...])


def _gmlp(x2d, g, win_bf16, vg, vb, wmix, bias, wout_bf16, nf, *, tm, emit_v):
    n, d = x2d.shape
    e = vg.shape[0]
    const2 = lambda shape: pl.BlockSpec(shape, lambda i: (0, 0))
    out_shape = [jax.ShapeDtypeStruct((n, d), F32)]
    out_specs = [pl.BlockSpec((tm, d), lambda i: (i, 0))]
    if emit_v:
        out_shape.append(jax.ShapeDtypeStruct((n, e), F32))
        out_specs.append(pl.BlockSpec((tm, e), lambda i: (i, 0)))
    outs = pl.pallas_call(
        functools.partial(_gmlp_kernel, emit_v=emit_v),
        grid=(n // tm,),
        in_specs=[pl.BlockSpec((tm, d), lambda i: (i, 0)),
                  const2((1, d)),
                  pl.BlockSpec((d, 3 * e), lambda i: (0, 0), pipeline_mode=pl.Buffered(1)),
                  const2((1, e)), const2((1, e)),
                  pl.BlockSpec((GM_GROUPS, GM_CHUNK, GM_CHUNK), lambda i: (0, 0, 0)),
                  pl.BlockSpec((GM_GROUPS, GM_CHUNK, 1), lambda i: (0, 0, 0)),
                  pl.BlockSpec((e, d), lambda i: (0, 0), pipeline_mode=pl.Buffered(1)),
                  const2((1, d))],
        out_specs=out_specs,
        out_shape=out_shape,
        compiler_params=pltpu.CompilerParams(dimension_semantics=("arbitrary",), vmem_limit_bytes=VMEM_LIMIT),
        name="gmlp",
    )(x2d, g.reshape(1, d), win_bf16, vg.reshape(1, e), vb.reshape(1, e), wmix,
      bias.reshape(GM_GROUPS, GM_CHUNK, 1), wout_bf16, nf.reshape(1, d))
    return outs


def _row_tile(n, pref):
    t = pref
    while n % t:
        t //= 2
    return t


def kernel(x_prompt, x_sample, state_shift, state_wkv, norm_g, norm_f, rw_in, rw_mu, rw_w0, rw_w2, rw_a0, rw_a2, rw_kk, rw_ka, rw_rk, rw_lnx_g, rw_lnx_b, rw_out, gm_in, gm_vg, gm_vb, gm_ws, gm_bs, gm_out):
    bp, tp, d = x_prompt.shape
    bs_, ts, _ = x_sample.shape
    e = rw_w0.shape[1]
    nh = e // HEAD
    shift = 3 * e + 2 * LORA
    xp = x_prompt.reshape(bp * tp, d)

    zeros_l = jnp.zeros((LORA, e), F32)
    prm = dict(
        mu=rw_mu[0].reshape(1, shift), w0=rw_w0[0].reshape(1, e), a0=rw_a0[0].reshape(1, e),
        w2p=jnp.concatenate([rw_w2[0], zeros_l], axis=0).astype(BF16),
        a2p=jnp.concatenate([zeros_l, rw_a2[0]], axis=0).astype(BF16),
        kk=rw_kk[0].reshape(1, e), ka=rw_ka[0].reshape(1, e), rk=rw_rk[0].reshape(1, e),
        lng=rw_lnx_g[0].reshape(1, e), lnb=rw_lnx_b[0].reshape(1, e))
    w_in = rw_in[0].astype(BF16)
    w_out = rw_out[0].astype(BF16)

    ops_p, shift_p = _proj_prep(x_prompt, norm_g[0], w_in, jnp.zeros((bp, shift), F32), prm,
                                chunk=WKV_CHUNK, tm=_row_tile(tp, 4 * WKV_CHUNK))
    o_p, wkv_p = _wkv_chunks(ops_p, jnp.zeros((bp, nh, HEAD, HEAD), F32), prm["lng"], prm["lnb"],
                             seq_len=tp, chunk=WKV_CHUNK, chunks_per_step=4)
    x1p = _out_residual(o_p, w_out, xp, _row_tile(bp * tp, 512))

    xs_t = jnp.transpose(x_sample, (1, 0, 2)).reshape(ts * bs_, d)
    proj_s = _norm_proj(xs_t, norm_g[0], w_in, _row_tile(bs_ * ts, 256)).reshape(ts, bs_, shift + e)
    o_s, wkv_s_t = _wkv_seq(proj_s, state_shift[0], jnp.transpose(state_wkv[0], (1, 2, 3, 0)), prm)
    wkv_s = jnp.transpose(wkv_s_t, (3, 0, 1, 2))
    shift_s = proj_s[ts - 1, :, :shift]
    x1s_t = _out_residual(o_s.reshape(ts * bs_, e), w_out, xs_t, _row_tile(bs_ * ts, 512))
    x1s = jnp.transpose(x1s_t.reshape(ts, bs_, d), (1, 0, 2)).reshape(bs_ * ts, d)

    g_in = gm_in[0].astype(BF16)
    g_out = gm_out[0].astype(BF16)
    wm_p = gm_ws[0]
    reps = GM_CHUNK // ts
    eye = jnp.eye(reps, dtype=F32)
    wm_s = jax.vmap(lambda w: jnp.kron(eye, w))(wm_p[:, :ts, :ts])
    bias_s = jnp.tile(gm_bs[0][:, :ts], (1, reps))

    (y_p,) = _gmlp(x1p, norm_g[1], g_in, gm_vg[0], gm_vb[0], wm_p, gm_bs[0], g_out, norm_f,
                   tm=_row_tile(bp * tp, 512), emit_v=False)
    y_s, v_s = _gmlp(x1s, norm_g[1], g_in, gm_vg[0], gm_vb[0], wm_s, bias_s, g_out, norm_f,
                     tm=_row_tile(bs_ * ts, 256), emit_v=True)

    return (y_p.reshape(bp, tp, d), y_s.reshape(bs_, ts, d),
            shift_p[None], wkv_p[None], shift_s[None], wkv_s[None],
            v_s.reshape(1, bs_, ts, e))
```

```python
import functools
import math

import jax
import jax.numpy as jnp
from jax import lax
from jax.experimental import pallas as pl
from jax.experimental.pallas import tpu as pltpu

F32 = jnp.float32
BF16 = jnp.bfloat16

HEAD = 64
QUAD = 4
QW = QUAD * HEAD
HPAIR = 2 * HEAD
LORA = 64
NORM_EPS = 1e-6
LN_EPS = 1e-5
GN_EPS = 64e-5
GM_CHUNK = 128
GM_GROUPS = 8
WKV_CHUNK = 64
VMEM_LIMIT = 56 * 1024 * 1024


def _split2(x):
    hi = x.astype(BF16)
    lo = (x - hi.astype(F32)).astype(BF16)
    return hi, lo


_NN = (((1,), (0,)), ((), ()))
_NT = (((1,), (1,)), ((), ()))
_TN = (((0,), (0,)), ((), ()))


def _mm(a, b, dims=_NN):
    return lax.dot_general(a.astype(BF16), b.astype(BF16), dims, preferred_element_type=F32)


def _mm_sum_rhs(a, b_bf16, dims=_NN):
    d = functools.partial(lax.dot_general, dimension_numbers=dims, preferred_element_type=F32)
    h, l = _split2(a)
    return d(h, b_bf16) + d(l, b_bf16)


def _mm_sum_lhs(a_bf16, b, dims=_NN):
    d = functools.partial(lax.dot_general, dimension_numbers=dims, preferred_element_type=F32)
    h, l = _split2(b)
    return d(a_bf16, h) + d(a_bf16, l)


def _iota(shape, dim):
    return lax.broadcasted_iota(jnp.int32, shape, dim)


def _sigmoid(x):
    return 0.5 * jnp.tanh(0.5 * x) + 0.5


def _rms_rows(x, g):
    return x * lax.rsqrt(jnp.mean(x * x, axis=-1, keepdims=True) + NORM_EPS) * g


def _head_ones(width):
    return ((_iota((width, width), 0) // HEAD) == (_iota((width, width), 1) // HEAD)).astype(BF16)


def _norm_proj_kernel(x_ref, g_ref, w_ref, o_ref):
    h = _rms_rows(x_ref[...], g_ref[...]).astype(BF16)
    o_ref[...] = jnp.dot(h, w_ref[...], preferred_element_type=F32)


def _norm_proj(x2d, g, w_bf16, tm):
    n, d = x2d.shape
    cols = w_bf16.shape[1]
    return pl.pallas_call(
        _norm_proj_kernel,
        grid=(n // tm,),
        in_specs=[pl.BlockSpec((tm, d), lambda i: (i, 0)),
                  pl.BlockSpec((1, d), lambda i: (0, 0)),
                  pl.BlockSpec((d, cols), lambda i: (0, 0), pipeline_mode=pl.Buffered(1))],
        out_specs=pl.BlockSpec((tm, cols), lambda i: (i, 0)),
        out_shape=jax.ShapeDtypeStruct((n, cols), F32),
        compiler_params=pltpu.CompilerParams(dimension_semantics=("arbitrary",), vmem_limit_bytes=VMEM_LIMIT),
        name="norm_proj",
    )(x2d, g.reshape(1, d), w_bf16)


_OPS_MM = ("at", "rt", "bt", "kt", "v", "bg", "kg")
_OPS_T = ("bt", "kt")
_OPS_OUT = ("bonus", "gate")


def _proj_prep_kernel(x_ref, g_ref, w_ref, sp_ref, mu_ref, w0_ref, w2_ref, a0_ref, a2_ref, kk_ref, ka_ref, rk_ref,
                      at_ref, rt_ref, bt_ref, kt_ref, v_ref, bg_ref, kg_ref, bonus_ref, gate_ref, gt_ref, sho_ref,
                      carry_ref, *, chunk):
    j = pl.program_id(1)
    tm = x_ref.shape[0]
    E = w0_ref.shape[1]
    SHIFT = 3 * E + 2 * LORA
    C = chunk
    n_ch = tm // C
    outs = dict(zip(_OPS_MM + _OPS_OUT,
                    (at_ref, rt_ref, bt_ref, kt_ref, v_ref, bg_ref, kg_ref, bonus_ref, gate_ref)))

    @pl.when(j == 0)
    def _init():
        carry_ref[...] = sp_ref[0]

    h = _rms_rows(x_ref[...], g_ref[...]).astype(BF16)
    first_row = _iota((tm, 1), 0) == 0
    ones_bd = _head_ones(QW)
    row = _iota((tm, tm), 0)
    col = _iota((tm, tm), 1)
    tri = ((row >= col) & (row // C == col // C)).astype(BF16)

    def proj(lo, hi):
        return jnp.dot(h, w_ref[:, lo:hi], preferred_element_type=F32)

    def shifted(lo, hi):
        sh = proj(lo, hi)
        prev = jnp.where(first_row, carry_ref[:, lo:hi], pltpu.roll(sh, 1, 0))
        carry_ref[:, lo:hi] = sh[tm - 1:tm]
        return sh + (prev - sh) * mu_ref[:, lo:hi]

    def per_chunk_last(x):
        return jnp.concatenate(
            [jnp.broadcast_to(x[c * C + C - 1:c * C + C], (C, x.shape[1])) for c in range(n_ch)], axis=0)

    x_lora = shifted(3 * E, SHIFT)
    lane_l = _iota((tm, 2 * LORA), 1)
    t_lora = jnp.where(lane_l < LORA, jnp.tanh(x_lora), x_lora)
    wf = w0_ref[...] + _mm(t_lora, w2_ref[...])
    af = a0_ref[...] + _mm(x_lora, a2_ref[...])

    for q in range(E // QW):
        lo = q * QW
        sl = slice(lo, lo + QW)
        r = shifted(lo, lo + QW)
        k = shifted(E + lo, E + lo + QW)
        v = shifted(2 * E + lo, 2 * E + lo + QW)
        z = proj(SHIFT + lo, SHIFT + lo + QW)
        ld = -math.exp(-0.5) * _sigmoid(wf[:, sl])
        a = _sigmoid(af[:, sl])
        kk0 = k * kk_ref[:, sl]
        k2 = k * (1.0 + (a - 1.0) * ka_ref[:, sl])
        sums = _mm(jnp.concatenate([kk0 * kk0, r * k2 * rk_ref[:, sl]], axis=0), ones_bd)
        kk = kk0 * jnp.minimum(lax.rsqrt(sums[:tm]), 1e12)
        cum = _mm_sum_lhs(tri, ld)
        cum_last = per_chunk_last(cum)
        g_rem = jnp.exp(cum_last - cum)
        g_inv = jnp.exp(-cum)
        kka = kk * a
        quad = dict(at=-kk * jnp.exp(cum - ld), rt=r * jnp.exp(cum), bt=kka * g_inv, kt=k2 * g_inv, v=v,
                    bg=kka * g_rem, kg=k2 * g_rem, bonus=sums[tm:] * v, gate=z * _sigmoid(z))
        for n, x in quad.items():
            if n in _OPS_T:
                xt = x.astype(BF16).T
                tt = outs[n].shape[2]
                for i in range(tm // tt):
                    outs[n][i, sl, :] = xt[:, i * tt:(i + 1) * tt]
            else:
                outs[n][:, sl] = x.astype(outs[n].dtype)
        for c in range(n_ch):
            gt_ref[c, :, sl] = jnp.exp(cum[c * C + C - 1:c * C + C])

    @pl.when(j == pl.num_programs(1) - 1)
    def _shift_out():
        sho_ref[0] = carry_ref[...]


def _proj_prep(x3d, g, w_bf16, shift_prev, prm, *, chunk, tm, t_tile):
    b, t, d = x3d.shape
    e = prm["w0"].shape[1]
    shift = 3 * e + 2 * LORA
    cols = w_bf16.shape[1]
    steps = t // tm
    n_ch = tm // chunk
    c2 = lambda shape: pl.BlockSpec(shape, lambda i, j: (0, 0))
    rows = pl.BlockSpec((tm, e), lambda i, j: (i * steps + j, 0))
    cols_t = pl.BlockSpec((tm // t_tile, e, t_tile), lambda i, j: (i * steps + j, 0, 0))
    outs = pl.pallas_call(
        functools.partial(_proj_prep_kernel, chunk=chunk),
        grid=(b, steps),
        in_specs=[pl.BlockSpec((tm, d), lambda i, j: (i * steps + j, 0)),
                  c2((1, d)),
                  pl.BlockSpec((d, cols), lambda i, j: (0, 0), pipeline_mode=pl.Buffered(1)),
                  pl.BlockSpec((1, 1, shift), lambda i, j: (i, 0, 0)),
                  c2((1, shift)), c2((1, e)), c2((2 * LORA, e)), c2((1, e)), c2((2 * LORA, e)),
                  c2((1, e)), c2((1, e)), c2((1, e))],
        out_specs=([cols_t if n in _OPS_T else rows for n in _OPS_MM + _OPS_OUT]
                   + [pl.BlockSpec((n_ch, 1, e), lambda i, j: (i * steps + j, 0, 0)),
                      pl.BlockSpec((1, 1, shift), lambda i, j: (i, 0, 0))]),
        out_shape=([jax.ShapeDtypeStruct((b * t // t_tile, e, t_tile) if n in _OPS_T else (b * t, e), BF16)
                    for n in _OPS_MM + _OPS_OUT]
                   + [jax.ShapeDtypeStruct((b * t // chunk, 1, e), F32),
                      jax.ShapeDtypeStruct((b, 1, shift), F32)]),
        scratch_shapes=[pltpu.VMEM((1, shift), F32)],
        compiler_params=pltpu.CompilerParams(dimension_semantics=("arbitrary", "arbitrary"),
                                             vmem_limit_bytes=VMEM_LIMIT),
        name="proj_prep",
    )(x3d.reshape(b * t, d), g.reshape(1, d), w_bf16, shift_prev.reshape(b, 1, shift),
      prm["mu"], prm["w0"], prm["w2p"], prm["a0"], prm["a2p"], prm["kk"], prm["ka"], prm["rk"])
    ops = dict(zip(_OPS_MM + _OPS_OUT + ("g_tot",), outs[:-1]))
    return ops, outs[-1].reshape(b, shift)


def _headsum(xs, ones_bd, split):
    e = xs[0].shape[1]
    nq = e // QW
    rows = [x[:, q * QW:(q + 1) * QW] for x in xs for q in range(nq)]
    stacked = jnp.concatenate(rows, axis=0)
    s = _mm_sum_rhs(stacked, ones_bd) if split else _mm(stacked, ones_bd)
    c = xs[0].shape[0]
    outs = []
    for i in range(len(xs)):
        outs.append(jnp.concatenate([s[(i * nq + q) * c:(i * nq + q + 1) * c] for q in range(nq)], axis=1))
    return outs


def _wkv_core(ops, states, lng, lnb, *, n_levels):
    C, E = ops["at"].shape
    TL = QUAD * C
    NQ = E // QW
    mask_bd = (_iota((QW, QW), 0) // HEAD) == (_iota((QW, QW), 1) // HEAD)
    ones_bd = mask_bd.astype(BF16)
    tok_t = _iota((C, TL), 0)
    tok_j = _iota((C, TL), 1) % C
    strict = tok_j < tok_t
    incl = tok_j <= tok_t
    eye_all = (tok_j == tok_t).astype(F32)

    def block_diag(x, width):
        assert 2 * width == 128
        x = x.astype(BF16)
        zero = jnp.zeros((), BF16)
        half = _iota((C, 128), 1) < width
        pad = jnp.zeros((C, 128), BF16)
        rows = []
        for h in range(QUAD):
            t = h // 2
            tile = jnp.where(half if h % 2 == 0 else ~half, x[:, t * 128:(t + 1) * 128], zero)
            rows.append(jnp.concatenate([tile if i == t else pad for i in range(QUAD // 2)], axis=1))
        return jnp.concatenate(rows, axis=0)

    def cols_bd(xt):
        zero = jnp.zeros((HEAD, C), xt.dtype)
        return jnp.concatenate(
            [jnp.concatenate([xt[h * HEAD:(h + 1) * HEAD] if i == h else zero for i in range(QUAD)], axis=1)
             for h in range(QUAD)], axis=0)

    stack_hc = functools.partial(block_diag, width=HEAD)
    stack_tt = functools.partial(block_diag, width=C)

    at, rt, bt, kt, v, bg, kg = (ops[n] for n in _OPS_MM)
    quads = range(NQ)
    sls = [slice(q * QW, (q + 1) * QW) for q in quads]
    ars = [jnp.concatenate([at[:, sl], rt[:, sl]], axis=0) for sl in sls]
    v_sts = [stack_hc(v[:, sl]) for sl in sls]
    abs_ = [_mm(ars[q], cols_bd(bt[sls[q], :])) for q in quads]
    aks = [_mm(ars[q], cols_bd(kt[sls[q], :])) for q in quads]
    pws = [jnp.where(strict, abs_[q][:C], 0.0) for q in quads]
    a_rbs = [jnp.where(incl, abs_[q][C:], 0.0) for q in quads]
    a_aks = [jnp.where(strict, aks[q][:C], 0.0) for q in quads]
    a_rks = [jnp.where(incl, aks[q][C:], 0.0) for q in quads]
    invs = [eye_all + pws[q] for q in quads]
    pws = [_mm(pws[q], stack_tt(pws[q])) for q in quads]
    for _ in range(n_levels - 1):
        bds = [stack_tt(pws[q]) for q in quads]
        res = [_mm(jnp.concatenate([pws[q], invs[q]], axis=0), bds[q]) for q in quads]
        pws = [res[q][:C] for q in quads]
        invs = [invs[q] + res[q][C:] for q in quads]
    invs = [invs[q] + _mm(invs[q], stack_tt(pws[q])) for q in quads]
    arss = [_mm(ars[q], states[q], _NT) for q in quads]
    avs = [_mm(jnp.concatenate([a_aks[q], a_rks[q]], axis=0), v_sts[q]) for q in quads]
    ws = [arss[q][:C] + avs[q][:C] for q in quads]
    us = [_mm(invs[q], stack_hc(ws[q])) for q in quads]
    ys = [arss[q][C:] + avs[q][C:] + _mm(a_rbs[q], stack_hc(us[q])) for q in quads]
    new_states = []
    for q in quads:
        uv = jnp.concatenate([us[q].astype(BF16), v[:, sls[q]]], axis=0)
        bk = jnp.concatenate([bg[:, sls[q]], kg[:, sls[q]]], axis=0)
        new_states.append(states[q] * ops["g_tot"][:, sls[q]] + jnp.where(mask_bd, _mm(uv, bk, _TN), 0.0))
    y = jnp.concatenate(ys, axis=1)

    (ysum,) = _headsum([y], ones_bd, split=True)
    yc = y - ysum * (1.0 / HEAD)
    (vsum,) = _headsum([yc * yc], ones_bd, split=False)
    yn = yc * lax.rsqrt(vsum * (1.0 / HEAD) + GN_EPS) * lng + lnb
    return (yn + ops["bonus"]) * ops["gate"], new_states


def _wkv_chunks_kernel(at_ref, rt_ref, bt_ref, kt_ref, v_ref, bg_ref, kg_ref, bonus_ref, gate_ref, gt_ref, s0_ref,
                       lng_ref, lnb_ref, o_ref, so_ref, st_ref, *, chunk, n_levels):
    j = pl.program_id(1)
    n_seq = at_ref.shape[1]
    NQ = st_ref.shape[0] // n_seq
    E = NQ * QW
    seqs = range(n_seq)
    refs = dict(zip(_OPS_MM + _OPS_OUT,
                    (at_ref, rt_ref, bt_ref, kt_ref, v_ref, bg_ref, kg_ref, bonus_ref, gate_ref)))

    @pl.when(j == 0)
    def _init():
        zero_blk = jnp.zeros((HEAD, HEAD), F32)
        for s in seqs:
            for q in range(NQ):
                st_ref[s * NQ + q] = jnp.concatenate(
                    [jnp.concatenate([s0_ref[0, s, QUAD * q + h] if i == h else zero_blk for i in range(QUAD)],
                                     axis=1) for h in range(QUAD)], axis=0)

    lng = jnp.concatenate([lng_ref[...]] * n_seq, axis=1)
    lnb = jnp.concatenate([lnb_ref[...]] * n_seq, axis=1)
    states = [st_ref[q] for q in range(n_seq * NQ)]
    for c in range(at_ref.shape[2] // chunk):
        rows = slice(c * chunk, (c + 1) * chunk)
        ops = {}
        for n, ref in refs.items():
            if n in _OPS_T:
                ops[n] = jnp.concatenate([ref[0, s, 0, :, rows] for s in seqs], axis=0)
            else:
                ops[n] = jnp.concatenate([ref[0, s, rows, :] for s in seqs], axis=1)
        ops["g_tot"] = jnp.concatenate([gt_ref[0, s, c] for s in seqs], axis=1)
        o, states = _wkv_core(ops, states, lng, lnb, n_levels=n_levels)
        for s in seqs:
            o_ref[0, s, rows, :] = o[:, s * E:(s + 1) * E].astype(o_ref.dtype)
    for q in range(n_seq * NQ):
        st_ref[q] = states[q]

    @pl.when(j == pl.num_programs(1) - 1)
    def _state_out():
        for s in seqs:
            for q in range(NQ):
                for h in range(QUAD):
                    so_ref[0, s, QUAD * q + h] = states[s * NQ + q][h * HEAD:(h + 1) * HEAD, h * HEAD:(h + 1) * HEAD]


def _wkv_chunks(ops, s0, lng, lnb, *, seq_len, chunk, chunks_per_step, n_seq):
    n, e = ops["at"].shape
    b = n // seq_len
    nh = e // HEAD
    tm = chunk * chunks_per_step
    steps = seq_len // tm
    assert ops["bt"].shape[2] == tm
    assert b % n_seq == 0
    g = b // n_seq
    rows = pl.BlockSpec((1, n_seq, tm, e), lambda i, j: (i, 0, j, 0))
    cols_t = pl.BlockSpec((1, n_seq, 1, e, tm), lambda i, j: (i, 0, j, 0, 0))
    c2 = lambda shape: pl.BlockSpec(shape, lambda i, j: (0, 0))
    sblk = pl.BlockSpec((1, n_seq, nh, HEAD, HEAD), lambda i, j: (i, 0, 0, 0, 0))
    n_levels = max(1, math.ceil(math.log2(chunk)) - 1)
    args = [ops[n_].reshape((g, n_seq, steps, e, tm) if n_ in _OPS_T else (g, n_seq, seq_len, e))
            for n_ in _OPS_MM + _OPS_OUT]
    o, so = pl.pallas_call(
        functools.partial(_wkv_chunks_kernel, chunk=chunk, n_levels=n_levels),
        grid=(g, steps),
        in_specs=([cols_t if n_ in _OPS_T else rows for n_ in _OPS_MM + _OPS_OUT]
                  + [pl.BlockSpec((1, n_seq, chunks_per_step, 1, e), lambda i, j: (i, 0, j, 0, 0)),
                     sblk, c2((1, e)), c2((1, e))]),
        out_specs=[rows, sblk],
        out_shape=[jax.ShapeDtypeStruct((g, n_seq, seq_len, e), BF16),
                   jax.ShapeDtypeStruct((g, n_seq, nh, HEAD, HEAD), F32)],
        scratch_shapes=[pltpu.VMEM((n_seq * (e // QW), QW, QW), F32)],
        compiler_params=pltpu.CompilerParams(dimension_semantics=("arbitrary", "arbitrary"),
                                             vmem_limit_bytes=VMEM_LIMIT),
        name="wkv_chunks",
    )(*args, ops["g_tot"].reshape(g, n_seq, seq_len // chunk, 1, e), s0.reshape(g, n_seq, nh, HEAD, HEAD), lng, lnb)
    return o.reshape(n, e), so.reshape(b, nh, HEAD, HEAD)


_VT = ("nkk", "w", "b", "k", "r", "v")


def _wkv_seq_kernel(pr_ref, pk_ref, pv_ref, pl_ref, pz_ref, sr_ref, sk_ref, sv_ref, sl_ref,
                    mur_ref, muk_ref, muv_ref, mul_ref, w0_ref, w2_ref, a0_ref, a2_ref, kk_ref, ka_ref, rk_ref,
                    lng_ref, lnb_ref, s_ref, o_ref, so_ref, vt_ref, yt_ref):
    T, B, _ = pr_ref.shape
    ones2 = _head_ones(HPAIR)

    def shifted(p_ref, prev_ref, mu_ref):
        sh = p_ref[...].reshape(T * B, HPAIR)
        prev = jnp.concatenate([prev_ref[...], sh[:(T - 1) * B]], axis=0)
        return sh + (prev - sh) * mu_ref[...]

    r = shifted(pr_ref, sr_ref, mur_ref)
    k = shifted(pk_ref, sk_ref, muk_ref)
    v = shifted(pv_ref, sv_ref, muv_ref)
    x_lora = shifted(pl_ref, sl_ref, mul_ref)
    z = pz_ref[...].reshape(T * B, HPAIR)
    lane_l = _iota((T * B, 2 * LORA), 1)
    t_lora = jnp.where(lane_l < LORA, jnp.tanh(x_lora), x_lora)
    wf = w0_ref[...] + _mm(t_lora, w2_ref[...])
    af = a0_ref[...] + _mm(x_lora, a2_ref[...])
    w = jnp.exp(-math.exp(-0.5) * _sigmoid(wf))
    a = _sigmoid(af)
    kk0 = k * kk_ref[...]
    k2 = k * (1.0 + (a - 1.0) * ka_ref[...])
    sums = _mm_sum_rhs(jnp.concatenate([kk0 * kk0, r * k2 * rk_ref[...]], axis=0), ones2)
    kk = kk0 * jnp.minimum(lax.rsqrt(sums[:T * B]), 1e12)
    bonus = sums[T * B:] * v
    vecs = dict(nkk=-kk, w=w, b=kk * a, k=k2, r=r, v=v)
    for i, n in enumerate(_VT):
        for t in range(T):
            vt_ref[i, t] = vecs[n][t * B:(t + 1) * B].T

    for h in range(2):
        ch = slice(h * HEAD, (h + 1) * HEAD)

        def body(g, carry, h=h, ch=ch):
            v0 = pl.multiple_of(g * 8, 8)
            vrows = [vt_ref[5, t, pl.ds(h * HEAD + v0, 8), :] for t in range(T)]
            ys = [[] for _ in range(T)]
            for u in range(8):
                s = s_ref[h, v0 + u]
                for t in range(T):
                    sa = jnp.sum(s * vt_ref[0, t, ch, :], axis=0, keepdims=True)
                    s = s * vt_ref[1, t, ch, :] + sa * vt_ref[2, t, ch, :] + vrows[t][u:u + 1] * vt_ref[3, t, ch, :]
                    ys[t].append(jnp.sum(s * vt_ref[4, t, ch, :], axis=0, keepdims=True))
                so_ref[h, v0 + u] = s
            for t in range(T):
                yt_ref[t, pl.ds(h * HEAD + v0, 8), :] = jnp.concatenate(ys[t], axis=0)
            return carry

        lax.fori_loop(0, HEAD // 8, body, 0)

    y = jnp.concatenate([yt_ref[t].T for t in range(T)], axis=0)
    yc = y - _mm_sum_rhs(y, ones2) * (1.0 / HEAD)
    var = _mm_sum_rhs(yc * yc, ones2) * (1.0 / HEAD)
    yn = yc * lax.rsqrt(var + GN_EPS) * lng_ref[...] + lnb_ref[...]
    o_ref[...] = ((yn + bonus) * (z * _sigmoid(z))).reshape(T, B, HPAIR)


def _wkv_seq(proj_t, shift_prev, s_t, prm):
    t, b, cols = proj_t.shape
    e = prm["w0"].shape[1]
    nh = e // HEAD
    kb = e // HPAIR
    bt = 128
    off = dict(r=0, k=kb, v=2 * kb, l=3 * kb)
    pspec = lambda o: pl.BlockSpec((t, bt, HPAIR), lambda i, j, o=o: (0, j, o + i))
    pfix = lambda o: pl.BlockSpec((t, bt, HPAIR), lambda i, j, o=o: (0, j, o))
    sspec = lambda o: pl.BlockSpec((bt, HPAIR), lambda i, j, o=o: (j, o + i))
    sfix = lambda o: pl.BlockSpec((bt, HPAIR), lambda i, j, o=o: (j, o))
    mspec = lambda o: pl.BlockSpec((1, HPAIR), lambda i, j, o=o: (0, o + i))
    mfix = lambda o: pl.BlockSpec((1, HPAIR), lambda i, j, o=o: (0, o))
    row = pl.BlockSpec((1, HPAIR), lambda i, j: (0, i))
    lora = pl.BlockSpec((2 * LORA, HPAIR), lambda i, j: (0, i))
    sblk = pl.BlockSpec((2, HEAD, HEAD, bt), lambda i, j: (i, 0, 0, j))
    return pl.pallas_call(
        _wkv_seq_kernel,
        grid=(nh // 2, b // bt),
        in_specs=[pspec(off["r"]), pspec(off["k"]), pspec(off["v"]), pfix(off["l"]), pspec(off["l"] + 1),
                  sspec(off["r"]), sspec(off["k"]), sspec(off["v"]), sfix(off["l"]),
                  mspec(off["r"]), mspec(off["k"]), mspec(off["v"]), mfix(off["l"]),
                  row, lora, row, lora, row, row, row, row, row, sblk],
        out_specs=[pl.BlockSpec((t, bt, HPAIR), lambda i, j: (0, j, i)), sblk],
        out_shape=[jax.ShapeDtypeStruct((t, b, e), F32), jax.ShapeDtypeStruct(s_t.shape, F32)],
        scratch_shapes=[pltpu.VMEM((len(_VT), t, HPAIR, bt), F32),
                        pltpu.VMEM((t, HPAIR, bt), F32)],
        compiler_params=pltpu.CompilerParams(dimension_semantics=("arbitrary", "arbitrary"),
                                             vmem_limit_bytes=VMEM_LIMIT),
        name="wkv_seq",
    )(proj_t, proj_t, proj_t, proj_t, proj_t, shift_prev, shift_prev, shift_prev, shift_prev,
      prm["mu"], prm["mu"], prm["mu"], prm["mu"], prm["w0"], prm["w2p"], prm["a0"], prm["a2p"],
      prm["kk"], prm["ka"], prm["rk"], prm["lng"], prm["lnb"], s_t)


def _out_residual_kernel(o_ref, w_ref, x_ref, y_ref):
    y_ref[...] = x_ref[...] + jnp.dot(o_ref[...].astype(BF16), w_ref[...], preferred_element_type=F32)


def _out_residual(o2d, w_bf16, x2d, tm):
    n, e = o2d.shape
    d = x2d.shape[1]
    return pl.pallas_call(
        _out_residual_kernel,
        grid=(n // tm,),
        in_specs=[pl.BlockSpec((tm, e), lambda i: (i, 0)),
                  pl.BlockSpec((e, d), lambda i: (0, 0)),
                  pl.BlockSpec((tm, d), lambda i: (i, 0))],
        out_specs=pl.BlockSpec((tm, d), lambda i: (i, 0)),
        out_shape=jax.ShapeDtypeStruct((n, d), F32),
        compiler_params=pltpu.CompilerParams(dimension_semantics=("arbitrary",), vmem_limit_bytes=VMEM_LIMIT),
        name="out_residual",
    )(o2d, w_bf16, x2d)


def _gmlp_kernel(x_ref, g_ref, win_ref, vg_ref, vb_ref, wm_ref, bs_ref, wout_ref, nf_ref, *out_refs, emit_v):
    y_ref = out_refs[0]
    tm = x_ref.shape[0]
    e = vg_ref.shape[1]
    gd = e // GM_GROUPS
    x = x_ref[...]
    h = _rms_rows(x, g_ref[...]).astype(BF16)
    proj = jnp.dot(h, win_ref[...], preferred_element_type=F32)
    u = jax.nn.gelu(proj[:, :e])
    vf = jax.nn.gelu(proj[:, e:2 * e])
    z = proj[:, 2 * e:]
    vm = jnp.mean(vf, axis=-1, keepdims=True)
    vc = vf - vm
    vv = jnp.mean(vc * vc, axis=-1, keepdims=True)
    vn = vc * lax.rsqrt(vv + LN_EPS) * vg_ref[...] + vb_ref[...]
    if emit_v:
        out_refs[1][...] = vn
    vn16 = vn.astype(BF16)
    causal = _iota((GM_CHUNK, GM_CHUNK), 0) >= _iota((GM_CHUNK, GM_CHUNK), 1)
    rows = []
    for j in range(tm // GM_CHUNK):
        cols = []
        for gi in range(GM_GROUPS):
            wm = jnp.where(causal, wm_ref[gi], 0.0).astype(BF16)
            blk = vn16[j * GM_CHUNK:(j + 1) * GM_CHUNK, gi * gd:(gi + 1) * gd]
            cols.append(jnp.dot(wm, blk, preferred_element_type=F32) + bs_ref[gi])
        rows.append(jnp.concatenate(cols, axis=1))
    mixed = jnp.concatenate(rows, axis=0)
    o = u * mixed * (z * _sigmoid(z))
    x2 = x + jnp.dot(o.astype(BF16), wout_ref[...], preferred_element_type=F32)
    y_ref[...] = _rms_rows(x2, nf_ref[...])


def _gmlp(x2d, g, win_bf16, vg, vb, wmix, bias, wout_bf16, nf, *, tm, emit_v):
    n, d = x2d.shape
    e = vg.shape[0]
    const2 = lambda shape: pl.BlockSpec(shape, lambda i: (0, 0))
    out_shape = [jax.ShapeDtypeStruct((n, d), F32)]
    out_specs = [pl.BlockSpec((tm, d), lambda i: (i, 0))]
    if emit_v:
        out_shape.append(jax.ShapeDtypeStruct((n, e), F32))
        out_specs.append(pl.BlockSpec((tm, e), lambda i: (i, 0)))
    outs = pl.pallas_call(
        functools.partial(_gmlp_kernel, emit_v=emit_v),
        grid=(n // tm,),
        in_specs=[pl.BlockSpec((tm, d), lambda i: (i, 0)),
                  const2((1, d)),
                  pl.BlockSpec((d, 3 * e), lambda i: (0, 0), pipeline_mode=pl.Buffered(1)),
                  const2((1, e)), const2((1, e)),
                  pl.BlockSpec((GM_GROUPS, GM_CHUNK, GM_CHUNK), lambda i: (0, 0, 0)),
                  pl.BlockSpec((GM_GROUPS, GM_CHUNK, 1), lambda i: (0, 0, 0)),
                  pl.BlockSpec((e, d), lambda i: (0, 0), pipeline_mode=pl.Buffered(1)),
                  const2((1, d))],
        out_specs=out_specs,
        out_shape=out_shape,
        compiler_params=pltpu.CompilerParams(dimension_semantics=("arbitrary",), vmem_limit_bytes=VMEM_LIMIT),
        name="gmlp",
    )(x2d, g.reshape(1, d), win_bf16, vg.reshape(1, e), vb.reshape(1, e), wmix,
      bias.reshape(GM_GROUPS, GM_CHUNK, 1), wout_bf16, nf.reshape(1, d))
    return outs


def _row_tile(n, pref):
    t = pref
    while n % t:
        t //= 2
    return t


def kernel(x_prompt, x_sample, state_shift, state_wkv, norm_g, norm_f, rw_in, rw_mu, rw_w0, rw_w2, rw_a0, rw_a2, rw_kk, rw_ka, rw_rk, rw_lnx_g, rw_lnx_b, rw_out, gm_in, gm_vg, gm_vb, gm_ws, gm_bs, gm_out):
    bp, tp, d = x_prompt.shape
    bs_, ts, _ = x_sample.shape
    e = rw_w0.shape[1]
    nh = e // HEAD
    shift = 3 * e + 2 * LORA
    xp = x_prompt.reshape(bp * tp, d)

    zeros_l = jnp.zeros((LORA, e), F32)
    prm = dict(
        mu=rw_mu[0].reshape(1, shift), w0=rw_w0[0].reshape(1, e), a0=rw_a0[0].reshape(1, e),
        w2p=jnp.concatenate([rw_w2[0], zeros_l], axis=0).astype(BF16),
        a2p=jnp.concatenate([zeros_l, rw_a2[0]], axis=0).astype(BF16),
        kk=rw_kk[0].reshape(1, e), ka=rw_ka[0].reshape(1, e), rk=rw_rk[0].reshape(1, e),
        lng=rw_lnx_g[0].reshape(1, e), lnb=rw_lnx_b[0].reshape(1, e))
    w_in = rw_in[0].astype(BF16)
    w_out = rw_out[0].astype(BF16)

    ops_p, shift_p = _proj_prep(x_prompt, norm_g[0], w_in, jnp.zeros((bp, shift), F32), prm,
                                chunk=WKV_CHUNK, tm=_row_tile(tp, 4 * WKV_CHUNK), t_tile=2 * WKV_CHUNK)
    o_p, wkv_p = _wkv_chunks(ops_p, jnp.zeros((bp, nh, HEAD, HEAD), F32), prm["lng"], prm["lnb"],
                             seq_len=tp, chunk=WKV_CHUNK, chunks_per_step=2, n_seq=2 if bp % 2 == 0 else 1)
    x1p = _out_residual(o_p, w_out, xp, _row_tile(bp * tp, 512))

    xs_t = jnp.transpose(x_sample, (1, 0, 2)).reshape(ts * bs_, d)
    proj_s = _norm_proj(xs_t, norm_g[0], w_in, _row_tile(bs_ * ts, 256)).reshape(ts, bs_, shift + e)
    o_s, wkv_s_t = _wkv_seq(proj_s, state_shift[0], jnp.transpose(state_wkv[0], (1, 2, 3, 0)), prm)
    wkv_s = jnp.transpose(wkv_s_t, (3, 0, 1, 2))
    shift_s = proj_s[ts - 1, :, :shift]
    x1s_t = _out_residual(o_s.reshape(ts * bs_, e), w_out, xs_t, _row_tile(bs_ * ts, 512))
    x1s = jnp.transpose(x1s_t.reshape(ts, bs_, d), (1, 0, 2)).reshape(bs_ * ts, d)

    g_in = gm_in[0].astype(BF16)
    g_out = gm_out[0].astype(BF16)
    wm_p = gm_ws[0]
    reps = GM_CHUNK // ts
    eye = jnp.eye(reps, dtype=F32)
    wm_s = jax.vmap(lambda w: jnp.kron(eye, w))(wm_p[:, :ts, :ts])
    bias_s = jnp.tile(gm_bs[0][:, :ts], (1, reps))

    (y_p,) = _gmlp(x1p, norm_g[1], g_in, gm_vg[0], gm_vb[0], wm_p, gm_bs[0], g_out, norm_f,
                   tm=_row_tile(bp * tp, 512), emit_v=False)
    y_s, v_s = _gmlp(x1s, norm_g[1], g_in, gm_vg[0], gm_vb[0], wm_s, bias_s, g_out, norm_f,
                     tm=_row_tile(bs_ * ts, 256), emit_v=True)

    return (y_p.reshape(bp, tp, d), y_s.reshape(bs_, ts, d),
            shift_p[None], wkv_p[None], shift_s[None], wkv_s[None],
            v_s.reshape(1, bs_, ts, e))
```

```python
import functools
import math

import jax
import jax.numpy as jnp
from jax import lax
from jax.experimental import pallas as pl
from jax.experimental.pallas import tpu as pltpu

F32 = jnp.float32
BF16 = jnp.bfloat16

LANES = 128
HEAD = 64
QUAD = 4
QW = QUAD * HEAD
HPAIR = 2 * HEAD
assert HPAIR == LANES
LORA = 64
NORM_EPS = 1e-6
LN_EPS = 1e-5
GN_EPS = 64e-5
GM_CHUNK = 128
GM_GROUPS = 8
VMEM_LIMIT = 56 * 1024 * 1024

WKV_CHUNK = 64
WKV_STEP_CHUNKS = 2
WKV_STEP_SEQS = 2
PREP_ROWS = 4 * WKV_CHUNK
PROJ_ROWS = 256
OUT_ROWS = 1024
GMLP_ROWS = 512


def _split2(x):
    hi = x.astype(BF16)
    lo = (x - hi.astype(F32)).astype(BF16)
    return hi, lo


_NN = (((1,), (0,)), ((), ()))
_NT = (((1,), (1,)), ((), ()))
_TN = (((0,), (0,)), ((), ()))


def _mm(a, b, dims=_NN):
    return lax.dot_general(a.astype(BF16), b.astype(BF16), dims, preferred_element_type=F32)


def _mm_sum_rhs(a, b_bf16, dims=_NN):
    d = functools.partial(lax.dot_general, dimension_numbers=dims, preferred_element_type=F32)
    h, l = _split2(a)
    return d(h, b_bf16) + d(l, b_bf16)


def _mm_sum_lhs(a_bf16, b, dims=_NN):
    d = functools.partial(lax.dot_general, dimension_numbers=dims, preferred_element_type=F32)
    h, l = _split2(b)
    return d(a_bf16, h) + d(a_bf16, l)


def _iota(shape, dim):
    return lax.broadcasted_iota(jnp.int32, shape, dim)


def _sigmoid(x):
    return 0.5 * jnp.tanh(0.5 * x) + 0.5


def _rms_rows(x, g):
    return x * lax.rsqrt(jnp.mean(x * x, axis=-1, keepdims=True) + NORM_EPS) * g


def _head_ones(width):
    return ((_iota((width, width), 0) // HEAD) == (_iota((width, width), 1) // HEAD)).astype(BF16)


def _norm_proj_kernel(x_ref, g_ref, w_ref, o_ref):
    h = _rms_rows(x_ref[...], g_ref[...]).astype(BF16)
    o_ref[...] = jnp.dot(h, w_ref[...], preferred_element_type=F32)


def _norm_proj(x2d, g, w_bf16, tm):
    n, d = x2d.shape
    cols = w_bf16.shape[1]
    return pl.pallas_call(
        _norm_proj_kernel,
        grid=(n // tm,),
        in_specs=[pl.BlockSpec((tm, d), lambda i: (i, 0)),
                  pl.BlockSpec((1, d), lambda i: (0, 0)),
                  pl.BlockSpec((d, cols), lambda i: (0, 0), pipeline_mode=pl.Buffered(1))],
        out_specs=pl.BlockSpec((tm, cols), lambda i: (i, 0)),
        out_shape=jax.ShapeDtypeStruct((n, cols), F32),
        compiler_params=pltpu.CompilerParams(dimension_semantics=("arbitrary",), vmem_limit_bytes=VMEM_LIMIT),
        name="norm_proj",
    )(x2d, g.reshape(1, d), w_bf16)


_OPS_MM = ("at", "rt", "bt", "kt", "v", "bg", "kg")
_OPS_T = ("bt", "kt")
_OPS_OUT = ("bonus", "gate")


def _proj_prep_kernel(x_ref, g_ref, w_ref, sp_ref, mu_ref, w0_ref, w2_ref, a0_ref, a2_ref, kk_ref, ka_ref, rk_ref,
                      at_ref, rt_ref, bt_ref, kt_ref, v_ref, bg_ref, kg_ref, bonus_ref, gate_ref, gt_ref, sho_ref,
                      carry_ref, *, chunk):
    j = pl.program_id(1)
    tm = x_ref.shape[0]
    E = w0_ref.shape[1]
    SHIFT = 3 * E + 2 * LORA
    C = chunk
    n_ch = tm // C
    outs = dict(zip(_OPS_MM + _OPS_OUT,
                    (at_ref, rt_ref, bt_ref, kt_ref, v_ref, bg_ref, kg_ref, bonus_ref, gate_ref)))

    @pl.when(j == 0)
    def _init():
        carry_ref[...] = sp_ref[0]

    h = _rms_rows(x_ref[...], g_ref[...]).astype(BF16)
    first_row = _iota((tm, 1), 0) == 0
    ones_bd = _head_ones(QW)
    row = _iota((tm, tm), 0)
    col = _iota((tm, tm), 1)
    tri = ((row >= col) & (row // C == col // C)).astype(BF16)

    def proj(lo, hi):
        return jnp.dot(h, w_ref[:, lo:hi], preferred_element_type=F32)

    def shifted(lo, hi):
        sh = proj(lo, hi)
        prev = jnp.where(first_row, carry_ref[:, lo:hi], pltpu.roll(sh, 1, 0))
        carry_ref[:, lo:hi] = sh[tm - 1:tm]
        return sh + (prev - sh) * mu_ref[:, lo:hi]

    def per_chunk_last(x):
        return jnp.concatenate(
            [jnp.broadcast_to(x[c * C + C - 1:c * C + C], (C, x.shape[1])) for c in range(n_ch)], axis=0)

    x_lora = shifted(3 * E, SHIFT)
    lane_l = _iota((tm, 2 * LORA), 1)
    t_lora = jnp.where(lane_l < LORA, jnp.tanh(x_lora), x_lora)
    wf = w0_ref[...] + _mm(t_lora, w2_ref[...])
    af = a0_ref[...] + _mm(x_lora, a2_ref[...])

    for q in range(E // QW):
        lo = q * QW
        sl = slice(lo, lo + QW)
        r = shifted(lo, lo + QW)
        k = shifted(E + lo, E + lo + QW)
        v = shifted(2 * E + lo, 2 * E + lo + QW)
        z = proj(SHIFT + lo, SHIFT + lo + QW)
        ld = -math.exp(-0.5) * _sigmoid(wf[:, sl])
        a = _sigmoid(af[:, sl])
        kk0 = k * kk_ref[:, sl]
        k2 = k * (1.0 + (a - 1.0) * ka_ref[:, sl])
        sums = _mm(jnp.concatenate([kk0 * kk0, r * k2 * rk_ref[:, sl]], axis=0), ones_bd)
        kk = kk0 * jnp.minimum(lax.rsqrt(sums[:tm]), 1e12)
        cum = _mm_sum_lhs(tri, ld)
        cum_last = per_chunk_last(cum)
        g_rem = jnp.exp(cum_last - cum)
        g_inv = jnp.exp(-cum)
        kka = kk * a
        quad = dict(at=-kk * jnp.exp(cum - ld), rt=r * jnp.exp(cum), bt=kka * g_inv, kt=k2 * g_inv, v=v,
                    bg=kka * g_rem, kg=k2 * g_rem, bonus=sums[tm:] * v, gate=z * _sigmoid(z))
        for n, x in quad.items():
            if n in _OPS_T:
                xt = x.astype(BF16).T
                tt = outs[n].shape[2]
                for i in range(tm // tt):
                    outs[n][i, sl, :] = xt[:, i * tt:(i + 1) * tt]
            else:
                outs[n][:, sl] = x.astype(outs[n].dtype)
        for c in range(n_ch):
            gt_ref[c, :, sl] = jnp.exp(cum[c * C + C - 1:c * C + C])

    @pl.when(j == pl.num_programs(1) - 1)
    def _shift_out():
        sho_ref[0] = carry_ref[...]


def _proj_prep(x3d, g, w_bf16, shift_prev, prm, *, chunk, tm, t_tile):
    b, t, d = x3d.shape
    e = prm["w0"].shape[1]
    shift = 3 * e + 2 * LORA
    cols = w_bf16.shape[1]
    assert t % tm == 0 and tm % t_tile == 0 and t_tile % chunk == 0 and e % QW == 0
    steps = t // tm
    n_ch = tm // chunk
    c2 = lambda shape: pl.BlockSpec(shape, lambda i, j: (0, 0))
    rows = pl.BlockSpec((tm, e), lambda i, j: (i * steps + j, 0))
    cols_t = pl.BlockSpec((tm // t_tile, e, t_tile), lambda i, j: (i * steps + j, 0, 0))
    outs = pl.pallas_call(
        functools.partial(_proj_prep_kernel, chunk=chunk),
        grid=(b, steps),
        in_specs=[pl.BlockSpec((tm, d), lambda i, j: (i * steps + j, 0)),
                  c2((1, d)),
                  pl.BlockSpec((d, cols), lambda i, j: (0, 0), pipeline_mode=pl.Buffered(1)),
                  pl.BlockSpec((1, 1, shift), lambda i, j: (i, 0, 0)),
                  c2((1, shift)), c2((1, e)), c2((2 * LORA, e)), c2((1, e)), c2((2 * LORA, e)),
                  c2((1, e)), c2((1, e)), c2((1, e))],
        out_specs=([cols_t if n in _OPS_T else rows for n in _OPS_MM + _OPS_OUT]
                   + [pl.BlockSpec((n_ch, 1, e), lambda i, j: (i * steps + j, 0, 0)),
                      pl.BlockSpec((1, 1, shift), lambda i, j: (i, 0, 0))]),
        out_shape=([jax.ShapeDtypeStruct((b * t // t_tile, e, t_tile) if n in _OPS_T else (b * t, e), BF16)
                    for n in _OPS_MM + _OPS_OUT]
                   + [jax.ShapeDtypeStruct((b * t // chunk, 1, e), F32),
                      jax.ShapeDtypeStruct((b, 1, shift), F32)]),
        scratch_shapes=[pltpu.VMEM((1, shift), F32)],
        compiler_params=pltpu.CompilerParams(dimension_semantics=("arbitrary", "arbitrary"),
                                             vmem_limit_bytes=VMEM_LIMIT),
        name="proj_prep",
    )(x3d.reshape(b * t, d), g.reshape(1, d), w_bf16, shift_prev.reshape(b, 1, shift),
      prm["mu"], prm["w0"], prm["w2p"], prm["a0"], prm["a2p"], prm["kk"], prm["ka"], prm["rk"])
    ops = dict(zip(_OPS_MM + _OPS_OUT + ("g_tot",), outs[:-1]))
    return ops, outs[-1].reshape(b, shift)


def _headsum(xs, ones_bd, split):
    e = xs[0].shape[1]
    nq = e // QW
    rows = [x[:, q * QW:(q + 1) * QW] for x in xs for q in range(nq)]
    stacked = jnp.concatenate(rows, axis=0)
    s = _mm_sum_rhs(stacked, ones_bd) if split else _mm(stacked, ones_bd)
    c = xs[0].shape[0]
    outs = []
    for i in range(len(xs)):
        outs.append(jnp.concatenate([s[(i * nq + q) * c:(i * nq + q + 1) * c] for q in range(nq)], axis=1))
    return outs


def _wkv_core(ops, states, lng, lnb, *, n_levels):
    C, E = ops["at"].shape
    TL = QUAD * C
    NQ = E // QW
    mask_bd = (_iota((QW, QW), 0) // HEAD) == (_iota((QW, QW), 1) // HEAD)
    ones_bd = mask_bd.astype(BF16)
    tok_t = _iota((C, TL), 0)
    tok_j = _iota((C, TL), 1) % C
    strict = tok_j < tok_t
    incl = tok_j <= tok_t
    eye_all = (tok_j == tok_t).astype(F32)

    def block_diag(x, width):
        assert 2 * width == LANES
        x = x.astype(BF16)
        zero = jnp.zeros((), BF16)
        half = _iota((C, LANES), 1) < width
        pad = jnp.zeros((C, LANES), BF16)
        rows = []
        for h in range(QUAD):
            t = h // 2
            tile = jnp.where(half if h % 2 == 0 else ~half, x[:, t * LANES:(t + 1) * LANES], zero)
            rows.append(jnp.concatenate([tile if i == t else pad for i in range(QUAD // 2)], axis=1))
        return jnp.concatenate(rows, axis=0)

    def cols_bd(xt):
        zero = jnp.zeros((HEAD, C), xt.dtype)
        return jnp.concatenate(
            [jnp.concatenate([xt[h * HEAD:(h + 1) * HEAD] if i == h else zero for i in range(QUAD)], axis=1)
             for h in range(QUAD)], axis=0)

    stack_hc = functools.partial(block_diag, width=HEAD)
    stack_tt = functools.partial(block_diag, width=C)

    at, rt, bt, kt, v, bg, kg = (ops[n] for n in _OPS_MM)
    quads = range(NQ)
    sls = [slice(q * QW, (q + 1) * QW) for q in quads]
    ars = [jnp.concatenate([at[:, sl], rt[:, sl]], axis=0) for sl in sls]
    v_sts = [stack_hc(v[:, sl]) for sl in sls]
    abs_ = [_mm(ars[q], cols_bd(bt[sls[q], :])) for q in quads]
    aks = [_mm(ars[q], cols_bd(kt[sls[q], :])) for q in quads]
    pws = [jnp.where(strict, abs_[q][:C], 0.0) for q in quads]
    a_rbs = [jnp.where(incl, abs_[q][C:], 0.0) for q in quads]
    a_aks = [jnp.where(strict, aks[q][:C], 0.0) for q in quads]
    a_rks = [jnp.where(incl, aks[q][C:], 0.0) for q in quads]
    invs = [eye_all + pws[q] for q in quads]
    pws = [_mm(pws[q], stack_tt(pws[q])) for q in quads]
    for _ in range(n_levels - 1):
        bds = [stack_tt(pws[q]) for q in quads]
        res = [_mm(jnp.concatenate([pws[q], invs[q]], axis=0), bds[q]) for q in quads]
        pws = [res[q][:C] for q in quads]
        invs = [invs[q] + res[q][C:] for q in quads]
    invs = [invs[q] + _mm(invs[q], stack_tt(pws[q])) for q in quads]
    arss = [_mm(ars[q], states[q], _NT) for q in quads]
    avs = [_mm(jnp.concatenate([a_aks[q], a_rks[q]], axis=0), v_sts[q]) for q in quads]
    ws = [arss[q][:C] + avs[q][:C] for q in quads]
    us = [_mm(invs[q], stack_hc(ws[q])) for q in quads]
    ys = [arss[q][C:] + avs[q][C:] + _mm(a_rbs[q], stack_hc(us[q])) for q in quads]
    new_states = []
    for q in quads:
        uv = jnp.concatenate([us[q].astype(BF16), v[:, sls[q]]], axis=0)
        bk = jnp.concatenate([bg[:, sls[q]], kg[:, sls[q]]], axis=0)
        new_states.append(states[q] * ops["g_tot"][:, sls[q]] + jnp.where(mask_bd, _mm(uv, bk, _TN), 0.0))
    y = jnp.concatenate(ys, axis=1)

    (ysum,) = _headsum([y], ones_bd, split=True)
    yc = y - ysum * (1.0 / HEAD)
    (vsum,) = _headsum([yc * yc], ones_bd, split=False)
    yn = yc * lax.rsqrt(vsum * (1.0 / HEAD) + GN_EPS) * lng + lnb
    return (yn + ops["bonus"]) * ops["gate"], new_states


def _wkv_chunks_kernel(at_ref, rt_ref, bt_ref, kt_ref, v_ref, bg_ref, kg_ref, bonus_ref, gate_ref, gt_ref, s0_ref,
                       lng_ref, lnb_ref, o_ref, so_ref, st_ref, *, chunk, n_levels):
    j = pl.program_id(1)
    n_seq = at_ref.shape[1]
    NQ = st_ref.shape[0] // n_seq
    E = NQ * QW
    seqs = range(n_seq)
    refs = dict(zip(_OPS_MM + _OPS_OUT,
                    (at_ref, rt_ref, bt_ref, kt_ref, v_ref, bg_ref, kg_ref, bonus_ref, gate_ref)))

    @pl.when(j == 0)
    def _init():
        zero_blk = jnp.zeros((HEAD, HEAD), F32)
        for s in seqs:
            for q in range(NQ):
                st_ref[s * NQ + q] = jnp.concatenate(
                    [jnp.concatenate([s0_ref[0, s, QUAD * q + h] if i == h else zero_blk for i in range(QUAD)],
                                     axis=1) for h in range(QUAD)], axis=0)

    lng = jnp.concatenate([lng_ref[...]] * n_seq, axis=1)
    lnb = jnp.concatenate([lnb_ref[...]] * n_seq, axis=1)
    states = [st_ref[q] for q in range(n_seq * NQ)]
    for c in range(at_ref.shape[2] // chunk):
        rows = slice(c * chunk, (c + 1) * chunk)
        ops = {}
        for n, ref in refs.items():
            if n in _OPS_T:
                ops[n] = jnp.concatenate([ref[0, s, 0, :, rows] for s in seqs], axis=0)
            else:
                ops[n] = jnp.concatenate([ref[0, s, rows, :] for s in seqs], axis=1)
        ops["g_tot"] = jnp.concatenate([gt_ref[0, s, c] for s in seqs], axis=1)
        o, states = _wkv_core(ops, states, lng, lnb, n_levels=n_levels)
        for s in seqs:
            o_ref[0, s, rows, :] = o[:, s * E:(s + 1) * E].astype(o_ref.dtype)
    for q in range(n_seq * NQ):
        st_ref[q] = states[q]

    @pl.when(j == pl.num_programs(1) - 1)
    def _state_out():
        for s in seqs:
            for q in range(NQ):
                for h in range(QUAD):
                    so_ref[0, s, QUAD * q + h] = states[s * NQ + q][h * HEAD:(h + 1) * HEAD, h * HEAD:(h + 1) * HEAD]


def _wkv_chunks(ops, s0, lng, lnb, *, seq_len, chunk, chunks_per_step, n_seq):
    n, e = ops["at"].shape
    b = n // seq_len
    nh = e // HEAD
    tm = chunk * chunks_per_step
    steps = seq_len // tm
    assert ops["bt"].shape[2] == tm
    assert b % n_seq == 0
    g = b // n_seq
    rows = pl.BlockSpec((1, n_seq, tm, e), lambda i, j: (i, 0, j, 0))
    cols_t = pl.BlockSpec((1, n_seq, 1, e, tm), lambda i, j: (i, 0, j, 0, 0))
    c2 = lambda shape: pl.BlockSpec(shape, lambda i, j: (0, 0))
    sblk = pl.BlockSpec((1, n_seq, nh, HEAD, HEAD), lambda i, j: (i, 0, 0, 0, 0))
    n_levels = max(1, math.ceil(math.log2(chunk)) - 1)
    args = [ops[n_].reshape((g, n_seq, steps, e, tm) if n_ in _OPS_T else (g, n_seq, seq_len, e))
            for n_ in _OPS_MM + _OPS_OUT]
    o, so = pl.pallas_call(
        functools.partial(_wkv_chunks_kernel, chunk=chunk, n_levels=n_levels),
        grid=(g, steps),
        in_specs=([cols_t if n_ in _OPS_T else rows for n_ in _OPS_MM + _OPS_OUT]
                  + [pl.BlockSpec((1, n_seq, chunks_per_step, 1, e), lambda i, j: (i, 0, j, 0, 0)),
                     sblk, c2((1, e)), c2((1, e))]),
        out_specs=[rows, sblk],
        out_shape=[jax.ShapeDtypeStruct((g, n_seq, seq_len, e), BF16),
                   jax.ShapeDtypeStruct((g, n_seq, nh, HEAD, HEAD), F32)],
        scratch_shapes=[pltpu.VMEM((n_seq * (e // QW), QW, QW), F32)],
        compiler_params=pltpu.CompilerParams(dimension_semantics=("arbitrary", "arbitrary"),
                                             vmem_limit_bytes=VMEM_LIMIT),
        name="wkv_chunks",
    )(*args, ops["g_tot"].reshape(g, n_seq, seq_len // chunk, 1, e), s0.reshape(g, n_seq, nh, HEAD, HEAD), lng, lnb)
    return o.reshape(n, e), so.reshape(b, nh, HEAD, HEAD)


_VT = ("nkk", "w", "b", "k", "r", "v")


def _wkv_seq_kernel(pr_ref, pk_ref, pv_ref, pl_ref, pz_ref, sr_ref, sk_ref, sv_ref, sl_ref,
                    mur_ref, muk_ref, muv_ref, mul_ref, w0_ref, w2_ref, a0_ref, a2_ref, kk_ref, ka_ref, rk_ref,
                    lng_ref, lnb_ref, s_ref, o_ref, so_ref, vt_ref, yt_ref):
    T, B, _ = pr_ref.shape
    ones2 = _head_ones(HPAIR)

    def shifted(p_ref, prev_ref, mu_ref):
        sh = p_ref[...].reshape(T * B, HPAIR)
        prev = jnp.concatenate([prev_ref[...], sh[:(T - 1) * B]], axis=0)
        return sh + (prev - sh) * mu_ref[...]

    r = shifted(pr_ref, sr_ref, mur_ref)
    k = shifted(pk_ref, sk_ref, muk_ref)
    v = shifted(pv_ref, sv_ref, muv_ref)
    x_lora = shifted(pl_ref, sl_ref, mul_ref)
    z = pz_ref[...].reshape(T * B, HPAIR)
    lane_l = _iota((T * B, 2 * LORA), 1)
    t_lora = jnp.where(lane_l < LORA, jnp.tanh(x_lora), x_lora)
    wf = w0_ref[...] + _mm(t_lora, w2_ref[...])
    af = a0_ref[...] + _mm(x_lora, a2_ref[...])
    w = jnp.exp(-math.exp(-0.5) * _sigmoid(wf))
    a = _sigmoid(af)
    kk0 = k * kk_ref[...]
    k2 = k * (1.0 + (a - 1.0) * ka_ref[...])
    sums = _mm_sum_rhs(jnp.concatenate([kk0 * kk0, r * k2 * rk_ref[...]], axis=0), ones2)
    kk = kk0 * jnp.minimum(lax.rsqrt(sums[:T * B]), 1e12)
    bonus = sums[T * B:] * v
    vecs = dict(nkk=-kk, w=w, b=kk * a, k=k2, r=r, v=v)
    for i, n in enumerate(_VT):
        for t in range(T):
            vt_ref[i, t] = vecs[n][t * B:(t + 1) * B].T

    for h in range(2):
        ch = slice(h * HEAD, (h + 1) * HEAD)

        def body(g, carry, h=h, ch=ch):
            v0 = pl.multiple_of(g * 8, 8)
            vrows = [vt_ref[5, t, pl.ds(h * HEAD + v0, 8), :] for t in range(T)]
            ys = [[] for _ in range(T)]
            for u in range(8):
                s = s_ref[h, v0 + u]
                for t in range(T):
                    sa = jnp.sum(s * vt_ref[0, t, ch, :], axis=0, keepdims=True)
                    s = s * vt_ref[1, t, ch, :] + sa * vt_ref[2, t, ch, :] + vrows[t][u:u + 1] * vt_ref[3, t, ch, :]
                    ys[t].append(jnp.sum(s * vt_ref[4, t, ch, :], axis=0, keepdims=True))
                so_ref[h, v0 + u] = s
            for t in range(T):
                yt_ref[t, pl.ds(h * HEAD + v0, 8), :] = jnp.concatenate(ys[t], axis=0)
            return carry

        lax.fori_loop(0, HEAD // 8, body, 0)

    y = jnp.concatenate([yt_ref[t].T for t in range(T)], axis=0)
    yc = y - _mm_sum_rhs(y, ones2) * (1.0 / HEAD)
    var = _mm_sum_rhs(yc * yc, ones2) * (1.0 / HEAD)
    yn = yc * lax.rsqrt(var + GN_EPS) * lng_ref[...] + lnb_ref[...]
    o_ref[...] = ((yn + bonus) * (z * _sigmoid(z))).reshape(T, B, HPAIR)


def _wkv_seq(proj_t, shift_prev, s_t, prm):
    t, b, cols = proj_t.shape
    e = prm["w0"].shape[1]
    nh = e // HEAD
    kb = e // HPAIR
    bt = LANES
    assert b % bt == 0 and nh % 2 == 0
    off = dict(r=0, k=kb, v=2 * kb, l=3 * kb)
    pspec = lambda o: pl.BlockSpec((t, bt, HPAIR), lambda i, j, o=o: (0, j, o + i))
    pfix = lambda o: pl.BlockSpec((t, bt, HPAIR), lambda i, j, o=o: (0, j, o))
    sspec = lambda o: pl.BlockSpec((bt, HPAIR), lambda i, j, o=o: (j, o + i))
    sfix = lambda o: pl.BlockSpec((bt, HPAIR), lambda i, j, o=o: (j, o))
    mspec = lambda o: pl.BlockSpec((1, HPAIR), lambda i, j, o=o: (0, o + i))
    mfix = lambda o: pl.BlockSpec((1, HPAIR), lambda i, j, o=o: (0, o))
    row = pl.BlockSpec((1, HPAIR), lambda i, j: (0, i))
    lora = pl.BlockSpec((2 * LORA, HPAIR), lambda i, j: (0, i))
    sblk = pl.BlockSpec((2, HEAD, HEAD, bt), lambda i, j: (i, 0, 0, j))
    return pl.pallas_call(
        _wkv_seq_kernel,
        grid=(nh // 2, b // bt),
        in_specs=[pspec(off["r"]), pspec(off["k"]), pspec(off["v"]), pfix(off["l"]), pspec(off["l"] + 1),
                  sspec(off["r"]), sspec(off["k"]), sspec(off["v"]), sfix(off["l"]),
                  mspec(off["r"]), mspec(off["k"]), mspec(off["v"]), mfix(off["l"]),
                  row, lora, row, lora, row, row, row, row, row, sblk],
        out_specs=[pl.BlockSpec((t, bt, HPAIR), lambda i, j: (0, j, i)), sblk],
        out_shape=[jax.ShapeDtypeStruct((t, b, e), F32), jax.ShapeDtypeStruct(s_t.shape, F32)],
        scratch_shapes=[pltpu.VMEM((len(_VT), t, HPAIR, bt), F32),
                        pltpu.VMEM((t, HPAIR, bt), F32)],
        compiler_params=pltpu.CompilerParams(dimension_semantics=("arbitrary", "arbitrary"),
                                             vmem_limit_bytes=VMEM_LIMIT),
        name="wkv_seq",
    )(proj_t, proj_t, proj_t, proj_t, proj_t, shift_prev, shift_prev, shift_prev, shift_prev,
      prm["mu"], prm["mu"], prm["mu"], prm["mu"], prm["w0"], prm["w2p"], prm["a0"], prm["a2p"],
      prm["kk"], prm["ka"], prm["rk"], prm["lng"], prm["lnb"], s_t)


def _out_residual_kernel(o_ref, w_ref, x_ref, y_ref):
    y_ref[...] = x_ref[...] + jnp.dot(o_ref[...].astype(BF16), w_ref[...], preferred_element_type=F32)


def _out_residual(o2d, w_bf16, x2d, tm):
    n, e = o2d.shape
    d = x2d.shape[1]
    return pl.pallas_call(
        _out_residual_kernel,
        grid=(n // tm,),
        in_specs=[pl.BlockSpec((tm, e), lambda i: (i, 0)),
                  pl.BlockSpec((e, d), lambda i: (0, 0)),
                  pl.BlockSpec((tm, d), lambda i: (i, 0))],
        out_specs=pl.BlockSpec((tm, d), lambda i: (i, 0)),
        out_shape=jax.ShapeDtypeStruct((n, d), F32),
        compiler_params=pltpu.CompilerParams(dimension_semantics=("arbitrary",), vmem_limit_bytes=VMEM_LIMIT),
        name="out_residual",
    )(o2d, w_bf16, x2d)


def _gmlp_kernel(x_ref, g_ref, win_ref, vg_ref, vb_ref, wm_ref, bs_ref, wout_ref, nf_ref, *out_refs, emit_v):
    y_ref = out_refs[0]
    tm = x_ref.shape[0]
    e = vg_ref.shape[1]
    gd = e // GM_GROUPS
    x = x_ref[...]
    h = _rms_rows(x, g_ref[...]).astype(BF16)
    proj = jnp.dot(h, win_ref[...], preferred_element_type=F32)
    u = jax.nn.gelu(proj[:, :e])
    vf = jax.nn.gelu(proj[:, e:2 * e])
    z = proj[:, 2 * e:]
    vm = jnp.mean(vf, axis=-1, keepdims=True)
    vc = vf - vm
    vv = jnp.mean(vc * vc, axis=-1, keepdims=True)
    vn = vc * lax.rsqrt(vv + LN_EPS) * vg_ref[...] + vb_ref[...]
    if emit_v:
        out_refs[1][...] = vn
    vn16 = vn.astype(BF16)
    causal = _iota((GM_CHUNK, GM_CHUNK), 0) >= _iota((GM_CHUNK, GM_CHUNK), 1)
    rows = []
    for j in range(tm // GM_CHUNK):
        cols = []
        for gi in range(GM_GROUPS):
            wm = jnp.where(causal, wm_ref[gi], 0.0).astype(BF16)
            blk = vn16[j * GM_CHUNK:(j + 1) * GM_CHUNK, gi * gd:(gi + 1) * gd]
            cols.append(jnp.dot(wm, blk, preferred_element_type=F32) + bs_ref[gi])
        rows.append(jnp.concatenate(cols, axis=1))
    mixed = jnp.concatenate(rows, axis=0)
    o = u * mixed * (z * _sigmoid(z))
    x2 = x + jnp.dot(o.astype(BF16), wout_ref[...], preferred_element_type=F32)
    y_ref[...] = _rms_rows(x2, nf_ref[...])


def _gmlp(x2d, g, win_bf16, vg, vb, wmix, bias, wout_bf16, nf, *, tm, emit_v):
    n, d = x2d.shape
    e = vg.shape[0]
    const2 = lambda shape: pl.BlockSpec(shape, lambda i: (0, 0))
    out_shape = [jax.ShapeDtypeStruct((n, d), F32)]
    out_specs = [pl.BlockSpec((tm, d), lambda i: (i, 0))]
    if emit_v:
        out_shape.append(jax.ShapeDtypeStruct((n, e), F32))
        out_specs.append(pl.BlockSpec((tm, e), lambda i: (i, 0)))
    outs = pl.pallas_call(
        functools.partial(_gmlp_kernel, emit_v=emit_v),
        grid=(n // tm,),
        in_specs=[pl.BlockSpec((tm, d), lambda i: (i, 0)),
                  const2((1, d)),
                  pl.BlockSpec((d, 3 * e), lambda i: (0, 0), pipeline_mode=pl.Buffered(1)),
                  const2((1, e)), const2((1, e)),
                  pl.BlockSpec((GM_GROUPS, GM_CHUNK, GM_CHUNK), lambda i: (0, 0, 0)),
                  pl.BlockSpec((GM_GROUPS, GM_CHUNK, 1), lambda i: (0, 0, 0)),
                  pl.BlockSpec((e, d), lambda i: (0, 0), pipeline_mode=pl.Buffered(1)),
                  const2((1, d))],
        out_specs=out_specs,
        out_shape=out_shape,
        compiler_params=pltpu.CompilerParams(dimension_semantics=("arbitrary",), vmem_limit_bytes=VMEM_LIMIT),
        name="gmlp",
    )(x2d, g.reshape(1, d), win_bf16, vg.reshape(1, e), vb.reshape(1, e), wmix,
      bias.reshape(GM_GROUPS, GM_CHUNK, 1), wout_bf16, nf.reshape(1, d))
    return outs


def _row_tile(n, pref):
    t = pref
    while n % t:
        t //= 2
    return t


def kernel(x_prompt, x_sample, state_shift, state_wkv, norm_g, norm_f, rw_in, rw_mu, rw_w0, rw_w2, rw_a0, rw_a2, rw_kk, rw_ka, rw_rk, rw_lnx_g, rw_lnx_b, rw_out, gm_in, gm_vg, gm_vb, gm_ws, gm_bs, gm_out):
    bp, tp, d = x_prompt.shape
    bs_, ts, _ = x_sample.shape
    e = rw_w0.shape[1]
    nh = e // HEAD
    shift = 3 * e + 2 * LORA
    xp = x_prompt.reshape(bp * tp, d)

    zeros_l = jnp.zeros((LORA, e), F32)
    prm = dict(
        mu=rw_mu[0].reshape(1, shift), w0=rw_w0[0].reshape(1, e), a0=rw_a0[0].reshape(1, e),
        w2p=jnp.concatenate([rw_w2[0], zeros_l], axis=0).astype(BF16),
        a2p=jnp.concatenate([zeros_l, rw_a2[0]], axis=0).astype(BF16),
        kk=rw_kk[0].reshape(1, e), ka=rw_ka[0].reshape(1, e), rk=rw_rk[0].reshape(1, e),
        lng=rw_lnx_g[0].reshape(1, e), lnb=rw_lnx_b[0].reshape(1, e))
    w_in = rw_in[0].astype(BF16)
    w_out = rw_out[0].astype(BF16)

    step_rows = WKV_STEP_CHUNKS * WKV_CHUNK
    assert tp % step_rows == 0 and tp % GM_CHUNK == 0
    ops_p, shift_p = _proj_prep(x_prompt, norm_g[0], w_in, jnp.zeros((bp, shift), F32), prm,
                                chunk=WKV_CHUNK, tm=max(step_rows, _row_tile(tp, PREP_ROWS)), t_tile=step_rows)
    o_p, wkv_p = _wkv_chunks(ops_p, jnp.zeros((bp, nh, HEAD, HEAD), F32), prm["lng"], prm["lnb"],
                             seq_len=tp, chunk=WKV_CHUNK, chunks_per_step=WKV_STEP_CHUNKS,
                             n_seq=_row_tile(bp, WKV_STEP_SEQS))
    x1p = _out_residual(o_p, w_out, xp, _row_tile(bp * tp, OUT_ROWS))

    xs_t = jnp.transpose(x_sample, (1, 0, 2)).reshape(ts * bs_, d)
    proj_s = _norm_proj(xs_t, norm_g[0], w_in, _row_tile(bs_ * ts, PROJ_ROWS)).reshape(ts, bs_, shift + e)
    o_s, wkv_s_t = _wkv_seq(proj_s, state_shift[0], jnp.transpose(state_wkv[0], (1, 2, 3, 0)), prm)
    wkv_s = jnp.transpose(wkv_s_t, (3, 0, 1, 2))
    shift_s = proj_s[ts - 1, :, :shift]
    x1s_t = _out_residual(o_s.reshape(ts * bs_, e), w_out, xs_t, _row_tile(bs_ * ts, OUT_ROWS))
    x1s = jnp.transpose(x1s_t.reshape(ts, bs_, d), (1, 0, 2)).reshape(bs_ * ts, d)

    g_in = gm_in[0].astype(BF16)
    g_out = gm_out[0].astype(BF16)
    wm_p = gm_ws[0]
    reps = GM_CHUNK // ts
    eye = jnp.eye(reps, dtype=F32)
    wm_s = jax.vmap(lambda w: jnp.kron(eye, w))(wm_p[:, :ts, :ts])
    bias_s = jnp.tile(gm_bs[0][:, :ts], (1, reps))

    (y_p,) = _gmlp(x1p, norm_g[1], g_in, gm_vg[0], gm_vb[0], wm_p, gm_bs[0], g_out, norm_f,
                   tm=_row_tile(bp * tp, GMLP_ROWS), emit_v=False)
    y_s, v_s = _gmlp(x1s, norm_g[1], g_in, gm_vg[0], gm_vb[0], wm_s, bias_s, g_out, norm_f,
                     tm=_row_tile(bs_ * ts, PROJ_ROWS), emit_v=True)

    return (y_p.reshape(bp, tp, d), y_s.reshape(bs_, ts, d),
            shift_p[None], wkv_p[None], shift_s[None], wkv_s[None],
            v_s.reshape(1, bs_, ts, e))
```

```python
import functools
import math

import jax
import jax.numpy as jnp
from jax import lax
from jax.experimental import pallas as pl
from jax.experimental.pallas import tpu as pltpu

F32 = jnp.float32
BF16 = jnp.bfloat16

LANES = 128
HEAD = 64
QUAD = 4
QW = QUAD * HEAD
HPAIR = 2 * HEAD
assert HPAIR == LANES
LORA = 64
NORM_EPS = 1e-6
LN_EPS = 1e-5
GN_EPS = 64e-5
GM_CHUNK = 128
GM_GROUPS = 8
VMEM_LIMIT = 56 * 1024 * 1024

WKV_CHUNK = 64
WKV_STEP_CHUNKS = 2
WKV_STEP_SEQS = 2
PREP_ROWS = 4 * WKV_CHUNK
PROJ_COL_TILES = 5
OUT_ROWS = 1024
GMLP_ROWS = 512
GMLP_ROWS_SMALL = 256
SUBLANES = 8


def _split2(x):
    hi = x.astype(BF16)
    lo = (x - hi.astype(F32)).astype(BF16)
    return hi, lo


_NN = (((1,), (0,)), ((), ()))
_NT = (((1,), (1,)), ((), ()))
_TN = (((0,), (0,)), ((), ()))


def _mm(a, b, dims=_NN):
    return lax.dot_general(a.astype(BF16), b.astype(BF16), dims, preferred_element_type=F32)


def _mm_sum_rhs(a, b_bf16, dims=_NN):
    d = functools.partial(lax.dot_general, dimension_numbers=dims, preferred_element_type=F32)
    h, l = _split2(a)
    return d(h, b_bf16) + d(l, b_bf16)


def _mm_sum_lhs(a_bf16, b, dims=_NN):
    d = functools.partial(lax.dot_general, dimension_numbers=dims, preferred_element_type=F32)
    h, l = _split2(b)
    return d(a_bf16, h) + d(a_bf16, l)


def _iota(shape, dim):
    return lax.broadcasted_iota(jnp.int32, shape, dim)


def _sigmoid(x):
    return 0.5 * jnp.tanh(0.5 * x) + 0.5


def _rms_rows(x, g):
    return x * lax.rsqrt(jnp.mean(x * x, axis=-1, keepdims=True) + NORM_EPS) * g


def _head_ones(width):
    return ((_iota((width, width), 0) // HEAD) == (_iota((width, width), 1) // HEAD)).astype(BF16)


def _norm_proj_kernel(x_ref, g_ref, w_ref, o_ref, w16_ref, h_ref):
    @pl.when(pl.program_id(0) == 0)
    def _norm():
        h_ref[...] = _rms_rows(x_ref[...], g_ref[...]).astype(BF16)

    w16 = w_ref[...].astype(BF16)
    w16_ref[...] = w16
    o_ref[...] = jnp.dot(h_ref[...], w16, preferred_element_type=F32)


def _norm_proj(x2d, g, w_f32, col_tiles):
    n, d = x2d.shape
    cols = w_f32.shape[1]
    tn = cols // col_tiles
    assert cols % col_tiles == 0 and tn % LANES == 0
    return pl.pallas_call(
        _norm_proj_kernel,
        grid=(col_tiles,),
        in_specs=[pl.BlockSpec((n, d), lambda j: (0, 0)),
                  pl.BlockSpec((1, d), lambda j: (0, 0)),
                  pl.BlockSpec((d, tn), lambda j: (0, j))],
        out_specs=[pl.BlockSpec((n, tn), lambda j: (0, j)),
                   pl.BlockSpec((d, tn), lambda j: (0, j))],
        out_shape=[jax.ShapeDtypeStruct((n, cols), F32), jax.ShapeDtypeStruct((d, cols), BF16)],
        scratch_shapes=[pltpu.VMEM((n, d), BF16)],
        compiler_params=pltpu.CompilerParams(dimension_semantics=("arbitrary",), vmem_limit_bytes=VMEM_LIMIT),
        name="norm_proj",
    )(x2d, g.reshape(1, d), w_f32)


_OPS_MM = ("at", "rt", "bt", "kt", "v", "bg", "kg")
_OPS_T = ("bt", "kt")
_OPS_OUT = ("bonus", "gate")


def _proj_prep_kernel(x_ref, g_ref, w_ref, sp_ref, mu_ref, w0_ref, w2_ref, a0_ref, a2_ref, kk_ref, ka_ref, rk_ref,
                      at_ref, rt_ref, bt_ref, kt_ref, v_ref, bg_ref, kg_ref, bonus_ref, gate_ref, gt_ref, sho_ref,
                      carry_ref, *, chunk):
    j = pl.program_id(1)
    tm = x_ref.shape[0]
    E = w0_ref.shape[1]
    SHIFT = 3 * E + 2 * LORA
    C = chunk
    n_ch = tm // C
    outs = dict(zip(_OPS_MM + _OPS_OUT,
                    (at_ref, rt_ref, bt_ref, kt_ref, v_ref, bg_ref, kg_ref, bonus_ref, gate_ref)))

    @pl.when(j == 0)
    def _init():
        carry_ref[...] = sp_ref[0]

    h = _rms_rows(x_ref[...], g_ref[...]).astype(BF16)
    first_row = _iota((tm, 1), 0) == 0
    ones_bd = _head_ones(QW)
    row = _iota((tm, tm), 0)
    col = _iota((tm, tm), 1)
    tri = ((row >= col) & (row // C == col // C)).astype(BF16)

    def proj(lo, hi):
        return jnp.dot(h, w_ref[:, lo:hi], preferred_element_type=F32)

    def shifted(lo, hi):
        sh = proj(lo, hi)
        prev = jnp.where(first_row, carry_ref[:, lo:hi], pltpu.roll(sh, 1, 0))
        carry_ref[:, lo:hi] = sh[tm - 1:tm]
        return sh + (prev - sh) * mu_ref[:, lo:hi]

    def per_chunk_last(x):
        return jnp.concatenate(
            [jnp.broadcast_to(x[c * C + C - 1:c * C + C], (C, x.shape[1])) for c in range(n_ch)], axis=0)

    x_lora = shifted(3 * E, SHIFT)
    lane_l = _iota((tm, 2 * LORA), 1)
    t_lora = jnp.where(lane_l < LORA, jnp.tanh(x_lora), x_lora)
    wf = w0_ref[...] + _mm(t_lora, w2_ref[...])
    af = a0_ref[...] + _mm(x_lora, a2_ref[...])

    for q in range(E // QW):
        lo = q * QW
        sl = slice(lo, lo + QW)
        r = shifted(lo, lo + QW)
        k = shifted(E + lo, E + lo + QW)
        v = shifted(2 * E + lo, 2 * E + lo + QW)
        z = proj(SHIFT + lo, SHIFT + lo + QW)
        ld = -math.exp(-0.5) * _sigmoid(wf[:, sl])
        a = _sigmoid(af[:, sl])
        kk0 = k * kk_ref[:, sl]
        k2 = k * (1.0 + (a - 1.0) * ka_ref[:, sl])
        sums = _mm(jnp.concatenate([kk0 * kk0, r * k2 * rk_ref[:, sl]], axis=0), ones_bd)
        kk = kk0 * jnp.minimum(lax.rsqrt(sums[:tm]), 1e12)
        cum = _mm_sum_lhs(tri, ld)
        cum_last = per_chunk_last(cum)
        g_rem = jnp.exp(cum_last - cum)
        g_inv = jnp.exp(-cum)
        kka = kk * a
        quad = dict(at=-kk * jnp.exp(cum - ld), rt=r * jnp.exp(cum), bt=kka * g_inv, kt=k2 * g_inv, v=v,
                    bg=kka * g_rem, kg=k2 * g_rem, bonus=sums[tm:] * v, gate=z * _sigmoid(z))
        for n, x in quad.items():
            if n in _OPS_T:
                xt = x.astype(BF16).T
                tt = outs[n].shape[2]
                for i in range(tm // tt):
                    outs[n][i, sl, :] = xt[:, i * tt:(i + 1) * tt]
            else:
                outs[n][:, sl] = x.astype(outs[n].dtype)
        for c in range(n_ch):
            gt_ref[c, :, sl] = jnp.exp(cum[c * C + C - 1:c * C + C])

    @pl.when(j == pl.num_programs(1) - 1)
    def _shift_out():
        sho_ref[0] = carry_ref[...]


def _proj_prep(x3d, g, w_bf16, shift_prev, prm, *, chunk, tm, t_tile):
    b, t, d = x3d.shape
    e = prm["w0"].shape[1]
    shift = 3 * e + 2 * LORA
    cols = w_bf16.shape[1]
    assert t % tm == 0 and tm % t_tile == 0 and t_tile % chunk == 0 and e % QW == 0
    steps = t // tm
    n_ch = tm // chunk
    c2 = lambda shape: pl.BlockSpec(shape, lambda i, j: (0, 0))
    rows = pl.BlockSpec((tm, e), lambda i, j: (i * steps + j, 0))
    cols_t = pl.BlockSpec((tm // t_tile, e, t_tile), lambda i, j: (i * steps + j, 0, 0))
    outs = pl.pallas_call(
        functools.partial(_proj_prep_kernel, chunk=chunk),
        grid=(b, steps),
        in_specs=[pl.BlockSpec((tm, d), lambda i, j: (i * steps + j, 0)),
                  c2((1, d)),
                  pl.BlockSpec((d, cols), lambda i, j: (0, 0), pipeline_mode=pl.Buffered(1)),
                  pl.BlockSpec((1, 1, shift), lambda i, j: (i, 0, 0)),
                  c2((1, shift)), c2((1, e)), c2((2 * LORA, e)), c2((1, e)), c2((2 * LORA, e)),
                  c2((1, e)), c2((1, e)), c2((1, e))],
        out_specs=([cols_t if n in _OPS_T else rows for n in _OPS_MM + _OPS_OUT]
                   + [pl.BlockSpec((n_ch, 1, e), lambda i, j: (i * steps + j, 0, 0)),
                      pl.BlockSpec((1, 1, shift), lambda i, j: (i, 0, 0))]),
        out_shape=([jax.ShapeDtypeStruct((b * t // t_tile, e, t_tile) if n in _OPS_T else (b * t, e), BF16)
                    for n in _OPS_MM + _OPS_OUT]
                   + [jax.ShapeDtypeStruct((b * t // chunk, 1, e), F32),
                      jax.ShapeDtypeStruct((b, 1, shift), F32)]),
        scratch_shapes=[pltpu.VMEM((1, shift), F32)],
        compiler_params=pltpu.CompilerParams(dimension_semantics=("arbitrary", "arbitrary"),
                                             vmem_limit_bytes=VMEM_LIMIT),
        name="proj_prep",
    )(x3d.reshape(b * t, d), g.reshape(1, d), w_bf16, shift_prev.reshape(b, 1, shift),
      prm["mu"], prm["w0"], prm["w2p"], prm["a0"], prm["a2p"], prm["kk"], prm["ka"], prm["rk"])
    ops = dict(zip(_OPS_MM + _OPS_OUT + ("g_tot",), outs[:-1]))
    return ops, outs[-1].reshape(b, shift)


def _headsum(xs, ones_bd, split):
    e = xs[0].shape[1]
    nq = e // QW
    rows = [x[:, q * QW:(q + 1) * QW] for x in xs for q in range(nq)]
    stacked = jnp.concatenate(rows, axis=0)
    s = _mm_sum_rhs(stacked, ones_bd) if split else _mm(stacked, ones_bd)
    c = xs[0].shape[0]
    outs = []
    for i in range(len(xs)):
        outs.append(jnp.concatenate([s[(i * nq + q) * c:(i * nq + q + 1) * c] for q in range(nq)], axis=1))
    return outs


def _wkv_core(ops, states, lng, lnb, *, n_levels):
    C, E = ops["at"].shape
    TL = QUAD * C
    NQ = E // QW
    mask_bd = (_iota((QW, QW), 0) // HEAD) == (_iota((QW, QW), 1) // HEAD)
    ones_bd = mask_bd.astype(BF16)
    tok_t = _iota((C, TL), 0)
    tok_j = _iota((C, TL), 1) % C
    strict = tok_j < tok_t
    incl = tok_j <= tok_t
    eye_all = (tok_j == tok_t).astype(F32)

    def block_diag(x, width):
        assert 2 * width == LANES
        x = x.astype(BF16)
        zero = jnp.zeros((), BF16)
        half = _iota((C, LANES), 1) < width
        pad = jnp.zeros((C, LANES), BF16)
        rows = []
        for h in range(QUAD):
            t = h // 2
            tile = jnp.where(half if h % 2 == 0 else ~half, x[:, t * LANES:(t + 1) * LANES], zero)
            rows.append(jnp.concatenate([tile if i == t else pad for i in range(QUAD // 2)], axis=1))
        return jnp.concatenate(rows, axis=0)

    def cols_bd(xt):
        zero = jnp.zeros((HEAD, C), xt.dtype)
        return jnp.concatenate(
            [jnp.concatenate([xt[h * HEAD:(h + 1) * HEAD] if i == h else zero for i in range(QUAD)], axis=1)
             for h in range(QUAD)], axis=0)

    stack_hc = functools.partial(block_diag, width=HEAD)
    stack_tt = functools.partial(block_diag, width=C)

    at, rt, bt, kt, v, bg, kg = (ops[n] for n in _OPS_MM)
    quads = range(NQ)
    sls = [slice(q * QW, (q + 1) * QW) for q in quads]
    ars = [jnp.concatenate([at[:, sl], rt[:, sl]], axis=0) for sl in sls]
    v_sts = [stack_hc(v[:, sl]) for sl in sls]
    abs_ = [_mm(ars[q], cols_bd(bt[sls[q], :])) for q in quads]
    aks = [_mm(ars[q], cols_bd(kt[sls[q], :])) for q in quads]
    pws = [jnp.where(strict, abs_[q][:C], 0.0) for q in quads]
    a_rbs = [jnp.where(incl, abs_[q][C:], 0.0) for q in quads]
    a_aks = [jnp.where(strict, aks[q][:C], 0.0) for q in quads]
    a_rks = [jnp.where(incl, aks[q][C:], 0.0) for q in quads]
    invs = [eye_all + pws[q] for q in quads]
    pws = [_mm(pws[q], stack_tt(pws[q])) for q in quads]
    for _ in range(n_levels - 1):
        bds = [stack_tt(pws[q]) for q in quads]
        res = [_mm(jnp.concatenate([pws[q], invs[q]], axis=0), bds[q]) for q in quads]
        pws = [res[q][:C] for q in quads]
        invs = [invs[q] + res[q][C:] for q in quads]
    invs = [invs[q] + _mm(invs[q], stack_tt(pws[q])) for q in quads]
    arss = [_mm(ars[q], states[q], _NT) for q in quads]
    avs = [_mm(jnp.concatenate([a_aks[q], a_rks[q]], axis=0), v_sts[q]) for q in quads]
    ws = [arss[q][:C] + avs[q][:C] for q in quads]
    us = [_mm(invs[q], stack_hc(ws[q])) for q in quads]
    ys = [arss[q][C:] + avs[q][C:] + _mm(a_rbs[q], stack_hc(us[q])) for q in quads]
    new_states = []
    for q in quads:
        uv = jnp.concatenate([us[q].astype(BF16), v[:, sls[q]]], axis=0)
        bk = jnp.concatenate([bg[:, sls[q]], kg[:, sls[q]]], axis=0)
        new_states.append(states[q] * ops["g_tot"][:, sls[q]] + jnp.where(mask_bd, _mm(uv, bk, _TN), 0.0))
    y = jnp.concatenate(ys, axis=1)

    (ysum,) = _headsum([y], ones_bd, split=True)
    yc = y - ysum * (1.0 / HEAD)
    (vsum,) = _headsum([yc * yc], ones_bd, split=False)
    yn = yc * lax.rsqrt(vsum * (1.0 / HEAD) + GN_EPS) * lng + lnb
    return (yn + ops["bonus"]) * ops["gate"], new_states


def _wkv_chunks_kernel(at_ref, rt_ref, bt_ref, kt_ref, v_ref, bg_ref, kg_ref, bonus_ref, gate_ref, gt_ref, s0_ref,
                       lng_ref, lnb_ref, o_ref, so_ref, st_ref, *, chunk, n_levels):
    j = pl.program_id(1)
    n_seq = at_ref.shape[1]
    NQ = st_ref.shape[0] // n_seq
    E = NQ * QW
    seqs = range(n_seq)
    refs = dict(zip(_OPS_MM + _OPS_OUT,
                    (at_ref, rt_ref, bt_ref, kt_ref, v_ref, bg_ref, kg_ref, bonus_ref, gate_ref)))

    @pl.when(j == 0)
    def _init():
        zero_blk = jnp.zeros((HEAD, HEAD), F32)
        for s in seqs:
            for q in range(NQ):
                st_ref[s * NQ + q] = jnp.concatenate(
                    [jnp.concatenate([s0_ref[0, s, QUAD * q + h] if i == h else zero_blk for i in range(QUAD)],
                                     axis=1) for h in range(QUAD)], axis=0)

    lng = jnp.concatenate([lng_ref[...]] * n_seq, axis=1)
    lnb = jnp.concatenate([lnb_ref[...]] * n_seq, axis=1)
    states = [st_ref[q] for q in range(n_seq * NQ)]
    for c in range(at_ref.shape[2] // chunk):
        rows = slice(c * chunk, (c + 1) * chunk)
        ops = {}
        for n, ref in refs.items():
            if n in _OPS_T:
                ops[n] = jnp.concatenate([ref[0, s, 0, :, rows] for s in seqs], axis=0)
            else:
                ops[n] = jnp.concatenate([ref[0, s, rows, :] for s in seqs], axis=1)
        ops["g_tot"] = jnp.concatenate([gt_ref[0, s, c] for s in seqs], axis=1)
        o, states = _wkv_core(ops, states, lng, lnb, n_levels=n_levels)
        for s in seqs:
            o_ref[0, s, rows, :] = o[:, s * E:(s + 1) * E].astype(o_ref.dtype)
    for q in range(n_seq * NQ):
        st_ref[q] = states[q]

    @pl.when(j == pl.num_programs(1) - 1)
    def _state_out():
        for s in seqs:
            for q in range(NQ):
                for h in range(QUAD):
                    so_ref[0, s, QUAD * q + h] = states[s * NQ + q][h * HEAD:(h + 1) * HEAD, h * HEAD:(h + 1) * HEAD]


def _wkv_chunks(ops, s0, lng, lnb, *, seq_len, chunk, chunks_per_step, n_seq):
    n, e = ops["at"].shape
    b = n // seq_len
    nh = e // HEAD
    tm = chunk * chunks_per_step
    steps = seq_len // tm
    assert ops["bt"].shape[2] == tm
    assert b % n_seq == 0
    g = b // n_seq
    rows = pl.BlockSpec((1, n_seq, tm, e), lambda i, j: (i, 0, j, 0))
    cols_t = pl.BlockSpec((1, n_seq, 1, e, tm), lambda i, j: (i, 0, j, 0, 0))
    c2 = lambda shape: pl.BlockSpec(shape, lambda i, j: (0, 0))
    sblk = pl.BlockSpec((1, n_seq, nh, HEAD, HEAD), lambda i, j: (i, 0, 0, 0, 0))
    n_levels = max(1, math.ceil(math.log2(chunk)) - 1)
    args = [ops[n_].reshape((g, n_seq, steps, e, tm) if n_ in _OPS_T else (g, n_seq, seq_len, e))
            for n_ in _OPS_MM + _OPS_OUT]
    o, so = pl.pallas_call(
        functools.partial(_wkv_chunks_kernel, chunk=chunk, n_levels=n_levels),
        grid=(g, steps),
        in_specs=([cols_t if n_ in _OPS_T else rows for n_ in _OPS_MM + _OPS_OUT]
                  + [pl.BlockSpec((1, n_seq, chunks_per_step, 1, e), lambda i, j: (i, 0, j, 0, 0)),
                     sblk, c2((1, e)), c2((1, e))]),
        out_specs=[rows, sblk],
        out_shape=[jax.ShapeDtypeStruct((g, n_seq, seq_len, e), BF16),
                   jax.ShapeDtypeStruct((g, n_seq, nh, HEAD, HEAD), F32)],
        scratch_shapes=[pltpu.VMEM((n_seq * (e // QW), QW, QW), F32)],
        compiler_params=pltpu.CompilerParams(dimension_semantics=("arbitrary", "arbitrary"),
                                             vmem_limit_bytes=VMEM_LIMIT),
        name="wkv_chunks",
    )(*args, ops["g_tot"].reshape(g, n_seq, seq_len // chunk, 1, e), s0.reshape(g, n_seq, nh, HEAD, HEAD), lng, lnb)
    return o.reshape(n, e), so.reshape(b, nh, HEAD, HEAD)


_VT = ("nkk", "w", "b", "k", "r", "v")


def _wkv_seq_kernel(pr_ref, pk_ref, pv_ref, pl_ref, pz_ref, sr_ref, sk_ref, sv_ref, sl_ref,
                    mur_ref, muk_ref, muv_ref, mul_ref, w0_ref, w2_ref, a0_ref, a2_ref, kk_ref, ka_ref, rk_ref,
                    lng_ref, lnb_ref, s_ref, o_ref, so_ref, vt_ref, yt_ref):
    T, B, _ = pr_ref.shape
    ones2 = _head_ones(HPAIR)

    def shifted(p_ref, prev_ref, mu_ref):
        sh = p_ref[...].reshape(T * B, HPAIR)
        prev = jnp.concatenate([prev_ref[...], sh[:(T - 1) * B]], axis=0)
        return sh + (prev - sh) * mu_ref[...]

    r = shifted(pr_ref, sr_ref, mur_ref)
    k = shifted(pk_ref, sk_ref, muk_ref)
    v = shifted(pv_ref, sv_ref, muv_ref)
    x_lora = shifted(pl_ref, sl_ref, mul_ref)
    z = pz_ref[...].reshape(T * B, HPAIR)
    lane_l = _iota((T * B, 2 * LORA), 1)
    t_lora = jnp.where(lane_l < LORA, jnp.tanh(x_lora), x_lora)
    wf = w0_ref[...] + _mm(t_lora, w2_ref[...])
    af = a0_ref[...] + _mm(x_lora, a2_ref[...])
    w = jnp.exp(-math.exp(-0.5) * _sigmoid(wf))
    a = _sigmoid(af)
    kk0 = k * kk_ref[...]
    k2 = k * (1.0 + (a - 1.0) * ka_ref[...])
    sums = _mm_sum_rhs(jnp.concatenate([kk0 * kk0, r * k2 * rk_ref[...]], axis=0), ones2)
    kk = kk0 * jnp.minimum(lax.rsqrt(sums[:T * B]), 1e12)
    bonus = sums[T * B:] * v
    vecs = dict(nkk=-kk, w=w, b=kk * a, k=k2, r=r, v=v)
    for i, n in enumerate(_VT):
        for t in range(T):
            vt_ref[i, t] = vecs[n][t * B:(t + 1) * B].T

    for h in range(2):
        ch = slice(h * HEAD, (h + 1) * HEAD)

        def body(g, carry, h=h, ch=ch):
            v0 = pl.multiple_of(g * SUBLANES, SUBLANES)
            vrows = [vt_ref[5, t, pl.ds(h * HEAD + v0, SUBLANES), :] for t in range(T)]
            ys = [[] for _ in range(T)]
            for u in range(SUBLANES):
                s = s_ref[h, v0 + u]
                for t in range(T):
                    sa = jnp.sum(s * vt_ref[0, t, ch, :], axis=0, keepdims=True)
                    s = s * vt_ref[1, t, ch, :] + sa * vt_ref[2, t, ch, :] + vrows[t][u:u + 1] * vt_ref[3, t, ch, :]
                    ys[t].append(jnp.sum(s * vt_ref[4, t, ch, :], axis=0, keepdims=True))
                so_ref[h, v0 + u] = s
            for t in range(T):
                yt_ref[t, pl.ds(h * HEAD + v0, SUBLANES), :] = jnp.concatenate(ys[t], axis=0)
            return carry

        lax.fori_loop(0, HEAD // SUBLANES, body, 0)

    y = jnp.concatenate([yt_ref[t].T for t in range(T)], axis=0)
    yc = y - _mm_sum_rhs(y, ones2) * (1.0 / HEAD)
    var = _mm_sum_rhs(yc * yc, ones2) * (1.0 / HEAD)
    yn = yc * lax.rsqrt(var + GN_EPS) * lng_ref[...] + lnb_ref[...]
    o_ref[...] = ((yn + bonus) * (z * _sigmoid(z))).reshape(T, B, HPAIR)


def _wkv_seq(proj_t, shift_prev, s_t, prm):
    t, b, cols = proj_t.shape
    e = prm["w0"].shape[1]
    nh = e // HEAD
    kb = e // HPAIR
    bt = LANES
    assert b % bt == 0 and nh % 2 == 0
    off = dict(r=0, k=kb, v=2 * kb, l=3 * kb)
    pspec = lambda o: pl.BlockSpec((t, bt, HPAIR), lambda i, j, o=o: (0, j, o + i))
    pfix = lambda o: pl.BlockSpec((t, bt, HPAIR), lambda i, j, o=o: (0, j, o))
    sspec = lambda o: pl.BlockSpec((bt, HPAIR), lambda i, j, o=o: (j, o + i))
    sfix = lambda o: pl.BlockSpec((bt, HPAIR), lambda i, j, o=o: (j, o))
    mspec = lambda o: pl.BlockSpec((1, HPAIR), lambda i, j, o=o: (0, o + i))
    mfix = lambda o: pl.BlockSpec((1, HPAIR), lambda i, j, o=o: (0, o))
    row = pl.BlockSpec((1, HPAIR), lambda i, j: (0, i))
    lora = pl.BlockSpec((2 * LORA, HPAIR), lambda i, j: (0, i))
    sblk = pl.BlockSpec((2, HEAD, HEAD, bt), lambda i, j: (i, 0, 0, j))
    return pl.pallas_call(
        _wkv_seq_kernel,
        grid=(nh // 2, b // bt),
        in_specs=[pspec(off["r"]), pspec(off["k"]), pspec(off["v"]), pfix(off["l"]), pspec(off["l"] + 1),
                  sspec(off["r"]), sspec(off["k"]), sspec(off["v"]), sfix(off["l"]),
                  mspec(off["r"]), mspec(off["k"]), mspec(off["v"]), mfix(off["l"]),
                  row, lora, row, lora, row, row, row, row, row, sblk],
        out_specs=[pl.BlockSpec((t, bt, HPAIR), lambda i, j: (0, j, i)), sblk],
        out_shape=[jax.ShapeDtypeStruct((t, b, e), F32), jax.ShapeDtypeStruct(s_t.shape, F32)],
        scratch_shapes=[pltpu.VMEM((len(_VT), t, HPAIR, bt), F32),
                        pltpu.VMEM((t, HPAIR, bt), F32)],
        compiler_params=pltpu.CompilerParams(dimension_semantics=("arbitrary", "arbitrary"),
                                             vmem_limit_bytes=VMEM_LIMIT),
        name="wkv_seq",
    )(proj_t, proj_t, proj_t, proj_t, proj_t, shift_prev, shift_prev, shift_prev, shift_prev,
      prm["mu"], prm["mu"], prm["mu"], prm["mu"], prm["w0"], prm["w2p"], prm["a0"], prm["a2p"],
      prm["kk"], prm["ka"], prm["rk"], prm["lng"], prm["lnb"], s_t)


def _out_residual_kernel(o_ref, w_ref, x_ref, y_ref):
    y_ref[...] = x_ref[...] + jnp.dot(o_ref[...].astype(BF16), w_ref[...], preferred_element_type=F32)


def _out_residual(o2d, w_bf16, x2d, tm):
    n, e = o2d.shape
    d = x2d.shape[1]
    return pl.pallas_call(
        _out_residual_kernel,
        grid=(n // tm,),
        in_specs=[pl.BlockSpec((tm, e), lambda i: (i, 0)),
                  pl.BlockSpec((e, d), lambda i: (0, 0)),
                  pl.BlockSpec((tm, d), lambda i: (i, 0))],
        out_specs=pl.BlockSpec((tm, d), lambda i: (i, 0)),
        out_shape=jax.ShapeDtypeStruct((n, d), F32),
        compiler_params=pltpu.CompilerParams(dimension_semantics=("arbitrary",), vmem_limit_bytes=VMEM_LIMIT),
        name="out_residual",
    )(o2d, w_bf16, x2d)


def _gmlp_kernel(x_ref, g_ref, win_ref, vg_ref, vb_ref, wm_ref, bs_ref, wout_ref, nf_ref, *out_refs, emit_v):
    y_ref = out_refs[0]
    tm = x_ref.shape[0]
    e = vg_ref.shape[1]
    gd = e // GM_GROUPS
    x = x_ref[...]
    h = _rms_rows(x, g_ref[...]).astype(BF16)
    proj = jnp.dot(h, win_ref[...], preferred_element_type=F32)
    u = jax.nn.gelu(proj[:, :e])
    vf = jax.nn.gelu(proj[:, e:2 * e])
    z = proj[:, 2 * e:]
    vm = jnp.mean(vf, axis=-1, keepdims=True)
    vc = vf - vm
    vv = jnp.mean(vc * vc, axis=-1, keepdims=True)
    vn = vc * lax.rsqrt(vv + LN_EPS) * vg_ref[...] + vb_ref[...]
    if emit_v:
        out_refs[1][...] = vn
    vn16 = vn.astype(BF16)
    causal = _iota((GM_CHUNK, GM_CHUNK), 0) >= _iota((GM_CHUNK, GM_CHUNK), 1)
    rows = []
    for j in range(tm // GM_CHUNK):
        cols = []
        for gi in range(GM_GROUPS):
            wm = jnp.where(causal, wm_ref[gi], 0.0).astype(BF16)
            blk = vn16[j * GM_CHUNK:(j + 1) * GM_CHUNK, gi * gd:(gi + 1) * gd]
            cols.append(jnp.dot(wm, blk, preferred_element_type=F32) + bs_ref[gi])
        rows.append(jnp.concatenate(cols, axis=1))
    mixed = jnp.concatenate(rows, axis=0)
    o = u * mixed * (z * _sigmoid(z))
    x2 = x + jnp.dot(o.astype(BF16), wout_ref[...], preferred_element_type=F32)
    y_ref[...] = _rms_rows(x2, nf_ref[...])


def _gmlp(x2d, g, win_bf16, vg, vb, wmix, bias, wout_bf16, nf, *, tm, emit_v):
    n, d = x2d.shape
    e = vg.shape[0]
    const2 = lambda shape: pl.BlockSpec(shape, lambda i: (0, 0))
    out_shape = [jax.ShapeDtypeStruct((n, d), F32)]
    out_specs = [pl.BlockSpec((tm, d), lambda i: (i, 0))]
    if emit_v:
        out_shape.append(jax.ShapeDtypeStruct((n, e), F32))
        out_specs.append(pl.BlockSpec((tm, e), lambda i: (i, 0)))
    outs = pl.pallas_call(
        functools.partial(_gmlp_kernel, emit_v=emit_v),
        grid=(n // tm,),
        in_specs=[pl.BlockSpec((tm, d), lambda i: (i, 0)),
                  const2((1, d)),
                  pl.BlockSpec((d, 3 * e), lambda i: (0, 0), pipeline_mode=pl.Buffered(1)),
                  const2((1, e)), const2((1, e)),
                  pl.BlockSpec((GM_GROUPS, GM_CHUNK, GM_CHUNK), lambda i: (0, 0, 0)),
                  pl.BlockSpec((GM_GROUPS, GM_CHUNK, 1), lambda i: (0, 0, 0)),
                  pl.BlockSpec((e, d), lambda i: (0, 0), pipeline_mode=pl.Buffered(1)),
                  const2((1, d))],
        out_specs=out_specs,
        out_shape=out_shape,
        compiler_params=pltpu.CompilerParams(dimension_semantics=("arbitrary",), vmem_limit_bytes=VMEM_LIMIT),
        name="gmlp",
    )(x2d, g.reshape(1, d), win_bf16, vg.reshape(1, e), vb.reshape(1, e), wmix,
      bias.reshape(GM_GROUPS, GM_CHUNK, 1), wout_bf16, nf.reshape(1, d))
    return outs


def _row_tile(n, pref):
    t = pref
    while n % t:
        t //= 2
    return t


def kernel(x_prompt, x_sample, state_shift, state_wkv, norm_g, norm_f, rw_in, rw_mu, rw_w0, rw_w2, rw_a0, rw_a2, rw_kk, rw_ka, rw_rk, rw_lnx_g, rw_lnx_b, rw_out, gm_in, gm_vg, gm_vb, gm_ws, gm_bs, gm_out):
    bp, tp, d = x_prompt.shape
    bs_, ts, _ = x_sample.shape
    e = rw_w0.shape[1]
    nh = e // HEAD
    shift = 3 * e + 2 * LORA
    xp = x_prompt.reshape(bp * tp, d)

    zeros_l = jnp.zeros((LORA, e), F32)
    prm = dict(
        mu=rw_mu[0].reshape(1, shift), w0=rw_w0[0].reshape(1, e), a0=rw_a0[0].reshape(1, e),
        w2p=jnp.concatenate([rw_w2[0], zeros_l], axis=0).astype(BF16),
        a2p=jnp.concatenate([zeros_l, rw_a2[0]], axis=0).astype(BF16),
        kk=rw_kk[0].reshape(1, e), ka=rw_ka[0].reshape(1, e), rk=rw_rk[0].reshape(1, e),
        lng=rw_lnx_g[0].reshape(1, e), lnb=rw_lnx_b[0].reshape(1, e))
    w_out = rw_out[0].astype(BF16)

    xs_t = jnp.transpose(x_sample, (1, 0, 2)).reshape(ts * bs_, d)
    proj_s, w_in = _norm_proj(xs_t, norm_g[0], rw_in[0], PROJ_COL_TILES)
    proj_s = proj_s.reshape(ts, bs_, shift + e)

    step_rows = WKV_STEP_CHUNKS * WKV_CHUNK
    assert tp % step_rows == 0 and tp % GM_CHUNK == 0
    ops_p, shift_p = _proj_prep(x_prompt, norm_g[0], w_in, jnp.zeros((bp, shift), F32), prm,
                                chunk=WKV_CHUNK, tm=max(step_rows, _row_tile(tp, PREP_ROWS)), t_tile=step_rows)
    o_p, wkv_p = _wkv_chunks(ops_p, jnp.zeros((bp, nh, HEAD, HEAD), F32), prm["lng"], prm["lnb"],
                             seq_len=tp, chunk=WKV_CHUNK, chunks_per_step=WKV_STEP_CHUNKS,
                             n_seq=_row_tile(bp, WKV_STEP_SEQS))
    x1p = _out_residual(o_p, w_out, xp, _row_tile(bp * tp, OUT_ROWS))

    o_s, wkv_s_t = _wkv_seq(proj_s, state_shift[0], jnp.transpose(state_wkv[0], (1, 2, 3, 0)), prm)
    wkv_s = jnp.transpose(wkv_s_t, (3, 0, 1, 2))
    shift_s = proj_s[ts - 1, :, :shift]
    x1s_t = _out_residual(o_s.reshape(ts * bs_, e), w_out, xs_t, _row_tile(bs_ * ts, OUT_ROWS))
    x1s = jnp.transpose(x1s_t.reshape(ts, bs_, d), (1, 0, 2)).reshape(bs_ * ts, d)

    g_in = gm_in[0].astype(BF16)
    g_out = gm_out[0].astype(BF16)
    wm_p = gm_ws[0]
    reps = GM_CHUNK // ts
    eye = jnp.eye(reps, dtype=F32)
    wm_s = jax.vmap(lambda w: jnp.kron(eye, w))(wm_p[:, :ts, :ts])
    bias_s = jnp.tile(gm_bs[0][:, :ts], (1, reps))

    (y_p,) = _gmlp(x1p, norm_g[1], g_in, gm_vg[0], gm_vb[0], wm_p, gm_bs[0], g_out, norm_f,
                   tm=_row_tile(bp * tp, GMLP_ROWS), emit_v=False)
    y_s, v_s = _gmlp(x1s, norm_g[1], g_in, gm_vg[0], gm_vb[0], wm_s, bias_s, g_out, norm_f,
                     tm=_row_tile(bs_ * ts, GMLP_ROWS_SMALL), emit_v=True)

    return (y_p.reshape(bp, tp, d), y_s.reshape(bs_, ts, d),
            shift_p[None], wkv_p[None], shift_s[None], wkv_s[None],
            v_s.reshape(1, bs_, ts, e))
```

```python
import functools
import math

import jax
import jax.numpy as jnp
from jax import lax
from jax.experimental import pallas as pl
from jax.experimental.pallas import tpu as pltpu

F32 = jnp.float32
BF16 = jnp.bfloat16

LANES = 128
HEAD = 64
QUAD = 4
QW = QUAD * HEAD
HPAIR = 2 * HEAD
assert HPAIR == LANES
LORA = 64
NORM_EPS = 1e-6
LN_EPS = 1e-5
GN_EPS = 64e-5
GM_CHUNK = 128
GM_GROUPS = 8
VMEM_LIMIT = 56 * 1024 * 1024

WKV_CHUNK = 64
WKV_STEP_CHUNKS = 2
WKV_STEP_SEQS = 2
PREP_ROWS = 4 * WKV_CHUNK
PROJ_COL_TILES = 5
OUT_ROWS = 1024
GMLP_ROWS = 512
GMLP_ROWS_SMALL = 256
SUBLANES = 8


def _split2(x):
    hi = x.astype(BF16)
    lo = (x - hi.astype(F32)).astype(BF16)
    return hi, lo


_NN = (((1,), (0,)), ((), ()))
_NT = (((1,), (1,)), ((), ()))
_TN = (((0,), (0,)), ((), ()))


def _mm(a, b, dims=_NN):
    return lax.dot_general(a.astype(BF16), b.astype(BF16), dims, preferred_element_type=F32)


def _mm_sum_rhs(a, b_bf16, dims=_NN):
    d = functools.partial(lax.dot_general, dimension_numbers=dims, preferred_element_type=F32)
    h, l = _split2(a)
    return d(h, b_bf16) + d(l, b_bf16)


def _mm_sum_lhs(a_bf16, b, dims=_NN):
    d = functools.partial(lax.dot_general, dimension_numbers=dims, preferred_element_type=F32)
    h, l = _split2(b)
    return d(a_bf16, h) + d(a_bf16, l)


def _iota(shape, dim):
    return lax.broadcasted_iota(jnp.int32, shape, dim)


def _sigmoid(x):
    return 0.5 * jnp.tanh(0.5 * x) + 0.5


def _rms_rows(x, g):
    return x * lax.rsqrt(jnp.mean(x * x, axis=-1, keepdims=True) + NORM_EPS) * g


def _head_ones(width):
    return ((_iota((width, width), 0) // HEAD) == (_iota((width, width), 1) // HEAD)).astype(BF16)


def _norm_proj_kernel(x_ref, g_ref, w_ref, o_ref, w16_ref, h_ref):
    @pl.when(pl.program_id(0) == 0)
    def _norm():
        h_ref[...] = _rms_rows(x_ref[...], g_ref[...]).astype(BF16)

    w16 = w_ref[...].astype(BF16)
    w16_ref[...] = w16
    o_ref[...] = jnp.dot(h_ref[...], w16, preferred_element_type=F32)


def _norm_proj(x2d, g, w_f32, col_tiles):
    n, d = x2d.shape
    cols = w_f32.shape[1]
    tn = cols // col_tiles
    assert cols % col_tiles == 0 and tn % LANES == 0
    return pl.pallas_call(
        _norm_proj_kernel,
        grid=(col_tiles,),
        in_specs=[pl.BlockSpec((n, d), lambda j: (0, 0)),
                  pl.BlockSpec((1, d), lambda j: (0, 0)),
                  pl.BlockSpec((d, tn), lambda j: (0, j))],
        out_specs=[pl.BlockSpec((n, tn), lambda j: (0, j)),
                   pl.BlockSpec((d, tn), lambda j: (0, j))],
        out_shape=[jax.ShapeDtypeStruct((n, cols), F32), jax.ShapeDtypeStruct((d, cols), BF16)],
        scratch_shapes=[pltpu.VMEM((n, d), BF16)],
        compiler_params=pltpu.CompilerParams(dimension_semantics=("arbitrary",), vmem_limit_bytes=VMEM_LIMIT),
        name="norm_proj",
    )(x2d, g.reshape(1, d), w_f32)


_OPS_MM = ("at", "rt", "bt", "kt", "v", "bg", "kg")
_OPS_T = ("bt", "kt")
_OPS_OUT = ("bonus", "gate")


def _proj_prep_kernel(x_ref, g_ref, w_ref, mu_ref, w0_ref, w2_ref, a0_ref, a2_ref, kk_ref, ka_ref, rk_ref,
                      at_ref, rt_ref, bt_ref, kt_ref, v_ref, bg_ref, kg_ref, bonus_ref, gate_ref, gt_ref, sho_ref,
                      carry_ref, *, chunk):
    j = pl.program_id(1)
    tm = x_ref.shape[0]
    E = w0_ref.shape[1]
    SHIFT = 3 * E + 2 * LORA
    C = chunk
    n_ch = tm // C
    outs = dict(zip(_OPS_MM + _OPS_OUT,
                    (at_ref, rt_ref, bt_ref, kt_ref, v_ref, bg_ref, kg_ref, bonus_ref, gate_ref)))

    @pl.when(j == 0)
    def _init():
        carry_ref[...] = jnp.zeros(carry_ref.shape, F32)

    h = _rms_rows(x_ref[...], g_ref[...]).astype(BF16)
    first_row = _iota((tm, 1), 0) == 0
    ones_bd = _head_ones(QW)
    row = _iota((tm, tm), 0)
    col = _iota((tm, tm), 1)
    tri = ((row >= col) & (row // C == col // C)).astype(BF16)

    def proj(lo, hi):
        return jnp.dot(h, w_ref[:, lo:hi], preferred_element_type=F32)

    def shifted(lo, hi):
        sh = proj(lo, hi)
        prev = jnp.where(first_row, carry_ref[:, lo:hi], pltpu.roll(sh, 1, 0))
        carry_ref[:, lo:hi] = sh[tm - 1:tm]
        return sh + (prev - sh) * mu_ref[:, lo:hi]

    def per_chunk_last(x):
        return jnp.concatenate(
            [jnp.broadcast_to(x[c * C + C - 1:c * C + C], (C, x.shape[1])) for c in range(n_ch)], axis=0)

    x_lora = shifted(3 * E, SHIFT)
    lane_l = _iota((tm, 2 * LORA), 1)
    t_lora = jnp.where(lane_l < LORA, jnp.tanh(x_lora), x_lora)
    wf = w0_ref[...] + _mm(t_lora, w2_ref[...])
    af = a0_ref[...] + _mm(x_lora, a2_ref[...])

    for q in range(E // QW):
        lo = q * QW
        sl = slice(lo, lo + QW)
        r = shifted(lo, lo + QW)
        k = shifted(E + lo, E + lo + QW)
        v = shifted(2 * E + lo, 2 * E + lo + QW)
        z = proj(SHIFT + lo, SHIFT + lo + QW)
        ld = -math.exp(-0.5) * _sigmoid(wf[:, sl])
        a = _sigmoid(af[:, sl])
        kk0 = k * kk_ref[:, sl]
        k2 = k * (1.0 + (a - 1.0) * ka_ref[:, sl])
        sums = _mm(jnp.concatenate([kk0 * kk0, r * k2 * rk_ref[:, sl]], axis=0), ones_bd)
        kk = kk0 * jnp.minimum(lax.rsqrt(sums[:tm]), 1e12)
        cum = _mm_sum_lhs(tri, ld)
        cum_last = per_chunk_last(cum)
        g_rem = jnp.exp(cum_last - cum)
        g_inv = jnp.exp(-cum)
        kka = kk * a
        quad = dict(at=-kk * jnp.exp(cum - ld), rt=r * jnp.exp(cum), bt=kka * g_inv, kt=k2 * g_inv, v=v,
                    bg=kka * g_rem, kg=k2 * g_rem, bonus=sums[tm:] * v, gate=z * _sigmoid(z))
        for n, x in quad.items():
            if n in _OPS_T:
                xt = x.astype(BF16).T
                tt = outs[n].shape[2]
                for i in range(tm // tt):
                    outs[n][i, sl, :] = xt[:, i * tt:(i + 1) * tt]
            else:
                outs[n][:, sl] = x.astype(outs[n].dtype)
        for c in range(n_ch):
            gt_ref[c, :, sl] = jnp.exp(cum[c * C + C - 1:c * C + C])

    @pl.when(j == pl.num_programs(1) - 1)
    def _shift_out():
        sho_ref[0] = carry_ref[...]


def _proj_prep(x3d, g, w_bf16, prm, *, chunk, tm, t_tile):
    b, t, d = x3d.shape
    e = prm["w0"].shape[1]
    shift = 3 * e + 2 * LORA
    cols = w_bf16.shape[1]
    assert t % tm == 0 and tm % t_tile == 0 and t_tile % chunk == 0 and e % QW == 0
    steps = t // tm
    n_ch = tm // chunk
    c2 = lambda shape: pl.BlockSpec(shape, lambda i, j: (0, 0))
    rows = pl.BlockSpec((tm, e), lambda i, j: (i * steps + j, 0))
    cols_t = pl.BlockSpec((tm // t_tile, e, t_tile), lambda i, j: (i * steps + j, 0, 0))
    outs = pl.pallas_call(
        functools.partial(_proj_prep_kernel, chunk=chunk),
        grid=(b, steps),
        in_specs=[pl.BlockSpec((tm, d), lambda i, j: (i * steps + j, 0)),
                  c2((1, d)),
                  pl.BlockSpec((d, cols), lambda i, j: (0, 0), pipeline_mode=pl.Buffered(1)),
                  c2((1, shift)), c2((1, e)), c2((2 * LORA, e)), c2((1, e)), c2((2 * LORA, e)),
                  c2((1, e)), c2((1, e)), c2((1, e))],
        out_specs=([cols_t if n in _OPS_T else rows for n in _OPS_MM + _OPS_OUT]
                   + [pl.BlockSpec((n_ch, 1, e), lambda i, j: (i * steps + j, 0, 0)),
                      pl.BlockSpec((1, 1, shift), lambda i, j: (i, 0, 0))]),
        out_shape=([jax.ShapeDtypeStruct((b * t // t_tile, e, t_tile) if n in _OPS_T else (b * t, e), BF16)
                    for n in _OPS_MM + _OPS_OUT]
                   + [jax.ShapeDtypeStruct((b * t // chunk, 1, e), F32),
                      jax.ShapeDtypeStruct((b, 1, shift), F32)]),
        scratch_shapes=[pltpu.VMEM((1, shift), F32)],
        compiler_params=pltpu.CompilerParams(dimension_semantics=("arbitrary", "arbitrary"),
                                             vmem_limit_bytes=VMEM_LIMIT),
        name="proj_prep",
    )(x3d.reshape(b * t, d), g.reshape(1, d), w_bf16,
      prm["mu"], prm["w0"], prm["w2p"], prm["a0"], prm["a2p"], prm["kk"], prm["ka"], prm["rk"])
    ops = dict(zip(_OPS_MM + _OPS_OUT + ("g_tot",), outs[:-1]))
    return ops, outs[-1].reshape(b, shift)


def _headsum(xs, ones_bd, split):
    e = xs[0].shape[1]
    nq = e // QW
    rows = [x[:, q * QW:(q + 1) * QW] for x in xs for q in range(nq)]
    stacked = jnp.concatenate(rows, axis=0)
    s = _mm_sum_rhs(stacked, ones_bd) if split else _mm(stacked, ones_bd)
    c = xs[0].shape[0]
    outs = []
    for i in range(len(xs)):
        outs.append(jnp.concatenate([s[(i * nq + q) * c:(i * nq + q + 1) * c] for q in range(nq)], axis=1))
    return outs


def _wkv_core(ops, states, lng, lnb, *, n_levels):
    C, E = ops["at"].shape
    TL = QUAD * C
    NQ = E // QW
    mask_bd = (_iota((QW, QW), 0) // HEAD) == (_iota((QW, QW), 1) // HEAD)
    ones_bd = mask_bd.astype(BF16)
    tok_t = _iota((C, TL), 0)
    tok_j = _iota((C, TL), 1) % C
    strict = tok_j < tok_t
    incl = tok_j <= tok_t
    eye_all = (tok_j == tok_t).astype(F32)

    def block_diag(x, width):
        assert 2 * width == LANES
        x = x.astype(BF16)
        zero = jnp.zeros((), BF16)
        half = _iota((C, LANES), 1) < width
        pad = jnp.zeros((C, LANES), BF16)
        rows = []
        for h in range(QUAD):
            t = h // 2
            tile = jnp.where(half if h % 2 == 0 else ~half, x[:, t * LANES:(t + 1) * LANES], zero)
            rows.append(jnp.concatenate([tile if i == t else pad for i in range(QUAD // 2)], axis=1))
        return jnp.concatenate(rows, axis=0)

    def cols_bd(xt):
        zero = jnp.zeros((HEAD, C), xt.dtype)
        return jnp.concatenate(
            [jnp.concatenate([xt[h * HEAD:(h + 1) * HEAD] if i == h else zero for i in range(QUAD)], axis=1)
             for h in range(QUAD)], axis=0)

    stack_hc = functools.partial(block_diag, width=HEAD)
    stack_tt = functools.partial(block_diag, width=C)

    at, rt, bt, kt, v, bg, kg = (ops[n] for n in _OPS_MM)
    quads = range(NQ)
    sls = [slice(q * QW, (q + 1) * QW) for q in quads]
    ars = [jnp.concatenate([at[:, sl], rt[:, sl]], axis=0) for sl in sls]
    v_sts = [stack_hc(v[:, sl]) for sl in sls]
    abs_ = [_mm(ars[q], cols_bd(bt[sls[q], :])) for q in quads]
    aks = [_mm(ars[q], cols_bd(kt[sls[q], :])) for q in quads]
    pws = [jnp.where(strict, abs_[q][:C], 0.0) for q in quads]
    a_rbs = [jnp.where(incl, abs_[q][C:], 0.0) for q in quads]
    a_aks = [jnp.where(strict, aks[q][:C], 0.0) for q in quads]
    a_rks = [jnp.where(incl, aks[q][C:], 0.0) for q in quads]
    invs = [eye_all + pws[q] for q in quads]
    pws = [_mm(pws[q], stack_tt(pws[q])) for q in quads]
    for _ in range(n_levels - 1):
        bds = [stack_tt(pws[q]) for q in quads]
        res = [_mm(jnp.concatenate([pws[q], invs[q]], axis=0), bds[q]) for q in quads]
        pws = [res[q][:C] for q in quads]
        invs = [invs[q] + res[q][C:] for q in quads]
    invs = [invs[q] + _mm(invs[q], stack_tt(pws[q])) for q in quads]
    arss = [_mm(ars[q], states[q], _NT) for q in quads]
    avs = [_mm(jnp.concatenate([a_aks[q], a_rks[q]], axis=0), v_sts[q]) for q in quads]
    ws = [arss[q][:C] + avs[q][:C] for q in quads]
    us = [_mm(invs[q], stack_hc(ws[q])) for q in quads]
    ys = [arss[q][C:] + avs[q][C:] + _mm(a_rbs[q], stack_hc(us[q])) for q in quads]
    new_states = []
    for q in quads:
        uv = jnp.concatenate([us[q].astype(BF16), v[:, sls[q]]], axis=0)
        bk = jnp.concatenate([bg[:, sls[q]], kg[:, sls[q]]], axis=0)
        new_states.append(states[q] * ops["g_tot"][:, sls[q]] + jnp.where(mask_bd, _mm(uv, bk, _TN), 0.0))
    y = jnp.concatenate(ys, axis=1)

    (ysum,) = _headsum([y], ones_bd, split=True)
    yc = y - ysum * (1.0 / HEAD)
    (vsum,) = _headsum([yc * yc], ones_bd, split=False)
    yn = yc * lax.rsqrt(vsum * (1.0 / HEAD) + GN_EPS) * lng + lnb
    return (yn + ops["bonus"]) * ops["gate"], new_states


def _wkv_chunks_kernel(at_ref, rt_ref, bt_ref, kt_ref, v_ref, bg_ref, kg_ref, bonus_ref, gate_ref, gt_ref,
                       lng_ref, lnb_ref, o_ref, so_ref, st_ref, *, chunk, n_levels):
    j = pl.program_id(1)
    n_seq = at_ref.shape[1]
    NQ = st_ref.shape[0] // n_seq
    E = NQ * QW
    seqs = range(n_seq)
    refs = dict(zip(_OPS_MM + _OPS_OUT,
                    (at_ref, rt_ref, bt_ref, kt_ref, v_ref, bg_ref, kg_ref, bonus_ref, gate_ref)))

    @pl.when(j == 0)
    def _init():
        st_ref[...] = jnp.zeros(st_ref.shape, F32)

    lng = jnp.concatenate([lng_ref[...]] * n_seq, axis=1)
    lnb = jnp.concatenate([lnb_ref[...]] * n_seq, axis=1)
    states = [st_ref[q] for q in range(n_seq * NQ)]
    for c in range(at_ref.shape[2] // chunk):
        rows = slice(c * chunk, (c + 1) * chunk)
        ops = {}
        for n, ref in refs.items():
            if n in _OPS_T:
                ops[n] = jnp.concatenate([ref[0, s, 0, :, rows] for s in seqs], axis=0)
            else:
                ops[n] = jnp.concatenate([ref[0, s, rows, :] for s in seqs], axis=1)
        ops["g_tot"] = jnp.concatenate([gt_ref[0, s, c] for s in seqs], axis=1)
        o, states = _wkv_core(ops, states, lng, lnb, n_levels=n_levels)
        for s in seqs:
            o_ref[0, s, rows, :] = o[:, s * E:(s + 1) * E].astype(o_ref.dtype)
    for q in range(n_seq * NQ):
        st_ref[q] = states[q]

    @pl.when(j == pl.num_programs(1) - 1)
    def _state_out():
        for s in seqs:
            for q in range(NQ):
                for h in range(QUAD):
                    so_ref[0, s, QUAD * q + h] = states[s * NQ + q][h * HEAD:(h + 1) * HEAD, h * HEAD:(h + 1) * HEAD]


def _wkv_chunks(ops, lng, lnb, *, seq_len, chunk, chunks_per_step, n_seq):
    n, e = ops["at"].shape
    b = n // seq_len
    nh = e // HEAD
    tm = chunk * chunks_per_step
    steps = seq_len // tm
    assert ops["bt"].shape[2] == tm
    assert b % n_seq == 0
    g = b // n_seq
    rows = pl.BlockSpec((1, n_seq, tm, e), lambda i, j: (i, 0, j, 0))
    cols_t = pl.BlockSpec((1, n_seq, 1, e, tm), lambda i, j: (i, 0, j, 0, 0))
    c2 = lambda shape: pl.BlockSpec(shape, lambda i, j: (0, 0))
    sblk = pl.BlockSpec((1, n_seq, nh, HEAD, HEAD), lambda i, j: (i, 0, 0, 0, 0))
    n_levels = max(1, math.ceil(math.log2(chunk)) - 1)
    args = [ops[n_].reshape((g, n_seq, steps, e, tm) if n_ in _OPS_T else (g, n_seq, seq_len, e))
            for n_ in _OPS_MM + _OPS_OUT]
    o, so = pl.pallas_call(
        functools.partial(_wkv_chunks_kernel, chunk=chunk, n_levels=n_levels),
        grid=(g, steps),
        in_specs=([cols_t if n_ in _OPS_T else rows for n_ in _OPS_MM + _OPS_OUT]
                  + [pl.BlockSpec((1, n_seq, chunks_per_step, 1, e), lambda i, j: (i, 0, j, 0, 0)),
                     c2((1, e)), c2((1, e))]),
        out_specs=[rows, sblk],
        out_shape=[jax.ShapeDtypeStruct((g, n_seq, seq_len, e), BF16),
                   jax.ShapeDtypeStruct((g, n_seq, nh, HEAD, HEAD), F32)],
        scratch_shapes=[pltpu.VMEM((n_seq * (e // QW), QW, QW), F32)],
        compiler_params=pltpu.CompilerParams(dimension_semantics=("arbitrary", "arbitrary"),
                                             vmem_limit_bytes=VMEM_LIMIT),
        name="wkv_chunks",
    )(*args, ops["g_tot"].reshape(g, n_seq, seq_len // chunk, 1, e), lng, lnb)
    return o.reshape(n, e), so.reshape(b, nh, HEAD, HEAD)


_VT = ("nkk", "w", "b", "k", "r", "v")


def _wkv_seq_kernel(pr_ref, pk_ref, pv_ref, pl_ref, pz_ref, sr_ref, sk_ref, sv_ref, sl_ref,
                    mur_ref, muk_ref, muv_ref, mul_ref, w0_ref, w2_ref, a0_ref, a2_ref, kk_ref, ka_ref, rk_ref,
                    lng_ref, lnb_ref, s_ref, o_ref, so_ref, vt_ref, yt_ref):
    T, B, _ = pr_ref.shape
    ones2 = _head_ones(HPAIR)

    def shifted(p_ref, prev_ref, mu_ref):
        sh = p_ref[...].reshape(T * B, HPAIR)
        prev = jnp.concatenate([prev_ref[...], sh[:(T - 1) * B]], axis=0)
        return sh + (prev - sh) * mu_ref[...]

    r = shifted(pr_ref, sr_ref, mur_ref)
    k = shifted(pk_ref, sk_ref, muk_ref)
    v = shifted(pv_ref, sv_ref, muv_ref)
    x_lora = shifted(pl_ref, sl_ref, mul_ref)
    z = pz_ref[...].reshape(T * B, HPAIR)
    lane_l = _iota((T * B, 2 * LORA), 1)
    t_lora = jnp.where(lane_l < LORA, jnp.tanh(x_lora), x_lora)
    wf = w0_ref[...] + _mm(t_lora, w2_ref[...])
    af = a0_ref[...] + _mm(x_lora, a2_ref[...])
    w = jnp.exp(-math.exp(-0.5) * _sigmoid(wf))
    a = _sigmoid(af)
    kk0 = k * kk_ref[...]
    k2 = k * (1.0 + (a - 1.0) * ka_ref[...])
    sums = _mm_sum_rhs(jnp.concatenate([kk0 * kk0, r * k2 * rk_ref[...]], axis=0), ones2)
    kk = kk0 * jnp.minimum(lax.rsqrt(sums[:T * B]), 1e12)
    bonus = sums[T * B:] * v
    vecs = dict(nkk=-kk, w=w, b=kk * a, k=k2, r=r, v=v)
    for i, n in enumerate(_VT):
        for t in range(T):
            vt_ref[i, t] = vecs[n][t * B:(t + 1) * B].T

    for h in range(2):
        ch = slice(h * HEAD, (h + 1) * HEAD)

        def body(g, carry, h=h, ch=ch):
            v0 = pl.multiple_of(g * SUBLANES, SUBLANES)
            vrows = [vt_ref[5, t, pl.ds(h * HEAD + v0, SUBLANES), :] for t in range(T)]
            ys = [[] for _ in range(T)]
            for u in range(SUBLANES):
                s = s_ref[h, v0 + u]
                for t in range(T):
                    sa = jnp.sum(s * vt_ref[0, t, ch, :], axis=0, keepdims=True)
                    s = s * vt_ref[1, t, ch, :] + sa * vt_ref[2, t, ch, :] + vrows[t][u:u + 1] * vt_ref[3, t, ch, :]
                    ys[t].append(jnp.sum(s * vt_ref[4, t, ch, :], axis=0, keepdims=True))
                so_ref[h, v0 + u] = s
            for t in range(T):
                yt_ref[t, pl.ds(h * HEAD + v0, SUBLANES), :] = jnp.concatenate(ys[t], axis=0)
            return carry

        lax.fori_loop(0, HEAD // SUBLANES, body, 0)

    y = jnp.concatenate([yt_ref[t].T for t in range(T)], axis=0)
    yc = y - _mm_sum_rhs(y, ones2) * (1.0 / HEAD)
    var = _mm_sum_rhs(yc * yc, ones2) * (1.0 / HEAD)
    yn = yc * lax.rsqrt(var + GN_EPS) * lng_ref[...] + lnb_ref[...]
    o_ref[...] = ((yn + bonus) * (z * _sigmoid(z))).reshape(T, B, HPAIR)


def _wkv_seq(proj_t, shift_prev, s_t, prm):
    t, b, cols = proj_t.shape
    e = prm["w0"].shape[1]
    nh = e // HEAD
    kb = e // HPAIR
    bt = LANES
    assert b % bt == 0 and nh % 2 == 0
    off = dict(r=0, k=kb, v=2 * kb, l=3 * kb)
    pspec = lambda o: pl.BlockSpec((t, bt, HPAIR), lambda i, j, o=o: (0, j, o + i))
    pfix = lambda o: pl.BlockSpec((t, bt, HPAIR), lambda i, j, o=o: (0, j, o))
    sspec = lambda o: pl.BlockSpec((bt, HPAIR), lambda i, j, o=o: (j, o + i))
    sfix = lambda o: pl.BlockSpec((bt, HPAIR), lambda i, j, o=o: (j, o))
    mspec = lambda o: pl.BlockSpec((1, HPAIR), lambda i, j, o=o: (0, o + i))
    mfix = lambda o: pl.BlockSpec((1, HPAIR), lambda i, j, o=o: (0, o))
    row = pl.BlockSpec((1, HPAIR), lambda i, j: (0, i))
    lora = pl.BlockSpec((2 * LORA, HPAIR), lambda i, j: (0, i))
    sblk = pl.BlockSpec((2, HEAD, HEAD, bt), lambda i, j: (i, 0, 0, j))
    return pl.pallas_call(
        _wkv_seq_kernel,
        grid=(nh // 2, b // bt),
        in_specs=[pspec(off["r"]), pspec(off["k"]), pspec(off["v"]), pfix(off["l"]), pspec(off["l"] + 1),
                  sspec(off["r"]), sspec(off["k"]), sspec(off["v"]), sfix(off["l"]),
                  mspec(off["r"]), mspec(off["k"]), mspec(off["v"]), mfix(off["l"]),
                  row, lora, row, lora, row, row, row, row, row, sblk],
        out_specs=[pl.BlockSpec((t, bt, HPAIR), lambda i, j: (0, j, i)), sblk],
        out_shape=[jax.ShapeDtypeStruct((t, b, e), F32), jax.ShapeDtypeStruct(s_t.shape, F32)],
        scratch_shapes=[pltpu.VMEM((len(_VT), t, HPAIR, bt), F32),
                        pltpu.VMEM((t, HPAIR, bt), F32)],
        compiler_params=pltpu.CompilerParams(dimension_semantics=("arbitrary", "arbitrary"),
                                             vmem_limit_bytes=VMEM_LIMIT),
        name="wkv_seq",
    )(proj_t, proj_t, proj_t, proj_t, proj_t, shift_prev, shift_prev, shift_prev, shift_prev,
      prm["mu"], prm["mu"], prm["mu"], prm["mu"], prm["w0"], prm["w2p"], prm["a0"], prm["a2p"],
      prm["kk"], prm["ka"], prm["rk"], prm["lng"], prm["lnb"], s_t)


def _out_residual_kernel(o_ref, w_ref, x_ref, y_ref):
    y_ref[...] = x_ref[...] + jnp.dot(o_ref[...].astype(BF16), w_ref[...], preferred_element_type=F32)


def _out_residual(o2d, w_bf16, x2d, tm):
    n, e = o2d.shape
    d = x2d.shape[1]
    return pl.pallas_call(
        _out_residual_kernel,
        grid=(n // tm,),
        in_specs=[pl.BlockSpec((tm, e), lambda i: (i, 0)),
                  pl.BlockSpec((e, d), lambda i: (0, 0)),
                  pl.BlockSpec((tm, d), lambda i: (i, 0))],
        out_specs=pl.BlockSpec((tm, d), lambda i: (i, 0)),
        out_shape=jax.ShapeDtypeStruct((n, d), F32),
        compiler_params=pltpu.CompilerParams(dimension_semantics=("arbitrary",), vmem_limit_bytes=VMEM_LIMIT),
        name="out_residual",
    )(o2d, w_bf16, x2d)


def _gmlp_kernel(x_ref, g_ref, win_ref, vg_ref, vb_ref, wm_ref, bs_ref, wout_ref, nf_ref, *out_refs, emit_v):
    y_ref = out_refs[0]
    tm = x_ref.shape[0]
    e = vg_ref.shape[1]
    gd = e // GM_GROUPS
    x = x_ref[...]
    h = _rms_rows(x, g_ref[...]).astype(BF16)
    proj = jnp.dot(h, win_ref[...], preferred_element_type=F32)
    u = jax.nn.gelu(proj[:, :e])
    vf = jax.nn.gelu(proj[:, e:2 * e])
    z = proj[:, 2 * e:]
    vm = jnp.mean(vf, axis=-1, keepdims=True)
    vc = vf - vm
    vv = jnp.mean(vc * vc, axis=-1, keepdims=True)
    vn = vc * lax.rsqrt(vv + LN_EPS) * vg_ref[...] + vb_ref[...]
    if emit_v:
        out_refs[1][...] = vn
    vn16 = vn.astype(BF16)
    causal = _iota((GM_CHUNK, GM_CHUNK), 0) >= _iota((GM_CHUNK, GM_CHUNK), 1)
    rows = []
    for j in range(tm // GM_CHUNK):
        cols = []
        for gi in range(GM_GROUPS):
            wm = jnp.where(causal, wm_ref[gi], 0.0).astype(BF16)
            blk = vn16[j * GM_CHUNK:(j + 1) * GM_CHUNK, gi * gd:(gi + 1) * gd]
            cols.append(jnp.dot(wm, blk, preferred_element_type=F32) + bs_ref[gi])
        rows.append(jnp.concatenate(cols, axis=1))
    mixed = jnp.concatenate(rows, axis=0)
    o = u * mixed * (z * _sigmoid(z))
    x2 = x + jnp.dot(o.astype(BF16), wout_ref[...], preferred_element_type=F32)
    y_ref[...] = _rms_rows(x2, nf_ref[...])


def _gmlp(x2d, g, win_bf16, vg, vb, wmix, bias, wout_bf16, nf, *, tm, emit_v):
    n, d = x2d.shape
    e = vg.shape[0]
    const2 = lambda shape: pl.BlockSpec(shape, lambda i: (0, 0))
    out_shape = [jax.ShapeDtypeStruct((n, d), F32)]
    out_specs = [pl.BlockSpec((tm, d), lambda i: (i, 0))]
    if emit_v:
        out_shape.append(jax.ShapeDtypeStruct((n, e), F32))
        out_specs.append(pl.BlockSpec((tm, e), lambda i: (i, 0)))
    outs = pl.pallas_call(
        functools.partial(_gmlp_kernel, emit_v=emit_v),
        grid=(n // tm,),
        in_specs=[pl.BlockSpec((tm, d), lambda i: (i, 0)),
                  const2((1, d)),
                  pl.BlockSpec((d, 3 * e), lambda i: (0, 0), pipeline_mode=pl.Buffered(1)),
                  const2((1, e)), const2((1, e)),
                  pl.BlockSpec((GM_GROUPS, GM_CHUNK, GM_CHUNK), lambda i: (0, 0, 0)),
                  pl.BlockSpec((GM_GROUPS, GM_CHUNK, 1), lambda i: (0, 0, 0)),
                  pl.BlockSpec((e, d), lambda i: (0, 0), pipeline_mode=pl.Buffered(1)),
                  const2((1, d))],
        out_specs=out_specs,
        out_shape=out_shape,
        compiler_params=pltpu.CompilerParams(dimension_semantics=("arbitrary",), vmem_limit_bytes=VMEM_LIMIT),
        name="gmlp",
    )(x2d, g.reshape(1, d), win_bf16, vg.reshape(1, e), vb.reshape(1, e), wmix,
      bias.reshape(GM_GROUPS, GM_CHUNK, 1), wout_bf16, nf.reshape(1, d))
    return outs


def _row_tile(n, pref):
    t = pref
    while n % t:
        t //= 2
    return t


def kernel(x_prompt, x_sample, state_shift, state_wkv, norm_g, norm_f, rw_in, rw_mu, rw_w0, rw_w2, rw_a0, rw_a2, rw_kk, rw_ka, rw_rk, rw_lnx_g, rw_lnx_b, rw_out, gm_in, gm_vg, gm_vb, gm_ws, gm_bs, gm_out):
    bp, tp, d = x_prompt.shape
    bs_, ts, _ = x_sample.shape
    e = rw_w0.shape[1]
    shift = 3 * e + 2 * LORA
    xp = x_prompt.reshape(bp * tp, d)

    zeros_l = jnp.zeros((LORA, e), F32)
    prm = dict(
        mu=rw_mu[0].reshape(1, shift), w0=rw_w0[0].reshape(1, e), a0=rw_a0[0].reshape(1, e),
        w2p=jnp.concatenate([rw_w2[0], zeros_l], axis=0).astype(BF16),
        a2p=jnp.concatenate([zeros_l, rw_a2[0]], axis=0).astype(BF16),
        kk=rw_kk[0].reshape(1, e), ka=rw_ka[0].reshape(1, e), rk=rw_rk[0].reshape(1, e),
        lng=rw_lnx_g[0].reshape(1, e), lnb=rw_lnx_b[0].reshape(1, e))
    w_out = rw_out[0].astype(BF16)

    xs_t = jnp.transpose(x_sample, (1, 0, 2)).reshape(ts * bs_, d)
    proj_s, w_in = _norm_proj(xs_t, norm_g[0], rw_in[0], PROJ_COL_TILES)
    proj_s = proj_s.reshape(ts, bs_, shift + e)

    step_rows = WKV_STEP_CHUNKS * WKV_CHUNK
    assert tp % step_rows == 0 and tp % GM_CHUNK == 0
    ops_p, shift_p = _proj_prep(x_prompt, norm_g[0], w_in, prm,
                                chunk=WKV_CHUNK, tm=max(step_rows, _row_tile(tp, PREP_ROWS)), t_tile=step_rows)
    o_p, wkv_p = _wkv_chunks(ops_p, prm["lng"], prm["lnb"],
                             seq_len=tp, chunk=WKV_CHUNK, chunks_per_step=WKV_STEP_CHUNKS,
                             n_seq=_row_tile(bp, WKV_STEP_SEQS))
    x1p = _out_residual(o_p, w_out, xp, _row_tile(bp * tp, OUT_ROWS))

    o_s, wkv_s_t = _wkv_seq(proj_s, state_shift[0], jnp.transpose(state_wkv[0], (1, 2, 3, 0)), prm)
    wkv_s = jnp.transpose(wkv_s_t, (3, 0, 1, 2))
    shift_s = proj_s[ts - 1, :, :shift]
    x1s_t = _out_residual(o_s.reshape(ts * bs_, e), w_out, xs_t, _row_tile(bs_ * ts, OUT_ROWS))
    x1s = jnp.transpose(x1s_t.reshape(ts, bs_, d), (1, 0, 2)).reshape(bs_ * ts, d)

    g_in = gm_in[0].astype(BF16)
    g_out = gm_out[0].astype(BF16)
    wm_p = gm_ws[0]
    reps = GM_CHUNK // ts
    eye = jnp.eye(reps, dtype=F32)
    wm_s = jax.vmap(lambda w: jnp.kron(eye, w))(wm_p[:, :ts, :ts])
    bias_s = jnp.tile(gm_bs[0][:, :ts], (1, reps))

    (y_p,) = _gmlp(x1p, norm_g[1], g_in, gm_vg[0], gm_vb[0], wm_p, gm_bs[0], g_out, norm_f,
                   tm=_row_tile(bp * tp, GMLP_ROWS), emit_v=False)
    y_s, v_s = _gmlp(x1s, norm_g[1], g_in, gm_vg[0], gm_vb[0], wm_s, bias_s, g_out, norm_f,
                     tm=_row_tile(bs_ * ts, GMLP_ROWS_SMALL), emit_v=True)

    return (y_p.reshape(bp, tp, d), y_s.reshape(bs_, ts, d),
            shift_p[None], wkv_p[None], shift_s[None], wkv_s[None],
            v_s.reshape(1, bs_, ts, e))
```

```python
import functools
import math

import jax
import jax.numpy as jnp
from jax import lax
from jax.experimental import pallas as pl
from jax.experimental.pallas import tpu as pltpu

F32 = jnp.float32
BF16 = jnp.bfloat16

LANES = 128
HEAD = 64
QUAD = 4
QW = QUAD * HEAD
HPAIR = 2 * HEAD
assert HPAIR == LANES
LORA = 64
NORM_EPS = 1e-6
LN_EPS = 1e-5
GN_EPS = 64e-5
GM_CHUNK = 128
GM_GROUPS = 8
VMEM_LIMIT = 56 * 1024 * 1024

WKV_CHUNK = 64
WKV_STEP_CHUNKS = 2
WKV_STEP_SEQS = 2
PREP_ROWS = 4 * WKV_CHUNK
PROJ_COL_TILES = 5
OUT_ROWS = 1024
GMLP_ROWS = 512
GMLP_ROWS_SMALL = 256
SUBLANES = 8


def _split2(x):
    hi = x.astype(BF16)
    lo = (x - hi.astype(F32)).astype(BF16)
    return hi, lo


_NN = (((1,), (0,)), ((), ()))
_NT = (((1,), (1,)), ((), ()))
_TN = (((0,), (0,)), ((), ()))


def _mm(a, b, dims=_NN):
    return lax.dot_general(a.astype(BF16), b.astype(BF16), dims, preferred_element_type=F32)


def _mm_sum_rhs(a, b_bf16, dims=_NN):
    d = functools.partial(lax.dot_general, dimension_numbers=dims, preferred_element_type=F32)
    h, l = _split2(a)
    return d(h, b_bf16) + d(l, b_bf16)


def _mm_sum_lhs(a_bf16, b, dims=_NN):
    d = functools.partial(lax.dot_general, dimension_numbers=dims, preferred_element_type=F32)
    h, l = _split2(b)
    return d(a_bf16, h) + d(a_bf16, l)


def _iota(shape, dim):
    return lax.broadcasted_iota(jnp.int32, shape, dim)


def _sigmoid(x):
    return 0.5 * jnp.tanh(0.5 * x) + 0.5


def _rms_rows(x, g):
    return x * lax.rsqrt(jnp.mean(x * x, axis=-1, keepdims=True) + NORM_EPS) * g


def _head_ones(width):
    return ((_iota((width, width), 0) // HEAD) == (_iota((width, width), 1) // HEAD)).astype(BF16)


def _norm_proj_kernel(x_ref, g_ref, w_ref, o_ref, w16_ref, h_ref):
    @pl.when(pl.program_id(0) == 0)
    def _norm():
        h_ref[...] = _rms_rows(x_ref[...], g_ref[...]).astype(BF16)

    w16 = w_ref[...].astype(BF16)
    w16_ref[...] = w16
    o_ref[...] = jnp.dot(h_ref[...], w16, preferred_element_type=F32)


def _norm_proj(x2d, g, w_f32, col_tiles):
    n, d = x2d.shape
    cols = w_f32.shape[1]
    tn = cols // col_tiles
    assert cols % col_tiles == 0 and tn % LANES == 0
    return pl.pallas_call(
        _norm_proj_kernel,
        grid=(col_tiles,),
        in_specs=[pl.BlockSpec((n, d), lambda j: (0, 0)),
                  pl.BlockSpec((1, d), lambda j: (0, 0)),
                  pl.BlockSpec((d, tn), lambda j: (0, j))],
        out_specs=[pl.BlockSpec((n, tn), lambda j: (0, j)),
                   pl.BlockSpec((d, tn), lambda j: (0, j))],
        out_shape=[jax.ShapeDtypeStruct((n, cols), F32), jax.ShapeDtypeStruct((d, cols), BF16)],
        scratch_shapes=[pltpu.VMEM((n, d), BF16)],
        compiler_params=pltpu.CompilerParams(dimension_semantics=("arbitrary",), vmem_limit_bytes=VMEM_LIMIT),
        name="norm_proj",
    )(x2d, g.reshape(1, d), w_f32)


_OPS_MM = ("at", "rt", "bt", "kt", "v", "bg", "kg")
_OPS_T = ("bt", "kt")
_OPS_OUT = ("bonus", "gate")


def _proj_prep_kernel(x_ref, g_ref, w_ref, mu_ref, w0_ref, w2_ref, a0_ref, a2_ref, kk_ref, ka_ref, rk_ref,
                      at_ref, rt_ref, bt_ref, kt_ref, v_ref, bg_ref, kg_ref, bonus_ref, gate_ref, gt_ref, sho_ref,
                      carry_ref, *, chunk):
    j = pl.program_id(1)
    tm = x_ref.shape[0]
    E = w0_ref.shape[1]
    SHIFT = 3 * E + 2 * LORA
    C = chunk
    n_ch = tm // C
    outs = dict(zip(_OPS_MM + _OPS_OUT,
                    (at_ref, rt_ref, bt_ref, kt_ref, v_ref, bg_ref, kg_ref, bonus_ref, gate_ref)))

    @pl.when(j == 0)
    def _init():
        carry_ref[...] = jnp.zeros(carry_ref.shape, F32)

    h = _rms_rows(x_ref[...], g_ref[...]).astype(BF16)
    first_row = _iota((tm, 1), 0) == 0
    ones_bd = _head_ones(QW)
    row = _iota((tm, tm), 0)
    col = _iota((tm, tm), 1)
    tri = ((row >= col) & (row // C == col // C)).astype(BF16)

    def proj(lo, hi):
        return jnp.dot(h, w_ref[:, lo:hi], preferred_element_type=F32)

    def shifted(lo, hi):
        sh = proj(lo, hi)
        prev = jnp.where(first_row, carry_ref[:, lo:hi], pltpu.roll(sh, 1, 0))
        carry_ref[:, lo:hi] = sh[tm - 1:tm]
        return sh + (prev - sh) * mu_ref[:, lo:hi]

    def per_chunk_last(x):
        return jnp.concatenate(
            [jnp.broadcast_to(x[c * C + C - 1:c * C + C], (C, x.shape[1])) for c in range(n_ch)], axis=0)

    x_lora = shifted(3 * E, SHIFT)
    lane_l = _iota((tm, 2 * LORA), 1)
    t_lora = jnp.where(lane_l < LORA, jnp.tanh(x_lora), x_lora)
    wf = w0_ref[...] + _mm(t_lora, w2_ref[...])
    af = a0_ref[...] + _mm(x_lora, a2_ref[...])

    for q in range(E // QW):
        lo = q * QW
        sl = slice(lo, lo + QW)
        r = shifted(lo, lo + QW)
        k = shifted(E + lo, E + lo + QW)
        v = shifted(2 * E + lo, 2 * E + lo + QW)
        z = proj(SHIFT + lo, SHIFT + lo + QW)
        ld = -math.exp(-0.5) * _sigmoid(wf[:, sl])
        a = _sigmoid(af[:, sl])
        kk0 = k * kk_ref[:, sl]
        k2 = k * (1.0 + (a - 1.0) * ka_ref[:, sl])
        sums = _mm(jnp.concatenate([kk0 * kk0, r * k2 * rk_ref[:, sl]], axis=0), ones_bd)
        kk = kk0 * jnp.minimum(lax.rsqrt(sums[:tm]), 1e12)
        cum = _mm_sum_lhs(tri, ld)
        cum_last = per_chunk_last(cum)
        g_rem = jnp.exp(cum_last - cum)
        g_inv = jnp.exp(-cum)
        kka = kk * a
        quad = dict(at=-kk * jnp.exp(cum - ld), rt=r * jnp.exp(cum), bt=kka * g_inv, kt=k2 * g_inv, v=v,
                    bg=kka * g_rem, kg=k2 * g_rem, bonus=sums[tm:] * v, gate=z * _sigmoid(z))
        for n, x in quad.items():
            if n in _OPS_T:
                xt = x.astype(BF16).T
                tt = outs[n].shape[2]
                for i in range(tm // tt):
                    outs[n][i, sl, :] = xt[:, i * tt:(i + 1) * tt]
            else:
                outs[n][:, sl] = x.astype(outs[n].dtype)
        for c in range(n_ch):
            gt_ref[c, :, sl] = jnp.exp(cum[c * C + C - 1:c * C + C])

    @pl.when(j == pl.num_programs(1) - 1)
    def _shift_out():
        sho_ref[0] = carry_ref[...]


def _proj_prep(x3d, g, w_bf16, prm, *, chunk, tm, t_tile):
    b, t, d = x3d.shape
    e = prm["w0"].shape[1]
    shift = 3 * e + 2 * LORA
    cols = w_bf16.shape[1]
    assert t % tm == 0 and tm % t_tile == 0 and t_tile % chunk == 0 and e % QW == 0
    steps = t // tm
    n_ch = tm // chunk
    c2 = lambda shape: pl.BlockSpec(shape, lambda i, j: (0, 0))
    rows = pl.BlockSpec((tm, e), lambda i, j: (i * steps + j, 0))
    cols_t = pl.BlockSpec((tm // t_tile, e, t_tile), lambda i, j: (i * steps + j, 0, 0))
    outs = pl.pallas_call(
        functools.partial(_proj_prep_kernel, chunk=chunk),
        grid=(b, steps),
        in_specs=[pl.BlockSpec((tm, d), lambda i, j: (i * steps + j, 0)),
                  c2((1, d)),
                  pl.BlockSpec((d, cols), lambda i, j: (0, 0), pipeline_mode=pl.Buffered(1)),
                  c2((1, shift)), c2((1, e)), c2((2 * LORA, e)), c2((1, e)), c2((2 * LORA, e)),
                  c2((1, e)), c2((1, e)), c2((1, e))],
        out_specs=([cols_t if n in _OPS_T else rows for n in _OPS_MM + _OPS_OUT]
                   + [pl.BlockSpec((n_ch, 1, e), lambda i, j: (i * steps + j, 0, 0)),
                      pl.BlockSpec((1, 1, shift), lambda i, j: (i, 0, 0))]),
        out_shape=([jax.ShapeDtypeStruct((b * t // t_tile, e, t_tile) if n in _OPS_T else (b * t, e), BF16)
                    for n in _OPS_MM + _OPS_OUT]
                   + [jax.ShapeDtypeStruct((b * t // chunk, 1, e), F32),
                      jax.ShapeDtypeStruct((b, 1, shift), F32)]),
        scratch_shapes=[pltpu.VMEM((1, shift), F32)],
        compiler_params=pltpu.CompilerParams(dimension_semantics=("arbitrary", "arbitrary"),
                                             vmem_limit_bytes=VMEM_LIMIT),
        name="proj_prep",
    )(x3d.reshape(b * t, d), g.reshape(1, d), w_bf16,
      prm["mu"], prm["w0"], prm["w2p"], prm["a0"], prm["a2p"], prm["kk"], prm["ka"], prm["rk"])
    ops = dict(zip(_OPS_MM + _OPS_OUT + ("g_tot",), outs[:-1]))
    return ops, outs[-1].reshape(b, shift)


def _headsum(xs, ones_bd, split):
    e = xs[0].shape[1]
    nq = e // QW
    rows = [x[:, q * QW:(q + 1) * QW] for x in xs for q in range(nq)]
    stacked = jnp.concatenate(rows, axis=0)
    s = _mm_sum_rhs(stacked, ones_bd) if split else _mm(stacked, ones_bd)
    c = xs[0].shape[0]
    outs = []
    for i in range(len(xs)):
        outs.append(jnp.concatenate([s[(i * nq + q) * c:(i * nq + q + 1) * c] for q in range(nq)], axis=1))
    return outs


def _wkv_core(ops, states, lng, lnb, *, n_levels):
    C, E = ops["at"].shape
    TL = QUAD * C
    NQ = E // QW
    mask_bd = (_iota((QW, QW), 0) // HEAD) == (_iota((QW, QW), 1) // HEAD)
    ones_bd = mask_bd.astype(BF16)
    tok_t = _iota((C, TL), 0)
    tok_j = _iota((C, TL), 1) % C
    strict = tok_j < tok_t
    incl = tok_j <= tok_t
    eye_all = (tok_j == tok_t).astype(F32)

    def block_diag(x, width):
        assert 2 * width == LANES
        x = x.astype(BF16)
        zero = jnp.zeros((), BF16)
        half = _iota((C, LANES), 1) < width
        pad = jnp.zeros((C, LANES), BF16)
        rows = []
        for h in range(QUAD):
            t = h // 2
            tile = jnp.where(half if h % 2 == 0 else ~half, x[:, t * LANES:(t + 1) * LANES], zero)
            rows.append(jnp.concatenate([tile if i == t else pad for i in range(QUAD // 2)], axis=1))
        return jnp.concatenate(rows, axis=0)

    def cols_bd(xt):
        zero = jnp.zeros((HEAD, C), xt.dtype)
        return jnp.concatenate(
            [jnp.concatenate([xt[h * HEAD:(h + 1) * HEAD] if i == h else zero for i in range(QUAD)], axis=1)
             for h in range(QUAD)], axis=0)

    stack_hc = functools.partial(block_diag, width=HEAD)
    stack_tt = functools.partial(block_diag, width=C)

    at, rt, bt, kt, v, bg, kg = (ops[n] for n in _OPS_MM)
    quads = range(NQ)
    sls = [slice(q * QW, (q + 1) * QW) for q in quads]
    ars = [jnp.concatenate([at[:, sl], rt[:, sl]], axis=0) for sl in sls]
    v_sts = [stack_hc(v[:, sl]) for sl in sls]
    abs_ = [_mm(ars[q], cols_bd(bt[sls[q], :])) for q in quads]
    aks = [_mm(ars[q], cols_bd(kt[sls[q], :])) for q in quads]
    pws = [jnp.where(strict, abs_[q][:C], 0.0) for q in quads]
    a_rbs = [jnp.where(incl, abs_[q][C:], 0.0) for q in quads]
    a_aks = [jnp.where(strict, aks[q][:C], 0.0) for q in quads]
    a_rks = [jnp.where(incl, aks[q][C:], 0.0) for q in quads]
    invs = [eye_all + pws[q] for q in quads]
    pws = [_mm(pws[q], stack_tt(pws[q])) for q in quads]
    for _ in range(n_levels - 1):
        bds = [stack_tt(pws[q]) for q in quads]
        res = [_mm(jnp.concatenate([pws[q], invs[q]], axis=0), bds[q]) for q in quads]
        pws = [res[q][:C] for q in quads]
        invs = [invs[q] + res[q][C:] for q in quads]
    invs = [invs[q] + _mm(invs[q], stack_tt(pws[q])) for q in quads]
    arss = [_mm(ars[q], states[q], _NT) for q in quads]
    avs = [_mm(jnp.concatenate([a_aks[q], a_rks[q]], axis=0), v_sts[q]) for q in quads]
    ws = [arss[q][:C] + avs[q][:C] for q in quads]
    us = [_mm(invs[q], stack_hc(ws[q])) for q in quads]
    ys = [arss[q][C:] + avs[q][C:] + _mm(a_rbs[q], stack_hc(us[q])) for q in quads]
    new_states = []
    for q in quads:
        uv = jnp.concatenate([us[q].astype(BF16), v[:, sls[q]]], axis=0)
        bk = jnp.concatenate([bg[:, sls[q]], kg[:, sls[q]]], axis=0)
        new_states.append(states[q] * ops["g_tot"][:, sls[q]] + jnp.where(mask_bd, _mm(uv, bk, _TN), 0.0))
    y = jnp.concatenate(ys, axis=1)

    (ysum,) = _headsum([y], ones_bd, split=True)
    yc = y - ysum * (1.0 / HEAD)
    (vsum,) = _headsum([yc * yc], ones_bd, split=False)
    yn = yc * lax.rsqrt(vsum * (1.0 / HEAD) + GN_EPS) * lng + lnb
    return (yn + ops["bonus"]) * ops["gate"], new_states


def _wkv_chunks_kernel(at_ref, rt_ref, bt_ref, kt_ref, v_ref, bg_ref, kg_ref, bonus_ref, gate_ref, gt_ref,
                       lng_ref, lnb_ref, o_ref, so_ref, st_ref, *, chunk, n_levels):
    j = pl.program_id(1)
    n_seq = at_ref.shape[1]
    NQ = st_ref.shape[0] // n_seq
    E = NQ * QW
    seqs = range(n_seq)
    refs = dict(zip(_OPS_MM + _OPS_OUT,
                    (at_ref, rt_ref, bt_ref, kt_ref, v_ref, bg_ref, kg_ref, bonus_ref, gate_ref)))

    @pl.when(j == 0)
    def _init():
        st_ref[...] = jnp.zeros(st_ref.shape, F32)

    lng = jnp.concatenate([lng_ref[...]] * n_seq, axis=1)
    lnb = jnp.concatenate([lnb_ref[...]] * n_seq, axis=1)
    states = [st_ref[q] for q in range(n_seq * NQ)]
    for c in range(at_ref.shape[2] // chunk):
        rows = slice(c * chunk, (c + 1) * chunk)
        ops = {}
        for n, ref in refs.items():
            if n in _OPS_T:
                ops[n] = jnp.concatenate([ref[0, s, 0, :, rows] for s in seqs], axis=0)
            else:
                ops[n] = jnp.concatenate([ref[0, s, rows, :] for s in seqs], axis=1)
        ops["g_tot"] = jnp.concatenate([gt_ref[0, s, c] for s in seqs], axis=1)
        o, states = _wkv_core(ops, states, lng, lnb, n_levels=n_levels)
        for s in seqs:
            o_ref[0, s, rows, :] = o[:, s * E:(s + 1) * E].astype(o_ref.dtype)
    for q in range(n_seq * NQ):
        st_ref[q] = states[q]

    @pl.when(j == pl.num_programs(1) - 1)
    def _state_out():
        for s in seqs:
            for q in range(NQ):
                for h in range(QUAD):
                    so_ref[0, s, QUAD * q + h] = states[s * NQ + q][h * HEAD:(h + 1) * HEAD, h * HEAD:(h + 1) * HEAD]


def _wkv_chunks(ops, lng, lnb, *, seq_len, chunk, chunks_per_step, n_seq):
    n, e = ops["at"].shape
    b = n // seq_len
    nh = e // HEAD
    tm = chunk * chunks_per_step
    steps = seq_len // tm
    assert ops["bt"].shape[2] == tm
    assert b % n_seq == 0
    g = b // n_seq
    rows = pl.BlockSpec((1, n_seq, tm, e), lambda i, j: (i, 0, j, 0))
    cols_t = pl.BlockSpec((1, n_seq, 1, e, tm), lambda i, j: (i, 0, j, 0, 0))
    c2 = lambda shape: pl.BlockSpec(shape, lambda i, j: (0, 0))
    sblk = pl.BlockSpec((1, n_seq, nh, HEAD, HEAD), lambda i, j: (i, 0, 0, 0, 0))
    n_levels = max(1, math.ceil(math.log2(chunk)) - 1)
    args = [ops[n_].reshape((g, n_seq, steps, e, tm) if n_ in _OPS_T else (g, n_seq, seq_len, e))
            for n_ in _OPS_MM + _OPS_OUT]
    o, so = pl.pallas_call(
        functools.partial(_wkv_chunks_kernel, chunk=chunk, n_levels=n_levels),
        grid=(g, steps),
        in_specs=([cols_t if n_ in _OPS_T else rows for n_ in _OPS_MM + _OPS_OUT]
                  + [pl.BlockSpec((1, n_seq, chunks_per_step, 1, e), lambda i, j: (i, 0, j, 0, 0)),
                     c2((1, e)), c2((1, e))]),
        out_specs=[rows, sblk],
        out_shape=[jax.ShapeDtypeStruct((g, n_seq, seq_len, e), BF16),
                   jax.ShapeDtypeStruct((g, n_seq, nh, HEAD, HEAD), F32)],
        scratch_shapes=[pltpu.VMEM((n_seq * (e // QW), QW, QW), F32)],
        compiler_params=pltpu.CompilerParams(dimension_semantics=("arbitrary", "arbitrary"),
                                             vmem_limit_bytes=VMEM_LIMIT),
        name="wkv_chunks",
    )(*args, ops["g_tot"].reshape(g, n_seq, seq_len // chunk, 1, e), lng, lnb)
    return o.reshape(n, e), so.reshape(b, nh, HEAD, HEAD)


_VT = ("nkk", "w", "b", "k", "r", "v")


def _wkv_seq_kernel(pr_ref, pk_ref, pv_ref, pl_ref, pz_ref, sr_ref, sk_ref, sv_ref, sl_ref,
                    mur_ref, muk_ref, muv_ref, mul_ref, w0_ref, w2_ref, a0_ref, a2_ref, kk_ref, ka_ref, rk_ref,
                    lng_ref, lnb_ref, s_ref, o_ref, so_ref, vt_ref, yt_ref):
    T, B, _ = pr_ref.shape
    ones2 = _head_ones(HPAIR)

    def shifted(p_ref, prev_ref, mu_ref):
        sh = p_ref[...].reshape(T * B, HPAIR)
        prev = jnp.concatenate([prev_ref[...], sh[:(T - 1) * B]], axis=0)
        return sh + (prev - sh) * mu_ref[...]

    r = shifted(pr_ref, sr_ref, mur_ref)
    k = shifted(pk_ref, sk_ref, muk_ref)
    v = shifted(pv_ref, sv_ref, muv_ref)
    x_lora = shifted(pl_ref, sl_ref, mul_ref)
    z = pz_ref[...].reshape(T * B, HPAIR)
    lane_l = _iota((T * B, 2 * LORA), 1)
    t_lora = jnp.where(lane_l < LORA, jnp.tanh(x_lora), x_lora)
    wf = w0_ref[...] + _mm(t_lora, w2_ref[...])
    af = a0_ref[...] + _mm(x_lora, a2_ref[...])
    w = jnp.exp(-math.exp(-0.5) * _sigmoid(wf))
    a = _sigmoid(af)
    kk0 = k * kk_ref[...]
    k2 = k * (1.0 + (a - 1.0) * ka_ref[...])
    sums = _mm_sum_rhs(jnp.concatenate([kk0 * kk0, r * k2 * rk_ref[...]], axis=0), ones2)
    kk = kk0 * jnp.minimum(lax.rsqrt(sums[:T * B]), 1e12)
    bonus = sums[T * B:] * v
    vecs = dict(nkk=-kk, w=w, b=kk * a, k=k2, r=r, v=v)
    for i, n in enumerate(_VT):
        for t in range(T):
            vt_ref[i, t] = vecs[n][t * B:(t + 1) * B].T

    for h in range(2):
        ch = slice(h * HEAD, (h + 1) * HEAD)

        def body(g, carry, h=h, ch=ch):
            v0 = pl.multiple_of(g * SUBLANES, SUBLANES)
            vrows = [vt_ref[5, t, pl.ds(h * HEAD + v0, SUBLANES), :] for t in range(T)]
            ys = [[] for _ in range(T)]
            for u in range(SUBLANES):
                s = s_ref[h, v0 + u]
                for t in range(T):
                    sa = jnp.sum(s * vt_ref[0, t, ch, :], axis=0, keepdims=True)
                    s = s * vt_ref[1, t, ch, :] + sa * vt_ref[2, t, ch, :] + vrows[t][u:u + 1] * vt_ref[3, t, ch, :]
                    ys[t].append(jnp.sum(s * vt_ref[4, t, ch, :], axis=0, keepdims=True))
                so_ref[h, v0 + u] = s
            for t in range(T):
                yt_ref[t, pl.ds(h * HEAD + v0, SUBLANES), :] = jnp.concatenate(ys[t], axis=0)
            return carry

        lax.fori_loop(0, HEAD // SUBLANES, body, 0)

    y = jnp.concatenate([yt_ref[t].T for t in range(T)], axis=0)
    yc = y - _mm_sum_rhs(y, ones2) * (1.0 / HEAD)
    var = _mm_sum_rhs(yc * yc, ones2) * (1.0 / HEAD)
    yn = yc * lax.rsqrt(var + GN_EPS) * lng_ref[...] + lnb_ref[...]
    o_ref[...] = ((yn + bonus) * (z * _sigmoid(z))).reshape(T, B, HPAIR)


def _wkv_seq(proj_t, shift_prev, s_t, prm):
    t, b, cols = proj_t.shape
    e = prm["w0"].shape[1]
    nh = e // HEAD
    kb = e // HPAIR
    bt = LANES
    assert b % bt == 0 and nh % 2 == 0
    off = dict(r=0, k=kb, v=2 * kb, l=3 * kb)
    pspec = lambda o: pl.BlockSpec((t, bt, HPAIR), lambda i, j, o=o: (0, j, o + i))
    pfix = lambda o: pl.BlockSpec((t, bt, HPAIR), lambda i, j, o=o: (0, j, o))
    sspec = lambda o: pl.BlockSpec((bt, HPAIR), lambda i, j, o=o: (j, o + i))
    sfix = lambda o: pl.BlockSpec((bt, HPAIR), lambda i, j, o=o: (j, o))
    mspec = lambda o: pl.BlockSpec((1, HPAIR), lambda i, j, o=o: (0, o + i))
    mfix = lambda o: pl.BlockSpec((1, HPAIR), lambda i, j, o=o: (0, o))
    row = pl.BlockSpec((1, HPAIR), lambda i, j: (0, i))
    lora = pl.BlockSpec((2 * LORA, HPAIR), lambda i, j: (0, i))
    sblk = pl.BlockSpec((2, HEAD, HEAD, bt), lambda i, j: (i, 0, 0, j))
    return pl.pallas_call(
        _wkv_seq_kernel,
        grid=(nh // 2, b // bt),
        in_specs=[pspec(off["r"]), pspec(off["k"]), pspec(off["v"]), pfix(off["l"]), pspec(off["l"] + 1),
                  sspec(off["r"]), sspec(off["k"]), sspec(off["v"]), sfix(off["l"]),
                  mspec(off["r"]), mspec(off["k"]), mspec(off["v"]), mfix(off["l"]),
                  row, lora, row, lora, row, row, row, row, row, sblk],
        out_specs=[pl.BlockSpec((t, bt, HPAIR), lambda i, j: (0, j, i)), sblk],
        out_shape=[jax.ShapeDtypeStruct((t, b, e), F32), jax.ShapeDtypeStruct(s_t.shape, F32)],
        scratch_shapes=[pltpu.VMEM((len(_VT), t, HPAIR, bt), F32),
                        pltpu.VMEM((t, HPAIR, bt), F32)],
        compiler_params=pltpu.CompilerParams(dimension_semantics=("arbitrary", "arbitrary"),
                                             vmem_limit_bytes=VMEM_LIMIT),
        name="wkv_seq",
    )(proj_t, proj_t, proj_t, proj_t, proj_t, shift_prev, shift_prev, shift_prev, shift_prev,
      prm["mu"], prm["mu"], prm["mu"], prm["mu"], prm["w0"], prm["w2p"], prm["a0"], prm["a2p"],
      prm["kk"], prm["ka"], prm["rk"], prm["lng"], prm["lnb"], s_t)


def _out_residual_kernel(o_ref, w_ref, x_ref, y_ref):
    y_ref[...] = x_ref[...] + jnp.dot(o_ref[...].astype(BF16), w_ref[...], preferred_element_type=F32)


def _out_residual(o2d, w_bf16, x2d, tm):
    n, e = o2d.shape
    d = x2d.shape[1]
    return pl.pallas_call(
        _out_residual_kernel,
        grid=(n // tm,),
        in_specs=[pl.BlockSpec((tm, e), lambda i: (i, 0)),
                  pl.BlockSpec((e, d), lambda i: (0, 0)),
                  pl.BlockSpec((tm, d), lambda i: (i, 0))],
        out_specs=pl.BlockSpec((tm, d), lambda i: (i, 0)),
        out_shape=jax.ShapeDtypeStruct((n, d), F32),
        compiler_params=pltpu.CompilerParams(dimension_semantics=("arbitrary",), vmem_limit_bytes=VMEM_LIMIT),
        name="out_residual",
    )(o2d, w_bf16, x2d)


def _gmlp_kernel(x_ref, g_ref, win_ref, vg_ref, vb_ref, wm_ref, bs_ref, wout_ref, nf_ref, *rest, emit_v, add_mixer):
    out_refs = rest[2:] if add_mixer else rest
    y_ref = out_refs[0]
    tm = x_ref.shape[0]
    e = vg_ref.shape[1]
    gd = e // GM_GROUPS
    x = x_ref[...]
    if add_mixer:
        x = x + jnp.dot(rest[0][...], rest[1][...], preferred_element_type=F32)
    h = _rms_rows(x, g_ref[...]).astype(BF16)
    proj = jnp.dot(h, win_ref[...], preferred_element_type=F32)
    u = jax.nn.gelu(proj[:, :e])
    vf = jax.nn.gelu(proj[:, e:2 * e])
    z = proj[:, 2 * e:]
    vm = jnp.mean(vf, axis=-1, keepdims=True)
    vc = vf - vm
    vv = jnp.mean(vc * vc, axis=-1, keepdims=True)
    vn = vc * lax.rsqrt(vv + LN_EPS) * vg_ref[...] + vb_ref[...]
    if emit_v:
        out_refs[1][...] = vn
    vn16 = vn.astype(BF16)
    causal = _iota((GM_CHUNK, GM_CHUNK), 0) >= _iota((GM_CHUNK, GM_CHUNK), 1)
    rows = []
    for j in range(tm // GM_CHUNK):
        cols = []
        for gi in range(GM_GROUPS):
            wm = jnp.where(causal, wm_ref[gi], 0.0).astype(BF16)
            blk = vn16[j * GM_CHUNK:(j + 1) * GM_CHUNK, gi * gd:(gi + 1) * gd]
            cols.append(jnp.dot(wm, blk, preferred_element_type=F32) + bs_ref[gi])
        rows.append(jnp.concatenate(cols, axis=1))
    mixed = jnp.concatenate(rows, axis=0)
    o = u * mixed * (z * _sigmoid(z))
    x2 = x + jnp.dot(o.astype(BF16), wout_ref[...], preferred_element_type=F32)
    y_ref[...] = _rms_rows(x2, nf_ref[...])


def _gmlp(x2d, g, win_bf16, vg, vb, wmix, bias, wout_bf16, nf, *, tm, emit_v, mixer=None):
    n, d = x2d.shape
    e = vg.shape[0]
    const2 = lambda shape: pl.BlockSpec(shape, lambda i: (0, 0))
    extra_specs, extra_args = [], []
    if mixer is not None:
        extra_specs = [pl.BlockSpec((tm, mixer[0].shape[1]), lambda i: (i, 0)),
                       pl.BlockSpec(mixer[1].shape, lambda i: (0, 0), pipeline_mode=pl.Buffered(1))]
        extra_args = list(mixer)
    out_shape = [jax.ShapeDtypeStruct((n, d), F32)]
    out_specs = [pl.BlockSpec((tm, d), lambda i: (i, 0))]
    if emit_v:
        out_shape.append(jax.ShapeDtypeStruct((n, e), F32))
        out_specs.append(pl.BlockSpec((tm, e), lambda i: (i, 0)))
    outs = pl.pallas_call(
        functools.partial(_gmlp_kernel, emit_v=emit_v, add_mixer=mixer is not None),
        grid=(n // tm,),
        in_specs=[pl.BlockSpec((tm, d), lambda i: (i, 0)),
                  const2((1, d)),
                  pl.BlockSpec((d, 3 * e), lambda i: (0, 0), pipeline_mode=pl.Buffered(1)),
                  const2((1, e)), const2((1, e)),
                  pl.BlockSpec((GM_GROUPS, GM_CHUNK, GM_CHUNK), lambda i: (0, 0, 0)),
                  pl.BlockSpec((GM_GROUPS, GM_CHUNK, 1), lambda i: (0, 0, 0)),
                  pl.BlockSpec((e, d), lambda i: (0, 0), pipeline_mode=pl.Buffered(1)),
                  const2((1, d))] + extra_specs,
        out_specs=out_specs,
        out_shape=out_shape,
        compiler_params=pltpu.CompilerParams(dimension_semantics=("arbitrary",), vmem_limit_bytes=VMEM_LIMIT),
        name="gmlp",
    )(x2d, g.reshape(1, d), win_bf16, vg.reshape(1, e), vb.reshape(1, e), wmix,
      bias.reshape(GM_GROUPS, GM_CHUNK, 1), wout_bf16, nf.reshape(1, d), *extra_args)
    return outs


def _row_tile(n, pref):
    t = pref
    while n % t:
        t //= 2
    return t


def kernel(x_prompt, x_sample, state_shift, state_wkv, norm_g, norm_f, rw_in, rw_mu, rw_w0, rw_w2, rw_a0, rw_a2, rw_kk, rw_ka, rw_rk, rw_lnx_g, rw_lnx_b, rw_out, gm_in, gm_vg, gm_vb, gm_ws, gm_bs, gm_out):
    bp, tp, d = x_prompt.shape
    bs_, ts, _ = x_sample.shape
    e = rw_w0.shape[1]
    shift = 3 * e + 2 * LORA
    xp = x_prompt.reshape(bp * tp, d)

    zeros_l = jnp.zeros((LORA, e), F32)
    prm = dict(
        mu=rw_mu[0].reshape(1, shift), w0=rw_w0[0].reshape(1, e), a0=rw_a0[0].reshape(1, e),
        w2p=jnp.concatenate([rw_w2[0], zeros_l], axis=0).astype(BF16),
        a2p=jnp.concatenate([zeros_l, rw_a2[0]], axis=0).astype(BF16),
        kk=rw_kk[0].reshape(1, e), ka=rw_ka[0].reshape(1, e), rk=rw_rk[0].reshape(1, e),
        lng=rw_lnx_g[0].reshape(1, e), lnb=rw_lnx_b[0].reshape(1, e))
    w_out = rw_out[0].astype(BF16)

    xs_t = jnp.transpose(x_sample, (1, 0, 2)).reshape(ts * bs_, d)
    proj_s, w_in = _norm_proj(xs_t, norm_g[0], rw_in[0], PROJ_COL_TILES)
    proj_s = proj_s.reshape(ts, bs_, shift + e)

    step_rows = WKV_STEP_CHUNKS * WKV_CHUNK
    assert tp % step_rows == 0 and tp % GM_CHUNK == 0
    ops_p, shift_p = _proj_prep(x_prompt, norm_g[0], w_in, prm,
                                chunk=WKV_CHUNK, tm=max(step_rows, _row_tile(tp, PREP_ROWS)), t_tile=step_rows)
    o_p, wkv_p = _wkv_chunks(ops_p, prm["lng"], prm["lnb"],
                             seq_len=tp, chunk=WKV_CHUNK, chunks_per_step=WKV_STEP_CHUNKS,
                             n_seq=_row_tile(bp, WKV_STEP_SEQS))

    o_s, wkv_s_t = _wkv_seq(proj_s, state_shift[0], jnp.transpose(state_wkv[0], (1, 2, 3, 0)), prm)
    wkv_s = jnp.transpose(wkv_s_t, (3, 0, 1, 2))
    shift_s = proj_s[ts - 1, :, :shift]
    x1s_t = _out_residual(o_s.reshape(ts * bs_, e), w_out, xs_t, _row_tile(bs_ * ts, OUT_ROWS))
    x1s = jnp.transpose(x1s_t.reshape(ts, bs_, d), (1, 0, 2)).reshape(bs_ * ts, d)

    g_in = gm_in[0].astype(BF16)
    g_out = gm_out[0].astype(BF16)
    wm_p = gm_ws[0]
    reps = GM_CHUNK // ts
    eye = jnp.eye(reps, dtype=F32)
    wm_s = jax.vmap(lambda w: jnp.kron(eye, w))(wm_p[:, :ts, :ts])
    bias_s = jnp.tile(gm_bs[0][:, :ts], (1, reps))

    (y_p,) = _gmlp(xp, norm_g[1], g_in, gm_vg[0], gm_vb[0], wm_p, gm_bs[0], g_out, norm_f,
                   tm=_row_tile(bp * tp, GMLP_ROWS), emit_v=False, mixer=(o_p, w_out))
    y_s, v_s = _gmlp(x1s, norm_g[1], g_in, gm_vg[0], gm_vb[0], wm_s, bias_s, g_out, norm_f,
                     tm=_row_tile(bs_ * ts, GMLP_ROWS_SMALL), emit_v=True)

    return (y_p.reshape(bp, tp, d), y_s.reshape(bs_, ts, d),
            shift_p[None], wkv_p[None], shift_s[None], wkv_s[None],
            v_s.reshape(1, bs_, ts, e))
```
